```python
import math
import jax, jax.numpy as jnp
from jax import lax
import numpy as np

D_MODEL = 1024
BATCH = 8
SEQ = 4096
DEPTH = 2

MIX_WIDTH = D_MODEL
LRU_WIDTH = MIX_WIDTH // 2
LRU_BLOCKS = 8
LRU_BLOCK_DIM = LRU_WIDTH // LRU_BLOCKS
CONV_WIDTH = 4
LRU_C = 8.0
N_HEADS = 8
HEAD_DIM = (MIX_WIDTH - LRU_WIDTH) // N_HEADS
N_KV = 2
Q_PER_KV = N_HEADS // N_KV
NSA_WIDTH = N_HEADS * HEAD_DIM
KV_WIDTH = N_KV * HEAD_DIM
N_BRANCHES = 3
CMP_BLOCK = 32
CMP_STRIDE = 16
SEL_BLOCK = 64
SEL_TOPN = 16
WINDOW = 512
Q_BLOCK = 128
FORCE_BONUS = 1e4
NEG = -1e30
PEER_HEADS = 8
PEER_KEYS = 128
PEER_TOPK = 16
PEER_KEY_DIM = 128
N_EXPERTS = PEER_KEYS * PEER_KEYS
PEER_CHUNK = 512
IN_SPLITS = (LRU_WIDTH, LRU_WIDTH, NSA_WIDTH, KV_WIDTH, KV_WIDTH, KV_WIDTH, KV_WIDTH, KV_WIDTH, KV_WIDTH, N_HEADS * N_BRANCHES)
IN_COLS = sum(IN_SPLITS)
ALPHA = (2 * DEPTH) ** 0.25
BETA = (8 * DEPTH) ** -0.25
LN_EPS = 1e-5

kernel_name = 'hybrid_rglru_nsa_peer_deepnorm'


def _layer_norm(x, g, b):
    xf = x.astype(jnp.float32)
    mu = jnp.mean(xf, -1, keepdims=True)
    var = jnp.mean(jnp.square(xf - mu), -1, keepdims=True)
    return ((xf - mu) * lax.rsqrt(var + LN_EPS) * g.astype(jnp.float32) + b.astype(jnp.float32)).astype(x.dtype)


def _rms_norm(x, g):
    xf = x.astype(jnp.float32)
    return (xf * lax.rsqrt(jnp.mean(jnp.square(xf), -1, keepdims=True) + LN_EPS) * g.astype(jnp.float32)).astype(x.dtype)


def _alibi_slopes(n):
    return jnp.asarray(np.array([2.0 ** (-8.0 * (h + 1) / n) for h in range(n)], np.float32))


def _rg_lru(xb, gate, conv_w, conv_b, wa, ba, wx, bx, lam):
    B, S, C = xb.shape
    xc = lax.conv_general_dilated(xb, conv_w[:, None, :], window_strides=(1,), padding=[(CONV_WIDTH - 1, 0)],
                                  dimension_numbers=('NWC', 'WIO', 'NWC'), feature_group_count=C) + conv_b
    xr = xc.reshape(B, S, LRU_BLOCKS, LRU_BLOCK_DIM)
    r = jax.nn.sigmoid(jnp.einsum('bsnd,nde->bsne', xr, wa) + ba.reshape(LRU_BLOCKS, LRU_BLOCK_DIM)).reshape(B, S, C)
    i = jax.nn.sigmoid(jnp.einsum('bsnd,nde->bsne', xr, wx) + bx.reshape(LRU_BLOCKS, LRU_BLOCK_DIM)).reshape(B, S, C)
    log_a = -LRU_C * r.astype(jnp.float32) * jax.nn.softplus(-lam.astype(jnp.float32))
    a = jnp.exp(log_a)
    u = jnp.sqrt(-jnp.expm1(2.0 * log_a)) * (i * xc).astype(jnp.float32)

    def combine(left, right):
        a1, b1 = left
        a2, b2 = right
        return a1 * a2, a2 * b1 + b2

    _, h = lax.associative_scan(combine, (a, u), axis=1)
    return h.astype(xb.dtype) * jax.nn.gelu(gate)


def _compress(k, pos, w1, b1, w2, b2):
    B, S, G, hd = k.shape
    nc = (S - CMP_BLOCK) // CMP_STRIDE + 1
    idx = np.arange(nc)[:, None] * CMP_STRIDE + np.arange(CMP_BLOCK)[None, :]
    blk = k[:, idx] + pos[None, None, :, None, :]
    flat = blk.transpose(0, 1, 3, 2, 4).reshape(B, nc, G, CMP_BLOCK * hd)
    return jax.nn.gelu(flat @ w1 + b1) @ w2 + b2


def _selection_matrix(n_cmp, n_sel):
    cs = np.arange(n_cmp)[:, None] * CMP_STRIDE
    ss = np.arange(n_sel)[None, :] * SEL_BLOCK
    ov = np.clip(np.minimum(cs + CMP_BLOCK, ss + SEL_BLOCK) - np.maximum(cs, ss), 0, None)
    return (ov / CMP_STRIDE).astype(np.float32)


def _nsa(q, kc, vc, ks, vs, kw, vw, gates):
    B, S = q.shape[:2]
    G, R, hd = N_KV, Q_PER_KV, HEAD_DIM
    nb = S // Q_BLOCK
    qb_all = (q * (hd ** -0.5)).reshape(B, nb, Q_BLOCK, G, R, hd).transpose(1, 0, 2, 3, 4, 5)
    g_all = jax.nn.sigmoid(gates).reshape(B, nb, Q_BLOCK, G, R, N_BRANCHES).transpose(1, 0, 2, 3, 4, 5)
    slopes = _alibi_slopes(N_HEADS).reshape(G, R)
    nc = kc.shape[1]
    cmp_end = jnp.arange(nc) * CMP_STRIDE + CMP_BLOCK - 1
    n_sel = S // SEL_BLOCK
    n_top = min(SEL_TOPN, n_sel)
    sel_mat = jnp.asarray(_selection_matrix(nc, n_sel))
    ks_blk = ks.transpose(0, 2, 1, 3).reshape(B, G, n_sel, SEL_BLOCK, hd)
    vs_blk = vs.transpose(0, 2, 1, 3).reshape(B, G, n_sel, SEL_BLOCK, hd)
    kw_pad = jnp.pad(kw, ((0, 0), (WINDOW, 0), (0, 0), (0, 0)))
    vw_pad = jnp.pad(vw, ((0, 0), (WINDOW, 0), (0, 0), (0, 0)))
    gather = jax.vmap(jax.vmap(lambda blk, ix: blk[ix]))
    blk_id = jnp.arange(n_sel)

    def one_block(args):
        qb, q_blk, g_blk = args
        t = qb * Q_BLOCK + jnp.arange(Q_BLOCK)
        s = jnp.einsum('bqgrd,bcgd->bgrqc', q_blk, kc).astype(jnp.float32)
        dist_c = (t[:, None] - cmp_end[None, :]).astype(jnp.float32)
        s = s - slopes[:, :, None, None] * jnp.abs(dist_c)
        valid_c = cmp_end[None, :] <= t[:, None]
        p_c = jax.nn.softmax(jnp.where(valid_c, s, NEG), axis=-1) * valid_c.any(-1)[:, None].astype(jnp.float32)
        o_c = jnp.einsum('bgrqc,bcgd->bqgrd', p_c.astype(vc.dtype), vc)
        imp = jnp.einsum('bgrqc,cj->bgqj', p_c, sel_mat)
        cur = t // SEL_BLOCK
        vblk = blk_id[None, :] <= cur[:, None]
        forced = (blk_id[None, :] == 0) | (blk_id[None, :] == cur[:, None]) | (blk_id[None, :] == cur[:, None] - 1)
        imp = jnp.where(vblk, imp + jnp.where(forced, FORCE_BONUS, 0.0), NEG)
        _, idx = lax.top_k(imp, n_top)
        k_sel = gather(ks_blk, idx)
        v_sel = gather(vs_blk, idx)
        tok = idx[..., None] * SEL_BLOCK + jnp.arange(SEL_BLOCK)
        dist_s = (t[None, None, :, None, None] - tok)[:, :, None]
        s = jnp.einsum('bqgrd,bgqnld->bgrqnl', q_blk, k_sel).astype(jnp.float32)
        s = s - slopes[None, :, :, None, None, None] * jnp.abs(dist_s).astype(jnp.float32)
        s = jnp.where(dist_s >= 0, s, NEG)
        p_s = jax.nn.softmax(s.reshape(s.shape[:4] + (n_top * SEL_BLOCK,)), axis=-1).reshape(s.shape)
        o_s = jnp.einsum('bgrqnl,bgqnld->bqgrd', p_s.astype(v_sel.dtype), v_sel)
        k_w = lax.dynamic_slice_in_dim(kw_pad, qb * Q_BLOCK, WINDOW + Q_BLOCK, axis=1)
        v_w = lax.dynamic_slice_in_dim(vw_pad, qb * Q_BLOCK, WINDOW + Q_BLOCK, axis=1)
        pos = qb * Q_BLOCK - WINDOW + jnp.arange(WINDOW + Q_BLOCK)
        dist_w = t[:, None] - pos[None, :]
        valid_w = (dist_w >= 0) & (dist_w < WINDOW) & (pos >= 0)[None, :]
        s = jnp.einsum('bqgrd,bkgd->bgrqk', q_blk, k_w).astype(jnp.float32)
        s = s - slopes[:, :, None, None] * dist_w.astype(jnp.float32)
        p_w = jax.nn.softmax(jnp.where(valid_w, s, NEG), axis=-1)
        o_w = jnp.einsum('bgrqk,bkgd->bqgrd', p_w.astype(v_w.dtype), v_w)
        return g_blk[..., 0:1] * o_c + g_blk[..., 1:2] * o_s + g_blk[..., 2:3] * o_w

    out = lax.map(one_block, (jnp.arange(nb), qb_all, g_all))
    return out.transpose(1, 0, 2, 3, 4, 5).reshape(B, S, N_HEADS * hd)


def _peer(x, wq, subkeys, u_tab, v_tab):
    B, S, D = x.shape
    T = B * S
    xf = x.reshape(T, D)
    q = (xf @ wq).reshape(T, PEER_HEADS, 2, PEER_KEY_DIM)
    s = jnp.einsum('thcd,ckd->thck', q, subkeys).astype(jnp.float32)
    sv, si = lax.top_k(s, PEER_TOPK)
    cand = (sv[:, :, 0, :, None] + sv[:, :, 1, None, :]).reshape(T, PEER_HEADS, PEER_TOPK * PEER_TOPK)
    cv, ci = lax.top_k(cand, PEER_TOPK)
    ea = jnp.take_along_axis(si[:, :, 0], ci // PEER_TOPK, axis=-1)
    eb = jnp.take_along_axis(si[:, :, 1], ci % PEER_TOPK, axis=-1)
    experts = (ea * PEER_KEYS + eb).reshape(T, PEER_HEADS * PEER_TOPK)
    gate = jax.nn.softmax(cv, axis=-1).reshape(T, PEER_HEADS * PEER_TOPK).astype(x.dtype)
    chunk = math.gcd(T, PEER_CHUNK)
    nch = T // chunk

    def one_chunk(args):
        xc, ec, gc = args
        act = jax.nn.gelu(jnp.einsum('cd,ced->ce', xc, u_tab[ec]))
        return jnp.einsum('ce,ced->cd', act * gc, v_tab[ec])

    y = lax.map(one_chunk, (xf.reshape(nch, chunk, D), experts.reshape(nch, chunk, -1), gate.reshape(nch, chunk, -1)))
    return y.reshape(B, S, D)


def setup_inputs(seed: int = 0) -> dict:
    key = jax.random.key(seed)
    ks = jax.random.split(key, 40)
    L = DEPTH

    def nrm(k, shape, scale):
        return jax.random.normal(k, shape, jnp.float32) * scale

    col_scale = np.concatenate([np.full(n, s, np.float32) for n, s in zip(
        IN_SPLITS, (BETA, 1.0, 1.0, 1.0, BETA, 1.0, BETA, 1.0, BETA, 1.0))])
    u = jax.random.uniform(ks[9], (L, LRU_WIDTH), jnp.float32, 0.9, 0.999)
    a0 = u ** (1.0 / LRU_C)
    fan_c = CMP_BLOCK * HEAD_DIM
    return {
        'x': nrm(ks[0], (BATCH, SEQ, D_MODEL), 1.0),
        'w_in': nrm(ks[1], (L, D_MODEL, IN_COLS), D_MODEL ** -0.5) * jnp.asarray(col_scale),
        'b_in': nrm(ks[2], (L, IN_COLS), 0.01),
        'conv_w': nrm(ks[3], (L, CONV_WIDTH, LRU_WIDTH), CONV_WIDTH ** -0.5),
        'conv_b': nrm(ks[4], (L, LRU_WIDTH), 0.01),
        'lru_wa': nrm(ks[5], (L, LRU_BLOCKS, LRU_BLOCK_DIM, LRU_BLOCK_DIM), LRU_BLOCK_DIM ** -0.5),
        'lru_ba': nrm(ks[6], (L, LRU_WIDTH), 0.01),
        'lru_wx': nrm(ks[7], (L, LRU_BLOCKS, LRU_BLOCK_DIM, LRU_BLOCK_DIM), LRU_BLOCK_DIM ** -0.5),
        'lru_bx': nrm(ks[8], (L, LRU_WIDTH), 0.01),
        'lru_lambda': jnp.log(a0) - jnp.log1p(-a0),
        'cmp_pos_k': nrm(ks[10], (L, CMP_BLOCK, HEAD_DIM), 0.02),
        'cmpk_w1': nrm(ks[11], (L, fan_c, HEAD_DIM), fan_c ** -0.5),
        'cmpk_b1': nrm(ks[12], (L, HEAD_DIM), 0.01),
        'cmpk_w2': nrm(ks[13], (L, HEAD_DIM, HEAD_DIM), HEAD_DIM ** -0.5),
        'cmpk_b2': nrm(ks[14], (L, HEAD_DIM), 0.01),
        'cmp_pos_v': nrm(ks[15], (L, CMP_BLOCK, HEAD_DIM), 0.02),
        'cmpv_w1': nrm(ks[16], (L, fan_c, HEAD_DIM), fan_c ** -0.5),
        'cmpv_b1': nrm(ks[17], (L, HEAD_DIM), 0.01),
        'cmpv_w2': nrm(ks[18], (L, HEAD_DIM, HEAD_DIM), HEAD_DIM ** -0.5),
        'cmpv_b2': nrm(ks[19], (L, HEAD_DIM), 0.01),
        'gn_lru_g': 1.0 + nrm(ks[20], (L, LRU_WIDTH), 0.02),
        'gn_nsa_g': 1.0 + nrm(ks[21], (L, NSA_WIDTH), 0.02),
        'w_out': nrm(ks[22], (L, MIX_WIDTH, D_MODEL), BETA * MIX_WIDTH ** -0.5),
        'ln1_g': 1.0 + nrm(ks[23], (L, D_MODEL), 0.02),
        'ln1_b': nrm(ks[24], (L, D_MODEL), 0.02),
        'peer_wq': nrm(ks[25], (L, D_MODEL, PEER_HEADS * 2 * PEER_KEY_DIM), D_MODEL ** -0.5),
        'peer_subkeys': nrm(ks[26], (L, 2, PEER_KEYS, PEER_KEY_DIM), PEER_KEY_DIM ** -0.5),
        'peer_u': nrm(ks[27], (L, N_EXPERTS, D_MODEL), D_MODEL ** -0.5),
        'peer_v': nrm(ks[28], (L, N_EXPERTS, D_MODEL), BETA * PEER_HEADS ** -0.5),
        'ln2_g': 1.0 + nrm(ks[29], (L, D_MODEL), 0.02),
        'ln2_b': nrm(ks[30], (L, D_MODEL), 0.02),
    }


def reference(x, w_in, b_in, conv_w, conv_b, lru_wa, lru_ba, lru_wx, lru_bx, lru_lambda,
              cmp_pos_k, cmpk_w1, cmpk_b1, cmpk_w2, cmpk_b2,
              cmp_pos_v, cmpv_w1, cmpv_b1, cmpv_w2, cmpv_b2,
              gn_lru_g, gn_nsa_g, w_out, ln1_g, ln1_b,
              peer_wq, peer_subkeys, peer_u, peer_v, ln2_g, ln2_b):
    B, S, _ = x.shape
    offsets = np.cumsum(IN_SPLITS)[:-1].tolist()
    for l in range(DEPTH):
        h = x @ w_in[l] + b_in[l]
        lru_x, lru_gate, q, kc_raw, vc_raw, ks_, vs_, kw_, vw_, gts = jnp.split(h, offsets, axis=-1)
        y_lru = _rg_lru(lru_x, lru_gate, conv_w[l], conv_b[l], lru_wa[l], lru_ba[l],
                        lru_wx[l], lru_bx[l], lru_lambda[l])
        kv = lambda t: t.reshape(B, S, N_KV, HEAD_DIM)
        kc = _compress(kv(kc_raw), cmp_pos_k[l], cmpk_w1[l], cmpk_b1[l], cmpk_w2[l], cmpk_b2[l])
        vc = _compress(kv(vc_raw), cmp_pos_v[l], cmpv_w1[l], cmpv_b1[l], cmpv_w2[l], cmpv_b2[l])
        y_nsa = _nsa(q.reshape(B, S, N_HEADS, HEAD_DIM), kc, vc, kv(ks_), kv(vs_), kv(kw_), kv(vw_),
                     gts.reshape(B, S, N_HEADS, N_BRANCHES))
        mix = jnp.concatenate([_rms_norm(y_lru, gn_lru_g[l]), _rms_norm(y_nsa, gn_nsa_g[l])], axis=-1) @ w_out[l]
        x = _layer_norm(ALPHA * x + mix, ln1_g[l], ln1_b[l])
        x = _layer_norm(ALPHA * x + _peer(x, peer_wq[l], peer_subkeys[l], peer_u[l], peer_v[l]), ln2_g[l], ln2_b[l])
    return x
```

```python
import functools

import numpy as np
import jax
import jax.numpy as jnp
from jax import lax
from jax.experimental import pallas as pl
from jax.experimental.pallas import tpu as pltpu

D_MODEL = 1024
LRU_WIDTH = 512
LRU_BLOCKS = 8
LRU_BLOCK_DIM = LRU_WIDTH // LRU_BLOCKS
CONV_WIDTH = 4
LRU_C = 8.0
N_HEADS = 8
HEAD_DIM = 64
N_KV = 2
Q_PER_KV = N_HEADS // N_KV
NSA_WIDTH = N_HEADS * HEAD_DIM
KV_WIDTH = N_KV * HEAD_DIM
N_BRANCHES = 3
CMP_BLOCK = 32
CMP_STRIDE = 16
SEL_BLOCK = 64
SEL_TOPN = 16
WINDOW = 512
Q_BLOCK = 128
FORCE_BONUS = 1e4
NEG = -1e30
REMOVED = -3.0e38
PEER_HEADS = 8
PEER_KEYS = 128
PEER_TOPK = 16
PEER_KEY_DIM = 128
N_EXPERTS = PEER_KEYS * PEER_KEYS
N_PAIRS = PEER_HEADS * PEER_TOPK
LN_EPS = 1e-5
IN_SPLITS = (LRU_WIDTH, LRU_WIDTH, NSA_WIDTH) + (KV_WIDTH,) * 6 + (N_HEADS * N_BRANCHES,)
IN_COLS = sum(IN_SPLITS)

LANES = 128
SUBLANES = 8
VMEM_LIMIT_BYTES = 56 * 1024 * 1024

IN_COLS_PAD = -(-IN_COLS // LANES) * LANES
COL_LRU_X = 0
COL_LRU_GATE = LRU_WIDTH
COL_Q = 2 * LRU_WIDTH
COL_KV = COL_Q + NSA_WIDTH
COL_GATES = COL_KV + 6 * KV_WIDTH

HI = lax.Precision.HIGHEST
F32 = jnp.float32
BF16 = jnp.bfloat16

HALF_EXPERTS = N_EXPERTS // 2
D_ROWS = D_MODEL // LANES


def _cparams(sem):
    return pltpu.CompilerParams(dimension_semantics=sem, vmem_limit_bytes=VMEM_LIMIT_BYTES)


def _dot_nt(a, b, precision=None):
    return lax.dot_general(a, b, (((1,), (1,)), ((), ())), precision=precision,
                           preferred_element_type=F32)


def _layer_norm_rows(z, g, b):
    mu = jnp.mean(z, axis=-1, keepdims=True)
    zc = z - mu
    var = jnp.mean(zc * zc, axis=-1, keepdims=True)
    return zc * lax.rsqrt(var + LN_EPS) * g + b


def _in_proj_kernel(x_ref, w_ref, b_ref, o_ref):
    o_ref[...] = jnp.dot(x_ref[...].astype(BF16), w_ref[...], preferred_element_type=F32) + b_ref[...]


def _in_proj(xf, w_bf, b_row, tm=512):
    T = xf.shape[0]
    return pl.pallas_call(
        _in_proj_kernel,
        grid=(T // tm,),
        in_specs=[pl.BlockSpec((tm, D_MODEL), lambda i: (i, 0)),
                  pl.BlockSpec((D_MODEL, IN_COLS_PAD), lambda i: (0, 0)),
                  pl.BlockSpec((1, IN_COLS_PAD), lambda i: (0, 0))],
        out_specs=pl.BlockSpec((tm, IN_COLS_PAD), lambda i: (i, 0)),
        out_shape=jax.ShapeDtypeStruct((T, IN_COLS_PAD), F32),
        compiler_params=_cparams(("arbitrary",)),
        name="in_proj",
    )(xf, w_bf, b_row)


def _rglru_kernel(x_ref, gate_ref, cw_ref, cb_ref, wa_ref, ba_ref, wx_ref, bx_ref, lam_ref, gn_ref,
                  o_ref, xprev_ref, h_ref, *, ts):
    i = pl.program_id(1)

    @pl.when(i == 0)
    def _():
        xprev_ref[...] = jnp.zeros_like(xprev_ref)
        h_ref[...] = jnp.zeros_like(h_ref)

    xb = x_ref[...]
    xp = xprev_ref[...]
    row = lax.broadcasted_iota(jnp.int32, xb.shape, 0)
    xc = cw_ref[CONV_WIDTH - 1:CONV_WIDTH, :] * xb + cb_ref[...]
    for j in range(1, CONV_WIDTH):
        shifted = jnp.where(row >= j, pltpu.roll(xb, j, axis=0), pltpu.roll(xp, j, axis=0))
        xc = xc + cw_ref[CONV_WIDTH - 1 - j:CONV_WIDTH - j, :] * shifted
    xprev_ref[...] = xb

    r = jax.nn.sigmoid(jnp.dot(xc, wa_ref[...], precision=HI, preferred_element_type=F32) + ba_ref[...])
    ig = jax.nn.sigmoid(jnp.dot(xc, wx_ref[...], precision=HI, preferred_element_type=F32) + bx_ref[...])
    lam = lam_ref[...]
    softplus_neg_lam = jnp.maximum(-lam, 0.0) + jnp.log1p(jnp.exp(-jnp.abs(lam)))
    log_a = -LRU_C * r * softplus_neg_lam
    a = jnp.exp(log_a)
    th = jnp.tanh(log_a)
    u = jnp.sqrt(-2.0 * th / (1.0 - th)) * (ig * xc)

    d = 1
    while d < ts:
        keep = row >= d
        a_sh = pltpu.roll(a, d, axis=0)
        u_sh = pltpu.roll(u, d, axis=0)
        u = jnp.where(keep, a * u_sh + u, u)
        a = jnp.where(keep, a * a_sh, a)
        d *= 2
    h = u + a * h_ref[0:1, :]
    h_ref[0:1, :] = h[ts - 1:ts, :]

    y = h * jax.nn.gelu(gate_ref[...])
    y = y * lax.rsqrt(jnp.mean(y * y, axis=-1, keepdims=True) + LN_EPS) * gn_ref[...]
    o_ref[...] = y


def _rglru(h_all, B, S, cw, cb, wa_bd, ba, wx_bd, bx, lam, gn, ts=512):
    ts = min(ts, S)
    nt = S // ts
    C = LRU_WIDTH
    row = lambda v: v.reshape(1, C)
    const = lambda shape: pl.BlockSpec(shape, lambda b, i: (0,) * len(shape))
    return pl.pallas_call(
        functools.partial(_rglru_kernel, ts=ts),
        grid=(B, nt),
        in_specs=[pl.BlockSpec((ts, C), lambda b, i: (b * nt + i, COL_LRU_X // C)),
                  pl.BlockSpec((ts, C), lambda b, i: (b * nt + i, COL_LRU_GATE // C)),
                  const((CONV_WIDTH, C)), const((1, C)), const((C, C)), const((1, C)),
                  const((C, C)), const((1, C)), const((1, C)), const((1, C))],
        out_specs=pl.BlockSpec((ts, C), lambda b, i: (b * nt + i, 0)),
        out_shape=jax.ShapeDtypeStruct((B * S, C), F32),
        scratch_shapes=[pltpu.VMEM((ts, C), F32), pltpu.VMEM((SUBLANES, C), F32)],
        compiler_params=_cparams(("arbitrary", "arbitrary")),
        name="rglru",
    )(h_all, h_all, cw, row(cb), wa_bd, row(ba), wx_bd, row(bx), row(lam), row(gn))


def _block_diag(w):
    n, d, e = w.shape
    eye = jnp.eye(n, dtype=w.dtype)
    return (w[:, :, None, :] * eye[:, None, :, None]).reshape(n * d, n * e)


def _compress_kernel(kr_ref, w1ab_ref, w1_ref, pos_ref, b1_ref, w2_ref, b2_ref, o_ref):
    kr = kr_ref[0, 0]
    nr = kr.shape[0]
    ab = jnp.dot(kr, w1ab_ref[0], precision=HI, preferred_element_type=F32)
    ab_next = pltpu.roll(ab, nr - 1, axis=0)
    pre = ab[:, :HEAD_DIM] + ab_next[:, HEAD_DIM:]
    posc = jnp.dot(pos_ref[0], w1_ref[0], precision=HI, preferred_element_type=F32)[0:1, :]
    h1 = jax.nn.gelu(pre + posc + b1_ref[0])
    o_ref[0, 0] = jnp.dot(h1, w2_ref[0], precision=HI, preferred_element_type=F32) + b2_ref[0]


def _compress(kr, w1ab, w1, posflat, b1, w2, b2):
    _, BG, nr, fan = kr.shape
    wspec = lambda shape: pl.BlockSpec((1,) + shape, lambda s, i: (s,) + (0,) * len(shape))
    return pl.pallas_call(
        _compress_kernel,
        grid=(2, BG),
        in_specs=[pl.BlockSpec((1, 1, nr, fan), lambda s, i: (s, i, 0, 0)),
                  wspec((fan, 2 * HEAD_DIM)), wspec((2 * fan, HEAD_DIM)), wspec((SUBLANES, 2 * fan)),
                  wspec((1, HEAD_DIM)), wspec((HEAD_DIM, HEAD_DIM)), wspec((1, HEAD_DIM))],
        out_specs=pl.BlockSpec((1, 1, nr, HEAD_DIM), lambda s, i: (s, i, 0, 0)),
        out_shape=jax.ShapeDtypeStruct((2, BG, nr, HEAD_DIM), F32),
        compiler_params=_cparams(("arbitrary", "arbitrary")),
        name="compress",
    )(kr, w1ab, w1, posflat, b1, w2, b2)


def _selection_matrix(n_rows, n_cmp, n_sel):
    cs = np.arange(n_rows)[:, None] * CMP_STRIDE
    ss = np.arange(n_sel)[None, :] * SEL_BLOCK
    ov = np.clip(np.minimum(cs + CMP_BLOCK, ss + SEL_BLOCK) - np.maximum(cs, ss), 0, None)
    ov = (ov / CMP_STRIDE).astype(np.float32)
    ov[n_cmp:] = 0.0
    return ov


def _nsa_kernel(q_ref, gt_ref, kc_ref, vc_ref, ks_ref, vs_ref, kw_ref, vw_ref, slope_ref, selmat_ref,
                emat_ref, gn_ref, o_ref, *, n_cmp, n_top):
    qt = pl.program_id(1)
    q0 = qt * Q_BLOCK
    R = Q_PER_KV
    rows = R * Q_BLOCK
    n_rows_c = kc_ref.shape[2]
    n_sel = selmat_ref.shape[1]
    t_row = q0 + (lax.broadcasted_iota(jnp.int32, (rows, 1), 0) % Q_BLOCK)
    t_q = q0 + lax.broadcasted_iota(jnp.int32, (Q_BLOCK, 1), 0)
    gates = jax.nn.sigmoid(gt_ref[...])
    key_lane = lax.broadcasted_iota(jnp.int32, (1, Q_BLOCK), 1)

    def online_softmax_step(carry, s, ok, v_bf):
        m, l, acc = carry
        s_m = jnp.where(ok, s, NEG)
        m_new = jnp.maximum(m, jnp.max(s_m, axis=-1, keepdims=True))
        alpha = jnp.exp(m - m_new)
        p = jnp.where(ok, jnp.exp(s_m - m_new), 0.0)
        l = alpha * l + jnp.sum(p, axis=-1, keepdims=True)
        acc = alpha * acc + jnp.dot(p.astype(BF16), v_bf, preferred_element_type=F32)
        return m_new, l, acc

    init = (jnp.full((rows, 1), NEG, F32), jnp.zeros((rows, 1), F32), jnp.zeros((rows, HEAD_DIM), F32))
    group_out = []
    for g in range(N_KV):
        qg = q_ref[:, g * R * HEAD_DIM:(g + 1) * R * HEAD_DIM] * (HEAD_DIM ** -0.5)
        qs = jnp.concatenate([qg[:, r * HEAD_DIM:(r + 1) * HEAD_DIM] for r in range(R)], axis=0)
        qs_bf = qs.astype(BF16)
        slope = slope_ref[g]

        kc = kc_ref[0, g]
        vc = vc_ref[0, g]
        c_idx = lax.broadcasted_iota(jnp.int32, (1, n_rows_c), 1)
        cmp_end = c_idx * CMP_STRIDE + (CMP_BLOCK - 1)
        dist_c = t_row - cmp_end
        valid_c = (cmp_end <= t_row) & (c_idx < n_cmp)
        s = _dot_nt(qs, kc, HI) - slope * jnp.abs(dist_c).astype(F32)
        s = jnp.where(valid_c, s, NEG)
        e = jnp.exp(s - jnp.max(s, axis=-1, keepdims=True))
        p_c = e / jnp.sum(e, axis=-1, keepdims=True)
        p_c = p_c * (t_row >= CMP_BLOCK - 1).astype(F32)
        o_c = jnp.dot(p_c, vc, precision=HI, preferred_element_type=F32)

        p_sum = p_c[0:Q_BLOCK]
        for r in range(1, R):
            p_sum = p_sum + p_c[r * Q_BLOCK:(r + 1) * Q_BLOCK]
        imp = jnp.dot(p_sum, selmat_ref[...], precision=HI, preferred_element_type=F32)
        j_idx = lax.broadcasted_iota(jnp.int32, (1, n_sel), 1)
        cur = t_q // SEL_BLOCK
        forced = (j_idx == 0) | (j_idx == cur) | (j_idx == cur - 1)
        work = jnp.where(j_idx <= cur, imp + jnp.where(forced, FORCE_BONUS, 0.0), NEG)
        msel = jnp.zeros((Q_BLOCK, n_sel), F32)
        for _ in range(n_top):
            mx = jnp.max(work, axis=-1, keepdims=True)
            first = jnp.min(jnp.where(work == mx, j_idx, n_sel), axis=-1, keepdims=True)
            pick = j_idx == first
            msel = jnp.where(pick, 1.0, msel)
            work = jnp.where(pick, REMOVED, work)
        msel_bf = msel.astype(BF16)

        def sel_body(kt, carry):
            k0 = pl.multiple_of(kt * Q_BLOCK, Q_BLOCK)
            k_bf = ks_ref[0, g, pl.ds(k0, Q_BLOCK), :].astype(BF16)
            v_bf = vs_ref[0, g, pl.ds(k0, Q_BLOCK), :].astype(BF16)
            dist = t_row - (k0 + key_lane)
            s = _dot_nt(qs_bf, k_bf) - slope * dist.astype(F32)
            chosen = jnp.dot(msel_bf, emat_ref[kt], preferred_element_type=F32)
            chosen = jnp.concatenate([chosen] * R, axis=0)
            ok = (dist >= 0) & (chosen > 0.5)
            return online_softmax_step(carry, s, ok, v_bf)

        _, l_s, acc_s = lax.fori_loop(0, qt + 1, sel_body, init)
        o_s = acc_s / l_s

        def win_body(kt, carry):
            k0 = pl.multiple_of(kt * Q_BLOCK, Q_BLOCK)
            k_bf = kw_ref[0, g, pl.ds(k0, Q_BLOCK), :].astype(BF16)
            v_bf = vw_ref[0, g, pl.ds(k0, Q_BLOCK), :].astype(BF16)
            dist = t_row - (k0 + key_lane)
            s = _dot_nt(qs_bf, k_bf) - slope * dist.astype(F32)
            ok = (dist >= 0) & (dist < WINDOW)
            return online_softmax_step(carry, s, ok, v_bf)

        _, l_w, acc_w = lax.fori_loop(jnp.maximum(qt - WINDOW // Q_BLOCK, 0), qt + 1, win_body, init)
        o_w = acc_w / l_w

        heads = []
        for r in range(R):
            col = (g * R + r) * N_BRANCHES
            sl = slice(r * Q_BLOCK, (r + 1) * Q_BLOCK)
            heads.append(gates[:, col:col + 1] * o_c[sl] + gates[:, col + 1:col + 2] * o_s[sl]
                         + gates[:, col + 2:col + 3] * o_w[sl])
        group_out.append(jnp.concatenate(heads, axis=1))
    y = jnp.concatenate(group_out, axis=1)
    o_ref[...] = y * lax.rsqrt(jnp.mean(y * y, axis=-1, keepdims=True) + LN_EPS) * gn_ref[...]


def _nsa(h_all, B, S, kc, vc, ks, vs, kw, vw, gn):
    nq = S // Q_BLOCK
    n_cmp = (S - CMP_BLOCK) // CMP_STRIDE + 1
    n_rows_c = kc.shape[2]
    n_sel = S // SEL_BLOCK
    n_top = min(SEL_TOPN, n_sel)
    slopes = np.array([2.0 ** (-8.0 * (h + 1) / N_HEADS) for h in range(N_HEADS)], np.float32)
    slope_rows = jnp.asarray(np.repeat(slopes.reshape(N_KV, Q_PER_KV), Q_BLOCK, axis=1)[:, :, None])
    selmat = jnp.asarray(_selection_matrix(n_rows_c, n_cmp, n_sel))
    key_blk = (np.arange(S) // SEL_BLOCK).reshape(nq, 1, Q_BLOCK)
    emat = jnp.asarray((key_blk == np.arange(n_sel)[None, :, None]).astype(np.float32), dtype=BF16)
    qw = Q_PER_KV * HEAD_DIM * N_KV
    kvspec = lambda n: pl.BlockSpec((1, N_KV, n, HEAD_DIM), lambda b, i: (b, 0, 0, 0))
    const = lambda shape: pl.BlockSpec(shape, lambda b, i: (0,) * len(shape))
    return pl.pallas_call(
        functools.partial(_nsa_kernel, n_cmp=n_cmp, n_top=n_top),
        grid=(B, nq),
        in_specs=[pl.BlockSpec((Q_BLOCK, qw), lambda b, i: (b * nq + i, COL_Q // qw)),
                  pl.BlockSpec((Q_BLOCK, LANES), lambda b, i: (b * nq + i, COL_GATES // LANES)),
                  kvspec(n_rows_c), kvspec(n_rows_c), kvspec(S), kvspec(S), kvspec(S), kvspec(S),
                  const((N_KV, Q_PER_KV * Q_BLOCK, 1)), const((n_rows_c, n_sel)),
                  const((nq, n_sel, Q_BLOCK)), const((1, NSA_WIDTH))],
        out_specs=pl.BlockSpec((Q_BLOCK, NSA_WIDTH), lambda b, i: (b * nq + i, 0)),
        out_shape=jax.ShapeDtypeStruct((B * S, NSA_WIDTH), F32),
        compiler_params=_cparams(("arbitrary", "arbitrary")),
        name="nsa",
    )(h_all, h_all, kc, vc, ks, vs, kw, vw, slope_rows, selmat, emat, gn.reshape(1, NSA_WIDTH))


def _out_ln_kernel(ylru_ref, ynsa_ref, x_ref, w_ref, g_ref, b_ref, o_ref, *, alpha):
    mix = jnp.dot(ylru_ref[...].astype(BF16), w_ref[0:LRU_WIDTH, :], preferred_element_type=F32)
    mix = mix + jnp.dot(ynsa_ref[...].astype(BF16), w_ref[LRU_WIDTH:, :], preferred_element_type=F32)
    o_ref[...] = _layer_norm_rows(alpha * x_ref[...] + mix, g_ref[...], b_ref[...])


def _out_ln(ylru, ynsa, xf, w_bf, g, b, alpha, tm=512):
    T = xf.shape[0]
    const = lambda shape: pl.BlockSpec(shape, lambda i: (0,) * len(shape))
    return pl.pallas_call(
        functools.partial(_out_ln_kernel, alpha=alpha),
        grid=(T // tm,),
        in_specs=[pl.BlockSpec((tm, LRU_WIDTH), lambda i: (i, 0)),
                  pl.BlockSpec((tm, NSA_WIDTH), lambda i: (i, 0)),
                  pl.BlockSpec((tm, D_MODEL), lambda i: (i, 0)),
                  const((D_MODEL, D_MODEL)), const((1, D_MODEL)), const((1, D_MODEL))],
        out_specs=pl.BlockSpec((tm, D_MODEL), lambda i: (i, 0)),
        out_shape=jax.ShapeDtypeStruct((T, D_MODEL), F32),
        compiler_params=_cparams(("arbitrary",)),
        name="out_ln",
    )(ylru, ynsa, xf, w_bf, g.reshape(1, D_MODEL), b.reshape(1, D_MODEL))


def _topk_rows(vals, n_keep, payload=None):
    n_rows, n = vals.shape
    ridx = lax.broadcasted_iota(jnp.int32, (n_rows, 1), 0)
    kidx = lax.broadcasted_iota(jnp.int32, (n_keep, 1), 0)
    kept_v = jnp.zeros((n_keep, n), F32)
    kept_i = jnp.zeros((n_keep, n), jnp.int32)
    work = vals
    for r in range(n_keep):
        mx = jnp.max(work, axis=0, keepdims=True)
        first = jnp.min(jnp.where(work == mx, ridx, n_rows), axis=0, keepdims=True)
        pick = ridx == first
        kept_v = jnp.where(kidx == r, mx, kept_v)
        if payload is None:
            kept_i = jnp.where(kidx == r, first, kept_i)
        else:
            kept_i = jnp.where(kidx == r, jnp.max(jnp.where(pick, payload, -1), axis=0, keepdims=True), kept_i)
        work = jnp.where(pick, REMOVED, work)
    return kept_v, kept_i


def _peer_route_kernel(x_ref, wq_ref, sk_ref, e_ref, g_ref):
    q = jnp.dot(x_ref[...].astype(BF16), wq_ref[...], preferred_element_type=F32)
    experts, gates = [], []
    for h in range(PEER_HEADS):
        sv, si = [], []
        for c in range(2):
            col = (h * 2 + c) * PEER_KEY_DIM
            s_t = _dot_nt(sk_ref[c], q[:, col:col + PEER_KEY_DIM], HI)
            v, i = _topk_rows(s_t, PEER_TOPK)
            sv.append(v)
            si.append(i)
        cand = jnp.concatenate([sv[0][i:i + 1, :] + sv[1] for i in range(PEER_TOPK)], axis=0)
        ecand = jnp.concatenate([si[0][i:i + 1, :] * PEER_KEYS + si[1] for i in range(PEER_TOPK)], axis=0)
        cv, ce = _topk_rows(cand, PEER_TOPK, payload=ecand)
        ex = jnp.exp(cv - cv[0:1, :])
        gates.append(ex / jnp.sum(ex, axis=0, keepdims=True))
        experts.append(ce)
    e_t = jnp.concatenate(experts, axis=0).astype(F32)
    g_t = jnp.concatenate(gates, axis=0)
    e_ref[...] = e_t.T.astype(jnp.int32)
    g_ref[...] = g_t.T


def _peer_route(x1, wq_bf, subkeys, tm=256):
    T = x1.shape[0]
    tm = min(tm, T)
    nq = wq_bf.shape[1]
    return pl.pallas_call(
        _peer_route_kernel,
        grid=(T // tm,),
        in_specs=[pl.BlockSpec((tm, D_MODEL), lambda i: (i, 0)),
                  pl.BlockSpec((D_MODEL, nq), lambda i: (0, 0)),
                  pl.BlockSpec((2, PEER_KEYS, PEER_KEY_DIM), lambda i: (0, 0, 0))],
        out_specs=[pl.BlockSpec((tm, N_PAIRS), lambda i: (i, 0)),
                   pl.BlockSpec((tm, N_PAIRS), lambda i: (i, 0))],
        out_shape=[jax.ShapeDtypeStruct((T, N_PAIRS), jnp.int32),
                   jax.ShapeDtypeStruct((T, N_PAIRS), F32)],
        compiler_params=_cparams(("arbitrary",)),
        name="peer_route",
    )(x1, wq_bf, subkeys)


def _pack_table(tab):
    t = tab.astype(BF16).reshape(2, HALF_EXPERTS, D_ROWS, LANES)
    return t.transpose(1, 0, 2, 3).reshape(HALF_EXPERTS, 2 * D_ROWS, LANES)


def _expert_row(tab_ref, e):
    tile = tab_ref[e & (HALF_EXPERTS - 1)].astype(F32)
    upper = jnp.full((D_ROWS, LANES), e >> 13, jnp.int32) == 1
    return jnp.where(upper, tile[D_ROWS:], tile[:D_ROWS])


def _sublane_sums(ps, roll, where, sub):
    lvl = ps
    for r in (1, 2, 4):
        m = (sub % (2 * r)) < r
        half = len(lvl) // 2
        lvl = [where(m, lvl[i], lvl[i + half]) + roll(where(m, lvl[i + half], lvl[i]), r) for i in range(half)]
    return lvl[0]


def _sum_src():
    sub = np.arange(SUBLANES)[:, None] * np.ones((1, LANES), np.int64)
    ps = [np.full((SUBLANES, LANES), 10.0 ** i) for i in range(SUBLANES)]
    out = _sublane_sums(ps, lambda v, r: np.roll(v, r, axis=0), np.where, sub)
    return [int(round(np.log10(out[s, 0] / SUBLANES))) for s in range(SUBLANES)]


_SUM_SRC = _sum_src()
_SUM_DST = [_SUM_SRC.index(i) for i in range(SUBLANES)]


def _peer_u_kernel(exp_ref, x_ref, tab_ref, gate_ref, o_ref, part_ref, zt_ref, *, tb):
    sub = lax.broadcasted_iota(jnp.int32, (SUBLANES, LANES), 0)
    lane = lax.broadcasted_iota(jnp.int32, (N_PAIRS, tb), 1)
    ones = jnp.ones((LANES, tb), F32)

    def token_body(t, carry):
        xt = x_ref[t]
        for grp in range(N_PAIRS // SUBLANES):
            prods = [None] * SUBLANES
            for s in range(SUBLANES):
                prods[_SUM_SRC[s]] = _expert_row(tab_ref, exp_ref[t, grp * SUBLANES + s]) * xt
            part_ref[grp * SUBLANES:(grp + 1) * SUBLANES, :] = _sublane_sums(
                prods, lambda v, r: pltpu.roll(v, r, axis=0), jnp.where, sub)
        z = jnp.dot(part_ref[...], ones, precision=HI, preferred_element_type=F32)
        zt_ref[...] = jnp.where(lane == t, z, zt_ref[...])
        return carry

    lax.fori_loop(0, tb, token_body, 0)
    o_ref[...] = jax.nn.gelu(zt_ref[...].T) * gate_ref[...]


def _peer_u(experts, x3, tab, gates, tb=128):
    T = x3.shape[0]
    return pl.pallas_call(
        functools.partial(_peer_u_kernel, tb=tb),
        grid=(T // tb,),
        in_specs=[pl.BlockSpec((tb, N_PAIRS), lambda i: (i, 0), memory_space=pltpu.SMEM),
                  pl.BlockSpec((tb, D_ROWS, LANES), lambda i: (i, 0, 0)),
                  pl.BlockSpec((HALF_EXPERTS, 2 * D_ROWS, LANES), lambda i: (0, 0, 0),
                               pipeline_mode=pl.Buffered(1)),
                  pl.BlockSpec((tb, N_PAIRS), lambda i: (i, 0))],
        out_specs=pl.BlockSpec((tb, N_PAIRS), lambda i: (i, 0)),
        out_shape=jax.ShapeDtypeStruct((T, N_PAIRS), F32),
        scratch_shapes=[pltpu.VMEM((N_PAIRS, LANES), F32), pltpu.VMEM((N_PAIRS, tb), F32)],
        compiler_params=_cparams(("arbitrary",)),
        name="peer_u",
    )(experts, x3, tab, gates)


def _peer_v_kernel(exp_ref, w_ref, x_ref, tab_ref, g_ref, b_ref, o_ref, y_ref, *, tb, alpha):
    n_acc = 4

    def token_body(t, carry):
        acc = [jnp.zeros((D_ROWS, LANES), F32) for _ in range(n_acc)]
        for k in range(N_PAIRS):
            acc[k % n_acc] = acc[k % n_acc] + _expert_row(tab_ref, exp_ref[t, k]) * w_ref[t, k]
        y_ref[t] = (acc[0] + acc[1]) + (acc[2] + acc[3])
        return carry

    lax.fori_loop(0, tb, token_body, 0)
    z = alpha * x_ref[...] + y_ref[...]
    inv_d = 1.0 / D_MODEL
    mu = jnp.sum(jnp.sum(z, axis=2, keepdims=True), axis=1, keepdims=True) * inv_d
    zc = z - mu
    var = jnp.sum(jnp.sum(zc * zc, axis=2, keepdims=True), axis=1, keepdims=True) * inv_d
    o_ref[...] = zc * lax.rsqrt(var + LN_EPS) * g_ref[...] + b_ref[...]


def _peer_v(experts, wgt, x3, tab, g, b, alpha, tb=128):
    T = x3.shape[0]
    return pl.pallas_call(
        functools.partial(_peer_v_kernel, tb=tb, alpha=alpha),
        grid=(T // tb,),
        in_specs=[pl.BlockSpec((tb, N_PAIRS), lambda i: (i, 0), memory_space=pltpu.SMEM),
                  pl.BlockSpec((tb, N_PAIRS), lambda i: (i, 0), memory_space=pltpu.SMEM),
                  pl.BlockSpec((tb, D_ROWS, LANES), lambda i: (i, 0, 0)),
                  pl.BlockSpec((HALF_EXPERTS, 2 * D_ROWS, LANES), lambda i: (0, 0, 0),
                               pipeline_mode=pl.Buffered(1)),
                  pl.BlockSpec((1, D_ROWS, LANES), lambda i: (0, 0, 0)),
                  pl.BlockSpec((1, D_ROWS, LANES), lambda i: (0, 0, 0))],
        out_specs=pl.BlockSpec((tb, D_ROWS, LANES), lambda i: (i, 0, 0)),
        out_shape=jax.ShapeDtypeStruct((T, D_ROWS, LANES), F32),
        scratch_shapes=[pltpu.VMEM((tb, D_ROWS, LANES), F32)],
        compiler_params=_cparams(("arbitrary",)),
        name="peer_v",
    )(experts, wgt, x3, tab, g.reshape(1, D_ROWS, LANES), b.reshape(1, D_ROWS, LANES))


def kernel(x, w_in, b_in, conv_w, conv_b, lru_wa, lru_ba, lru_wx, lru_bx, lru_lambda, cmp_pos_k, cmpk_w1, cmpk_b1, cmpk_w2, cmpk_b2, cmp_pos_v, cmpv_w1, cmpv_b1, cmpv_w2, cmpv_b2, gn_lru_g, gn_nsa_g, w_out, ln1_g, ln1_b, peer_wq, peer_subkeys, peer_u, peer_v, ln2_g, ln2_b):
    B, S, D = x.shape
    T = B * S
    depth = w_in.shape[0]
    alpha = (2 * depth) ** 0.25
    fan = CMP_STRIDE * HEAD_DIM
    xf = x.reshape(T, D)
    for l in range(depth):
        w_in_bf = jnp.pad(w_in[l], ((0, 0), (0, IN_COLS_PAD - IN_COLS))).astype(BF16)
        b_in_row = jnp.pad(b_in[l], (0, IN_COLS_PAD - IN_COLS)).reshape(1, IN_COLS_PAD)
        h_all = _in_proj(xf, w_in_bf, b_in_row)

        y_lru = _rglru(h_all, B, S, conv_w[l], conv_b[l], _block_diag(lru_wa[l]), lru_ba[l],
                       _block_diag(lru_wx[l]), lru_bx[l], lru_lambda[l], gn_lru_g[l])

        kv = h_all[:, COL_KV:COL_KV + 6 * KV_WIDTH].reshape(B, S, 6, N_KV, HEAD_DIM).transpose(2, 0, 3, 1, 4)
        kr = kv[0:2].reshape(2, B * N_KV, S // CMP_STRIDE, fan)
        w1 = jnp.stack([cmpk_w1[l], cmpv_w1[l]])
        w1ab = jnp.concatenate([w1[:, :fan], w1[:, fan:]], axis=2)
        pos = jnp.stack([cmp_pos_k[l], cmp_pos_v[l]]).reshape(2, 1, CMP_BLOCK * HEAD_DIM)
        posflat = jnp.broadcast_to(pos, (2, SUBLANES, CMP_BLOCK * HEAD_DIM))
        kcvc = _compress(kr, w1ab, w1, posflat,
                         jnp.stack([cmpk_b1[l], cmpv_b1[l]]).reshape(2, 1, HEAD_DIM),
                         jnp.stack([cmpk_w2[l], cmpv_w2[l]]),
                         jnp.stack([cmpk_b2[l], cmpv_b2[l]]).reshape(2, 1, HEAD_DIM))
        kcvc = kcvc.reshape(2, B, N_KV, S // CMP_STRIDE, HEAD_DIM)
        y_nsa = _nsa(h_all, B, S, kcvc[0], kcvc[1], kv[2], kv[3], kv[4], kv[5], gn_nsa_g[l])

        x1 = _out_ln(y_lru, y_nsa, xf, w_out[l].astype(BF16), ln1_g[l], ln1_b[l], alpha)

        experts, gates = _peer_route(x1, peer_wq[l].astype(BF16), peer_subkeys[l])
        x3 = x1.reshape(T, D_ROWS, LANES)
        wgt = _peer_u(experts, x3, _pack_table(peer_u[l]), gates)
        xf = _peer_v(experts, wgt, x3, _pack_table(peer_v[l]), ln2_g[l], ln2_b[l], alpha).reshape(T, D)
    return xf.reshape(B, S, D)
```

```python
import functools

import numpy as np
import jax
import jax.numpy as jnp
from jax import lax
from jax.experimental import pallas as pl
from jax.experimental.pallas import tpu as pltpu

D_MODEL = 1024
LRU_WIDTH = 512
LRU_BLOCKS = 8
LRU_BLOCK_DIM = LRU_WIDTH // LRU_BLOCKS
CONV_WIDTH = 4
LRU_C = 8.0
N_HEADS = 8
HEAD_DIM = 64
N_KV = 2
Q_PER_KV = N_HEADS // N_KV
NSA_WIDTH = N_HEADS * HEAD_DIM
KV_WIDTH = N_KV * HEAD_DIM
N_BRANCHES = 3
CMP_BLOCK = 32
CMP_STRIDE = 16
SEL_BLOCK = 64
SEL_TOPN = 16
WINDOW = 512
Q_BLOCK = 128
FORCE_BONUS = 1e4
NEG = -1e30
REMOVED = -3.0e38
PEER_HEADS = 8
PEER_KEYS = 128
PEER_TOPK = 16
PEER_KEY_DIM = 128
N_EXPERTS = PEER_KEYS * PEER_KEYS
N_PAIRS = PEER_HEADS * PEER_TOPK
LN_EPS = 1e-5
IN_SPLITS = (LRU_WIDTH, LRU_WIDTH, NSA_WIDTH) + (KV_WIDTH,) * 6 + (N_HEADS * N_BRANCHES,)
IN_COLS = sum(IN_SPLITS)

LANES = 128
SUBLANES = 8
VMEM_LIMIT_BYTES = 56 * 1024 * 1024

IN_COLS_PAD = -(-IN_COLS // LANES) * LANES
COL_LRU_X = 0
COL_LRU_GATE = LRU_WIDTH
COL_Q = 2 * LRU_WIDTH
COL_KV = COL_Q + NSA_WIDTH
COL_GATES = COL_KV + 6 * KV_WIDTH

HI = lax.Precision.HIGHEST
F32 = jnp.float32
BF16 = jnp.bfloat16

HALF_EXPERTS = N_EXPERTS // 2
D_ROWS = D_MODEL // LANES


def _cparams(sem):
    return pltpu.CompilerParams(dimension_semantics=sem, vmem_limit_bytes=VMEM_LIMIT_BYTES)


def _dot_nt(a, b, precision=None):
    return lax.dot_general(a, b, (((1,), (1,)), ((), ())), precision=precision,
                           preferred_element_type=F32)


def _split2(a):
    hi = a.astype(BF16)
    return hi, (a - hi.astype(F32)).astype(BF16)


def _layer_norm_rows(z, g, b):
    mu = jnp.mean(z, axis=-1, keepdims=True)
    zc = z - mu
    var = jnp.mean(zc * zc, axis=-1, keepdims=True)
    return zc * lax.rsqrt(var + LN_EPS) * g + b


def _in_proj_kernel(x_ref, w_ref, b_ref, o_ref):
    o_ref[...] = jnp.dot(x_ref[...].astype(BF16), w_ref[...], preferred_element_type=F32) + b_ref[...]


def _in_proj(xf, w_bf, b_row, tm=512):
    T = xf.shape[0]
    return pl.pallas_call(
        _in_proj_kernel,
        grid=(T // tm,),
        in_specs=[pl.BlockSpec((tm, D_MODEL), lambda i: (i, 0)),
                  pl.BlockSpec((D_MODEL, IN_COLS_PAD), lambda i: (0, 0)),
                  pl.BlockSpec((1, IN_COLS_PAD), lambda i: (0, 0))],
        out_specs=pl.BlockSpec((tm, IN_COLS_PAD), lambda i: (i, 0)),
        out_shape=jax.ShapeDtypeStruct((T, IN_COLS_PAD), F32),
        compiler_params=_cparams(("arbitrary",)),
        name="in_proj",
    )(xf, w_bf, b_row)


def _rglru_kernel(x_ref, gate_ref, cw_ref, cb_ref, wa_ref, ba_ref, wx_ref, bx_ref, lam_ref, gn_ref,
                  o_ref, xprev_ref, h_ref, *, ts):
    i = pl.program_id(1)

    @pl.when(i == 0)
    def _():
        xprev_ref[...] = jnp.zeros_like(xprev_ref)
        h_ref[...] = jnp.zeros_like(h_ref)

    xb = x_ref[...]
    xp = xprev_ref[...]
    row = lax.broadcasted_iota(jnp.int32, xb.shape, 0)
    xc = cw_ref[CONV_WIDTH - 1:CONV_WIDTH, :] * xb + cb_ref[...]
    for j in range(1, CONV_WIDTH):
        shifted = jnp.where(row >= j, pltpu.roll(xb, j, axis=0), pltpu.roll(xp, j, axis=0))
        xc = xc + cw_ref[CONV_WIDTH - 1 - j:CONV_WIDTH - j, :] * shifted
    xprev_ref[...] = xb

    r = jax.nn.sigmoid(jnp.dot(xc, wa_ref[...], precision=HI, preferred_element_type=F32) + ba_ref[...])
    ig = jax.nn.sigmoid(jnp.dot(xc, wx_ref[...], precision=HI, preferred_element_type=F32) + bx_ref[...])
    lam = lam_ref[...]
    softplus_neg_lam = jnp.maximum(-lam, 0.0) + jnp.log1p(jnp.exp(-jnp.abs(lam)))
    log_a = -LRU_C * r * softplus_neg_lam
    a = jnp.exp(log_a)
    th = jnp.tanh(log_a)
    u = jnp.sqrt(-2.0 * th / (1.0 - th)) * (ig * xc)

    d = 1
    while d < ts:
        keep = row >= d
        a_sh = pltpu.roll(a, d, axis=0)
        u_sh = pltpu.roll(u, d, axis=0)
        u = jnp.where(keep, a * u_sh + u, u)
        a = jnp.where(keep, a * a_sh, a)
        d *= 2
    h = u + a * h_ref[0:1, :]
    h_ref[0:1, :] = h[ts - 1:ts, :]

    y = h * jax.nn.gelu(gate_ref[...])
    y = y * lax.rsqrt(jnp.mean(y * y, axis=-1, keepdims=True) + LN_EPS) * gn_ref[...]
    o_ref[...] = y


def _rglru(h_all, B, S, cw, cb, wa_bd, ba, wx_bd, bx, lam, gn, ts=512):
    ts = min(ts, S)
    nt = S // ts
    C = LRU_WIDTH
    row = lambda v: v.reshape(1, C)
    const = lambda shape: pl.BlockSpec(shape, lambda b, i: (0,) * len(shape))
    return pl.pallas_call(
        functools.partial(_rglru_kernel, ts=ts),
        grid=(B, nt),
        in_specs=[pl.BlockSpec((ts, C), lambda b, i: (b * nt + i, COL_LRU_X // C)),
                  pl.BlockSpec((ts, C), lambda b, i: (b * nt + i, COL_LRU_GATE // C)),
                  const((CONV_WIDTH, C)), const((1, C)), const((C, C)), const((1, C)),
                  const((C, C)), const((1, C)), const((1, C)), const((1, C))],
        out_specs=pl.BlockSpec((ts, C), lambda b, i: (b * nt + i, 0)),
        out_shape=jax.ShapeDtypeStruct((B * S, C), F32),
        scratch_shapes=[pltpu.VMEM((ts, C), F32), pltpu.VMEM((SUBLANES, C), F32)],
        compiler_params=_cparams(("arbitrary", "arbitrary")),
        name="rglru",
    )(h_all, h_all, cw, row(cb), wa_bd, row(ba), wx_bd, row(bx), row(lam), row(gn))


def _block_diag(w):
    n, d, e = w.shape
    eye = jnp.eye(n, dtype=w.dtype)
    return (w[:, :, None, :] * eye[:, None, :, None]).reshape(n * d, n * e)


def _compress_kernel(kr_ref, w1ab_ref, w1_ref, pos_ref, b1_ref, w2_ref, b2_ref, o_ref):
    kr = kr_ref[0, 0]
    nr = kr.shape[0]
    ab = jnp.dot(kr, w1ab_ref[0], precision=HI, preferred_element_type=F32)
    ab_next = pltpu.roll(ab, nr - 1, axis=0)
    pre = ab[:, :HEAD_DIM] + ab_next[:, HEAD_DIM:]
    posc = jnp.dot(pos_ref[0], w1_ref[0], precision=HI, preferred_element_type=F32)[0:1, :]
    h1 = jax.nn.gelu(pre + posc + b1_ref[0])
    o_ref[0, 0] = jnp.dot(h1, w2_ref[0], precision=HI, preferred_element_type=F32) + b2_ref[0]


def _compress(kr, w1ab, w1, posflat, b1, w2, b2):
    _, BG, nr, fan = kr.shape
    wspec = lambda shape: pl.BlockSpec((1,) + shape, lambda s, i: (s,) + (0,) * len(shape))
    return pl.pallas_call(
        _compress_kernel,
        grid=(2, BG),
        in_specs=[pl.BlockSpec((1, 1, nr, fan), lambda s, i: (s, i, 0, 0)),
                  wspec((fan, 2 * HEAD_DIM)), wspec((2 * fan, HEAD_DIM)), wspec((SUBLANES, 2 * fan)),
                  wspec((1, HEAD_DIM)), wspec((HEAD_DIM, HEAD_DIM)), wspec((1, HEAD_DIM))],
        out_specs=pl.BlockSpec((1, 1, nr, HEAD_DIM), lambda s, i: (s, i, 0, 0)),
        out_shape=jax.ShapeDtypeStruct((2, BG, nr, HEAD_DIM), F32),
        compiler_params=_cparams(("arbitrary", "arbitrary")),
        name="compress",
    )(kr, w1ab, w1, posflat, b1, w2, b2)


def _selection_matrix(n_rows, n_cmp, n_sel):
    cs = np.arange(n_rows)[:, None] * CMP_STRIDE
    ss = np.arange(n_sel)[None, :] * SEL_BLOCK
    ov = np.clip(np.minimum(cs + CMP_BLOCK, ss + SEL_BLOCK) - np.maximum(cs, ss), 0, None)
    ov = (ov / CMP_STRIDE).astype(np.float32)
    ov[n_cmp:] = 0.0
    return ov


V_AUG_ROWS = HEAD_DIM + 16
SEL_TILE = 4 * Q_BLOCK
WIN_SPAN = WINDOW + Q_BLOCK


def _nsa_kernel(q_ref, gt_ref, kc_ref, vct_ref, ks_ref, vst_ref, kw_ref, vwt_ref, slope_ref, bias_ref,
                selmat_ref, emat_ref, gn_ref, o_ref, *, n_cmp, n_top, win_span):
    qt = pl.program_id(1)
    q0 = qt * Q_BLOCK
    R = Q_PER_KV
    cols = R * Q_BLOCK
    n_rows_c = kc_ref.shape[2]
    n_sel = selmat_ref.shape[0]
    q_loc = lax.broadcasted_iota(jnp.int32, (1, cols), 1) % Q_BLOCK
    t_col = q0 + q_loc
    gates_t = jax.nn.sigmoid(gt_ref[...]).T

    def step(carry, qs_bf, slope, bias, k_bf, v_tiles, offset, ok):
        m, acc = carry
        s = jnp.where(ok, _dot_nt(k_bf, qs_bf) + bias, NEG)
        c = slope * (-offset).astype(F32)
        m_new = jnp.maximum(m, jnp.max(s, axis=0, keepdims=True) + c)
        p = jnp.exp(s + (c - m_new)).astype(BF16)
        acc = jnp.exp(m - m_new) * acc
        for i, v in enumerate(v_tiles):
            acc = acc + jnp.dot(v, p[i * Q_BLOCK:(i + 1) * Q_BLOCK, :], preferred_element_type=F32)
        return m_new, acc

    init = (jnp.full((1, cols), NEG, F32), jnp.zeros((V_AUG_ROWS, cols), F32))
    finish = lambda acc: acc[0:HEAD_DIM, :] / acc[HEAD_DIM:HEAD_DIM + 1, :]
    y_rows = []
    for g in range(N_KV):
        qg = q_ref[:, g * R * HEAD_DIM:(g + 1) * R * HEAD_DIM] * (HEAD_DIM ** -0.5)
        qs = jnp.concatenate([qg[:, r * HEAD_DIM:(r + 1) * HEAD_DIM] for r in range(R)], axis=0)
        qs_bf = qs.astype(BF16)
        slope = slope_ref[g]

        kc = kc_ref[0, g]
        c_idx = lax.broadcasted_iota(jnp.int32, (n_rows_c, 1), 0)
        cmp_end = c_idx * CMP_STRIDE + (CMP_BLOCK - 1)
        valid_c = (cmp_end <= t_col) & (c_idx < n_cmp)
        q_hi, q_lo = _split2(qs)
        k_hi, k_lo = _split2(kc)
        s = _dot_nt(k_hi, q_hi) + (_dot_nt(k_hi, q_lo) + _dot_nt(k_lo, q_hi))
        s = s - slope * jnp.abs(t_col - cmp_end).astype(F32)
        s = jnp.where(valid_c, s, NEG)
        e = jnp.exp(s - jnp.max(s, axis=0, keepdims=True))
        p_c = e / jnp.sum(e, axis=0, keepdims=True)
        p_c = p_c * (t_col >= CMP_BLOCK - 1).astype(F32)
        o_c = jnp.dot(vct_ref[0, g].astype(BF16), p_c.astype(BF16), preferred_element_type=F32)

        p_sum = p_c[:, 0:Q_BLOCK]
        for r in range(1, R):
            p_sum = p_sum + p_c[:, r * Q_BLOCK:(r + 1) * Q_BLOCK]
        p_hi, p_lo = _split2(p_sum)
        dot = lambda a, b: jnp.dot(a, b, preferred_element_type=F32)
        imp_t = dot(selmat_ref[...], p_hi) + dot(selmat_ref[...], p_lo)
        j_idx = lax.broadcasted_iota(jnp.int32, (n_sel, Q_BLOCK), 0)
        j_f = j_idx.astype(F32)
        cur = t_col[:, 0:Q_BLOCK] // SEL_BLOCK
        forced = (j_idx == 0) | (j_idx == cur) | (j_idx == cur - 1)
        work = jnp.where(j_idx <= cur, imp_t + jnp.where(forced, FORCE_BONUS, 0.0), NEG)
        msel_t = jnp.zeros((n_sel, Q_BLOCK), F32)
        for _ in range(n_top):
            mx = jnp.max(work, axis=0, keepdims=True)
            first = jnp.min(jnp.where(work == mx, j_f, float(LANES)), axis=0, keepdims=True)
            pick = j_f == first
            msel_t = jnp.where(pick, 1.0, msel_t)
            work = jnp.where(pick, REMOVED, work)
        if n_sel < LANES:
            msel_t = jnp.concatenate([msel_t, jnp.zeros((LANES - n_sel, Q_BLOCK), F32)], axis=0)
        msel_bf = msel_t.astype(BF16)

        sel_key = lax.broadcasted_iota(jnp.int32, (SEL_TILE, cols), 0)
        bias_sel = bias_ref[g, 0:SEL_TILE, :]

        def sel_span(kt, carry, causal):
            k0 = pl.multiple_of(kt * SEL_TILE, SEL_TILE)
            ch = jnp.dot(emat_ref[kt], msel_bf, preferred_element_type=F32)
            ok = jnp.concatenate([ch] * R, axis=1) > 0.5
            if causal:
                ok = ok & (sel_key <= q_loc + (q0 - k0))
            v_tiles = [vst_ref[0, g, kt * (SEL_TILE // Q_BLOCK) + i] for i in range(SEL_TILE // Q_BLOCK)]
            return step(carry, qs_bf, slope, bias_sel, ks_ref[0, g, pl.ds(k0, SEL_TILE), :], v_tiles, q0 - k0, ok)

        kt_diag = q0 // SEL_TILE
        carry = sel_span(kt_diag, init, True)
        carry = lax.fori_loop(0, kt_diag, lambda kt, c: sel_span(kt, c, False), carry)
        o_s = finish(carry[1])

        w0 = pl.multiple_of(jnp.maximum(q0 + Q_BLOCK - win_span, 0), Q_BLOCK)
        win_key = lax.broadcasted_iota(jnp.int32, (win_span, cols), 0)
        reach = q_loc + (q0 - w0)
        ok_w = (win_key <= reach) & (win_key > reach - WINDOW)
        v_tiles = [vwt_ref[0, g, w0 // Q_BLOCK + i] for i in range(win_span // Q_BLOCK)]
        carry = step(init, qs_bf, slope, bias_ref[g, 0:win_span, :], kw_ref[0, g, pl.ds(w0, win_span), :],
                     v_tiles, q0 - w0, ok_w)
        o_w = finish(carry[1])

        for r in range(R):
            col = (g * R + r) * N_BRANCHES
            sl = slice(r * Q_BLOCK, (r + 1) * Q_BLOCK)
            y_rows.append(gates_t[col:col + 1, :] * o_c[:, sl] + gates_t[col + 1:col + 2, :] * o_s[:, sl]
                          + gates_t[col + 2:col + 3, :] * o_w[:, sl])
    y_t = jnp.concatenate(y_rows, axis=0)
    y_t = y_t * lax.rsqrt(jnp.mean(y_t * y_t, axis=0, keepdims=True) + LN_EPS) * gn_ref[...]
    o_ref[...] = y_t.T


def _value_tiles(v):
    B, G, S, hd = v.shape
    vt = v.transpose(0, 1, 3, 2)
    pad = jnp.zeros((B, G, V_AUG_ROWS - hd, S), v.dtype).at[:, :, 0, :].set(1.0)
    vt = jnp.concatenate([vt, pad], axis=2).reshape(B, G, V_AUG_ROWS, S // Q_BLOCK, Q_BLOCK)
    return vt.transpose(0, 1, 3, 2, 4).astype(BF16)


def _nsa(h_all, B, S, kc, vc, ks, vs, kw, vw, gn):
    nq = S // Q_BLOCK
    n_cmp = (S - CMP_BLOCK) // CMP_STRIDE + 1
    n_rows_c = kc.shape[2]
    n_sel = S // SEL_BLOCK
    n_top = min(SEL_TOPN, n_sel)
    n_span = S // SEL_TILE
    win_span = min(WIN_SPAN, S)
    n_bias = max(win_span, SEL_TILE)
    cols = Q_PER_KV * Q_BLOCK
    slopes = np.array([2.0 ** (-8.0 * (h + 1) / N_HEADS) for h in range(N_HEADS)], np.float32)
    slope_cols = np.repeat(slopes.reshape(N_KV, Q_PER_KV), Q_BLOCK, axis=1)[:, None, :]
    rel = (np.arange(cols)[None, :] % Q_BLOCK - np.arange(n_bias)[:, None]).astype(np.float32)
    bias = -slope_cols * rel[None]
    selmat_t = jnp.asarray(_selection_matrix(n_rows_c, n_cmp, n_sel).T, dtype=BF16)
    key_blk = (np.arange(S) // SEL_BLOCK).reshape(n_span, SEL_TILE, 1)
    emat_t = jnp.asarray((key_blk == np.arange(LANES)[None, None, :]).astype(np.float32), dtype=BF16)
    qw = Q_PER_KV * HEAD_DIM * N_KV
    per_batch = lambda *dims: pl.BlockSpec((1,) + dims, lambda b, i: (b,) + (0,) * len(dims))
    const = lambda shape: pl.BlockSpec(shape, lambda b, i: (0,) * len(shape))
    return pl.pallas_call(
        functools.partial(_nsa_kernel, n_cmp=n_cmp, n_top=n_top, win_span=win_span),
        grid=(B, nq),
        in_specs=[pl.BlockSpec((Q_BLOCK, qw), lambda b, i: (b * nq + i, COL_Q // qw)),
                  pl.BlockSpec((Q_BLOCK, LANES), lambda b, i: (b * nq + i, COL_GATES // LANES)),
                  per_batch(N_KV, n_rows_c, HEAD_DIM), per_batch(N_KV, HEAD_DIM, n_rows_c),
                  per_batch(N_KV, S, HEAD_DIM), per_batch(N_KV, nq, V_AUG_ROWS, Q_BLOCK),
                  per_batch(N_KV, S, HEAD_DIM), per_batch(N_KV, nq, V_AUG_ROWS, Q_BLOCK),
                  const((N_KV, 1, cols)), const((N_KV, n_bias, cols)), const((n_sel, n_rows_c)),
                  const((n_span, SEL_TILE, LANES)), const((NSA_WIDTH, 1))],
        out_specs=pl.BlockSpec((Q_BLOCK, NSA_WIDTH), lambda b, i: (b * nq + i, 0)),
        out_shape=jax.ShapeDtypeStruct((B * S, NSA_WIDTH), F32),
        compiler_params=_cparams(("arbitrary", "arbitrary")),
        name="nsa",
    )(h_all, h_all, kc, vc.transpose(0, 1, 3, 2), ks.astype(BF16), _value_tiles(vs), kw.astype(BF16),
      _value_tiles(vw), jnp.asarray(slope_cols), jnp.asarray(bias), selmat_t, emat_t, gn.reshape(NSA_WIDTH, 1))


def _out_ln_kernel(ylru_ref, ynsa_ref, x_ref, w_ref, g_ref, b_ref, o_ref, *, alpha):
    mix = jnp.dot(ylru_ref[...].astype(BF16), w_ref[0:LRU_WIDTH, :], preferred_element_type=F32)
    mix = mix + jnp.dot(ynsa_ref[...].astype(BF16), w_ref[LRU_WIDTH:, :], preferred_element_type=F32)
    o_ref[...] = _layer_norm_rows(alpha * x_ref[...] + mix, g_ref[...], b_ref[...])


def _out_ln(ylru, ynsa, xf, w_bf, g, b, alpha, tm=512):
    T = xf.shape[0]
    const = lambda shape: pl.BlockSpec(shape, lambda i: (0,) * len(shape))
    return pl.pallas_call(
        functools.partial(_out_ln_kernel, alpha=alpha),
        grid=(T // tm,),
        in_specs=[pl.BlockSpec((tm, LRU_WIDTH), lambda i: (i, 0)),
                  pl.BlockSpec((tm, NSA_WIDTH), lambda i: (i, 0)),
                  pl.BlockSpec((tm, D_MODEL), lambda i: (i, 0)),
                  const((D_MODEL, D_MODEL)), const((1, D_MODEL)), const((1, D_MODEL))],
        out_specs=pl.BlockSpec((tm, D_MODEL), lambda i: (i, 0)),
        out_shape=jax.ShapeDtypeStruct((T, D_MODEL), F32),
        compiler_params=_cparams(("arbitrary",)),
        name="out_ln",
    )(ylru, ynsa, xf, w_bf, g.reshape(1, D_MODEL), b.reshape(1, D_MODEL))


def _topk_rows(vals, order, payload, n_keep):
    n = vals.shape[1]
    kidx = lax.broadcasted_iota(jnp.int32, (n_keep, n), 0)
    never = float(1 << 20)
    kept_v = jnp.zeros((n_keep, n), F32)
    kept_p = jnp.zeros((n_keep, n), F32)
    work = vals
    for r in range(n_keep):
        mx = jnp.max(work, axis=0, keepdims=True)
        first = jnp.min(jnp.where(work == mx, order, never), axis=0, keepdims=True)
        pick = order == first
        kept_v = jnp.where(kidx == r, mx, kept_v)
        if payload is None:
            kept_p = jnp.where(kidx == r, first, kept_p)
        else:
            kept_p = jnp.where(kidx == r, jnp.max(jnp.where(pick, payload, -1.0), axis=0, keepdims=True), kept_p)
        work = jnp.where(pick, REMOVED, work)
    return kept_v, kept_p


def _candidate_rows():
    ij = [(i, 0) for i in range(PEER_TOPK)]
    for j in range(1, SUBLANES):
        ij += [(i, j) for i in range(SUBLANES)]
    ij += [(0, j) for j in range(SUBLANES, PEER_TOPK)]
    flat = np.array([i * PEER_TOPK + j for i, j in ij], np.float32)
    valid = np.array([(i + 1) * (j + 1) <= PEER_TOPK for i, j in ij])
    return flat[:, None], np.where(valid, 0.0, REMOVED).astype(np.float32)[:, None]


def _candidates(a0, a1):
    return ([(a0, a1[0:1, :])]
            + [(a0[0:SUBLANES, :], a1[j:j + 1, :]) for j in range(1, SUBLANES)]
            + [(a0[0:1, :], a1[SUBLANES:PEER_TOPK, :])])


def _peer_route_kernel(x_ref, wq_ref, sk_ref, flat_ref, pad_ref, e_ref, p_ref, g_ref):
    tm = x_ref.shape[0]
    q = jnp.dot(x_ref[...].astype(BF16), wq_ref[...], preferred_element_type=F32)
    key_idx = lax.broadcasted_iota(jnp.int32, (PEER_KEYS, tm), 0).astype(F32)
    flat = jnp.broadcast_to(flat_ref[...], (flat_ref.shape[0], tm))
    experts, gates = [], []
    for h in range(PEER_HEADS):
        sv, si = [], []
        for c in range(2):
            col = (h * 2 + c) * PEER_KEY_DIM
            s_t = _dot_nt(sk_ref[c], q[:, col:col + PEER_KEY_DIM], HI)
            v, i = _topk_rows(s_t, key_idx, None, PEER_TOPK)
            sv.append(v)
            si.append(i)
        cand = jnp.concatenate([a + b for a, b in _candidates(sv[0], sv[1])], axis=0) + pad_ref[...]
        ecand = jnp.concatenate([a * float(PEER_KEYS) + b for a, b in _candidates(si[0], si[1])], axis=0)
        cv, ce = _topk_rows(cand, flat, ecand, PEER_TOPK)
        ex = jnp.exp(cv - cv[0:1, :])
        gates.append(ex / jnp.sum(ex, axis=0, keepdims=True))
        experts.append(ce)
    e_t = jnp.concatenate(experts, axis=0)
    upper = (e_t >= float(HALF_EXPERTS)).astype(F32)
    e_ref[...] = (e_t - float(HALF_EXPERTS) * upper).T.astype(jnp.int32)
    p_ref[...] = upper.T
    g_ref[...] = jnp.concatenate(gates, axis=0).T


def _peer_route(x1, wq_bf, subkeys, tm=256):
    T = x1.shape[0]
    tm = min(tm, T)
    nq = wq_bf.shape[1]
    flat, pad = _candidate_rows()
    n_cand = flat.shape[0]
    out = lambda: pl.BlockSpec((tm, N_PAIRS), lambda i: (i, 0))
    return pl.pallas_call(
        _peer_route_kernel,
        grid=(T // tm,),
        in_specs=[pl.BlockSpec((tm, D_MODEL), lambda i: (i, 0)),
                  pl.BlockSpec((D_MODEL, nq), lambda i: (0, 0)),
                  pl.BlockSpec((2, PEER_KEYS, PEER_KEY_DIM), lambda i: (0, 0, 0)),
                  pl.BlockSpec((n_cand, 1), lambda i: (0, 0)),
                  pl.BlockSpec((n_cand, 1), lambda i: (0, 0))],
        out_specs=[out(), out(), out()],
        out_shape=[jax.ShapeDtypeStruct((T, N_PAIRS), jnp.int32),
                   jax.ShapeDtypeStruct((T, N_PAIRS), F32),
                   jax.ShapeDtypeStruct((T, N_PAIRS), F32)],
        compiler_params=_cparams(("arbitrary",)),
        name="peer_route",
    )(x1, wq_bf, subkeys, jnp.asarray(flat), jnp.asarray(pad))


def _pack_table(tab):
    t = tab.astype(BF16).reshape(2, HALF_EXPERTS, D_ROWS, LANES)
    return t.transpose(1, 0, 2, 3).reshape(HALF_EXPERTS, 2 * D_ROWS, LANES)


def _sublane_sums(ps, roll, where, sub):
    lvl = ps
    for r in (1, 2, 4):
        m = (sub % (2 * r)) < r
        half = len(lvl) // 2
        lvl = [where(m, lvl[i], lvl[i + half]) + roll(where(m, lvl[i + half], lvl[i]), r) for i in range(half)]
    return lvl[0]


def _sum_src():
    sub = np.arange(SUBLANES)[:, None] * np.ones((1, LANES), np.int64)
    ps = [np.full((SUBLANES, LANES), 10.0 ** i) for i in range(SUBLANES)]
    out = _sublane_sums(ps, lambda v, r: np.roll(v, r, axis=0), np.where, sub)
    return [int(round(np.log10(out[s, 0] / SUBLANES))) for s in range(SUBLANES)]


_SUM_SRC = _sum_src()


def _split_dot_ones(d, ones_bf):
    hi = d.astype(BF16)
    r1 = d - hi.astype(F32)
    mid = r1.astype(BF16)
    lo = (r1 - mid.astype(F32)).astype(BF16)
    dot = lambda a: jnp.dot(a, ones_bf, preferred_element_type=F32)
    return (dot(hi) + dot(mid)) + dot(lo)


def _eye():
    return (lax.broadcasted_iota(jnp.int32, (N_PAIRS, LANES), 0)
            == lax.broadcasted_iota(jnp.int32, (N_PAIRS, LANES), 1))


def _peer_u_kernel(ej_ref, x_ref, tab_ref, gate_ref, par_ref, w_ref, part_a, part_b, prep_a, prep_b, zt_ref, *, tb):
    sub = lax.broadcasted_iota(jnp.int32, (SUBLANES, LANES), 0)
    lane = lax.broadcasted_iota(jnp.int32, (N_PAIRS, tb), 1)
    eye = _eye()
    ones_bf = jnp.ones((LANES, LANES), BF16)

    def replicate_parity(t, prep_ref):
        prow = jnp.broadcast_to(par_ref[pl.ds(t, 1), :], (N_PAIRS, LANES))
        prep_ref[...] = jnp.dot(jnp.where(eye, prow, 0.0).astype(BF16), ones_bf, preferred_element_type=F32)

    def gather_dots(t, prep_ref, part_ref):
        xt = x_ref[t]
        for grp in range(N_PAIRS // SUBLANES):
            prods = [None] * SUBLANES
            for s in range(SUBLANES):
                k = grp * SUBLANES + s
                tile = tab_ref[ej_ref[t * N_PAIRS + k]].astype(F32)
                upper = jnp.broadcast_to(prep_ref[k:k + 1, :], (D_ROWS, LANES)) > 0.5
                prods[_SUM_SRC[s]] = jnp.where(upper, tile[D_ROWS:], tile[:D_ROWS]) * xt
            part_ref[grp * SUBLANES:(grp + 1) * SUBLANES, :] = _sublane_sums(
                prods, lambda v, r: pltpu.roll(v, r, axis=0), jnp.where, sub)

    def lane_sums(t, part_ref):
        z = _split_dot_ones(part_ref[...], ones_bf)
        zt_ref[...] = jnp.where(lane == t, z, zt_ref[...])

    replicate_parity(0, prep_a)

    def two_tokens(i, carry):
        t0 = 2 * i
        t1 = t0 + 1
        replicate_parity(t1, prep_b)
        gather_dots(t0, prep_a, part_a)
        lane_sums(t0, part_a)
        replicate_parity(jnp.minimum(t1 + 1, tb - 1), prep_a)
        gather_dots(t1, prep_b, part_b)
        lane_sums(t1, part_b)
        return carry

    lax.fori_loop(0, tb // 2, two_tokens, 0)
    w_ref[...] = jax.nn.gelu(zt_ref[...].T) * gate_ref[...]


def _peer_u(ej, x3, tab, gates, parf, tb=128):
    T = x3.shape[0]
    vm = lambda: pl.BlockSpec((tb, N_PAIRS), lambda i: (i, 0))
    return pl.pallas_call(
        functools.partial(_peer_u_kernel, tb=tb),
        grid=(T // tb,),
        in_specs=[pl.BlockSpec((tb * N_PAIRS,), lambda i: (i,), memory_space=pltpu.SMEM),
                  pl.BlockSpec((tb, D_ROWS, LANES), lambda i: (i, 0, 0)),
                  pl.BlockSpec((HALF_EXPERTS, 2 * D_ROWS, LANES), lambda i: (0, 0, 0),
                               pipeline_mode=pl.Buffered(1)),
                  vm(), vm()],
        out_specs=vm(),
        out_shape=jax.ShapeDtypeStruct((T, N_PAIRS), F32),
        scratch_shapes=[pltpu.VMEM((N_PAIRS, LANES), F32)] * 4 + [pltpu.VMEM((N_PAIRS, tb), F32)],
        compiler_params=_cparams(("arbitrary",)),
        name="peer_u",
    )(ej.reshape(T * N_PAIRS), x3, tab, gates, parf)


def _peer_v_kernel(ej_ref, w_ref, par_ref, x_ref, tab_ref, g_ref, b_ref, o_ref, y_ref, wlo_a, whi_a, wlo_b, whi_b,
                   *, tb, alpha):
    n_chain = 2
    eye = _eye()
    ones_bf = jnp.ones((LANES, LANES), BF16)

    def replicate_weights(t, wlo_ref, whi_ref):
        wrow = jnp.broadcast_to(w_ref[pl.ds(t, 1), :], (N_PAIRS, LANES))
        prow = jnp.broadcast_to(par_ref[pl.ds(t, 1), :], (N_PAIRS, LANES))
        wrep = _split_dot_ones(jnp.where(eye, wrow, 0.0), ones_bf)
        prep = jnp.dot(jnp.where(eye, prow, 0.0).astype(BF16), ones_bf, preferred_element_type=F32)
        whi = wrep * prep
        whi_ref[...] = whi
        wlo_ref[...] = wrep - whi

    def gather_sum(t, wlo_ref, whi_ref):
        zero = jnp.zeros((D_ROWS, LANES), F32)
        accs = [zero] * (2 * n_chain)
        for k in range(N_PAIRS):
            tile = tab_ref[ej_ref[t * N_PAIRS + k]].astype(F32)
            a = k % n_chain
            accs[2 * a] = accs[2 * a] + tile[:D_ROWS] * wlo_ref[k:k + 1, :]
            accs[2 * a + 1] = accs[2 * a + 1] + tile[D_ROWS:] * whi_ref[k:k + 1, :]
        y_ref[t] = (accs[0] + accs[1]) + (accs[2] + accs[3])

    replicate_weights(0, wlo_a, whi_a)

    def two_tokens(i, carry):
        t0 = 2 * i
        t1 = t0 + 1
        replicate_weights(t1, wlo_b, whi_b)
        gather_sum(t0, wlo_a, whi_a)
        replicate_weights(jnp.minimum(t1 + 1, tb - 1), wlo_a, whi_a)
        gather_sum(t1, wlo_b, whi_b)
        return carry

    lax.fori_loop(0, tb // 2, two_tokens, 0)
    z = alpha * x_ref[...] + y_ref[...]
    inv_d = 1.0 / D_MODEL
    mu = jnp.sum(jnp.sum(z, axis=2, keepdims=True), axis=1, keepdims=True) * inv_d
    zc = z - mu
    var = jnp.sum(jnp.sum(zc * zc, axis=2, keepdims=True), axis=1, keepdims=True) * inv_d
    o_ref[...] = zc * lax.rsqrt(var + LN_EPS) * g_ref[...] + b_ref[...]


def _peer_v(ej, wgt, parf, x3, tab, g, b, alpha, tb=128):
    T = x3.shape[0]
    vm = lambda: pl.BlockSpec((tb, N_PAIRS), lambda i: (i, 0))
    return pl.pallas_call(
        functools.partial(_peer_v_kernel, tb=tb, alpha=alpha),
        grid=(T // tb,),
        in_specs=[pl.BlockSpec((tb * N_PAIRS,), lambda i: (i,), memory_space=pltpu.SMEM),
                  vm(), vm(),
                  pl.BlockSpec((tb, D_ROWS, LANES), lambda i: (i, 0, 0)),
                  pl.BlockSpec((HALF_EXPERTS, 2 * D_ROWS, LANES), lambda i: (0, 0, 0),
                               pipeline_mode=pl.Buffered(1)),
                  pl.BlockSpec((1, D_ROWS, LANES), lambda i: (0, 0, 0)),
                  pl.BlockSpec((1, D_ROWS, LANES), lambda i: (0, 0, 0))],
        out_specs=pl.BlockSpec((tb, D_ROWS, LANES), lambda i: (i, 0, 0)),
        out_shape=jax.ShapeDtypeStruct((T, D_ROWS, LANES), F32),
        scratch_shapes=[pltpu.VMEM((tb, D_ROWS, LANES), F32)] + [pltpu.VMEM((N_PAIRS, LANES), F32)] * 4,
        compiler_params=_cparams(("arbitrary",)),
        name="peer_v",
    )(ej.reshape(T * N_PAIRS), wgt, parf, x3, tab, g.reshape(1, D_ROWS, LANES), b.reshape(1, D_ROWS, LANES))


def kernel(x, w_in, b_in, conv_w, conv_b, lru_wa, lru_ba, lru_wx, lru_bx, lru_lambda, cmp_pos_k, cmpk_w1, cmpk_b1, cmpk_w2, cmpk_b2, cmp_pos_v, cmpv_w1, cmpv_b1, cmpv_w2, cmpv_b2, gn_lru_g, gn_nsa_g, w_out, ln1_g, ln1_b, peer_wq, peer_subkeys, peer_u, peer_v, ln2_g, ln2_b):
    B, S, D = x.shape
    T = B * S
    depth = w_in.shape[0]
    alpha = (2 * depth) ** 0.25
    fan = CMP_STRIDE * HEAD_DIM
    xf = x.reshape(T, D)
    for l in range(depth):
        w_in_bf = jnp.pad(w_in[l], ((0, 0), (0, IN_COLS_PAD - IN_COLS))).astype(BF16)
        b_in_row = jnp.pad(b_in[l], (0, IN_COLS_PAD - IN_COLS)).reshape(1, IN_COLS_PAD)
        h_all = _in_proj(xf, w_in_bf, b_in_row)

        y_lru = _rglru(h_all, B, S, conv_w[l], conv_b[l], _block_diag(lru_wa[l]), lru_ba[l],
                       _block_diag(lru_wx[l]), lru_bx[l], lru_lambda[l], gn_lru_g[l])

        kv = h_all[:, COL_KV:COL_KV + 6 * KV_WIDTH].reshape(B, S, 6, N_KV, HEAD_DIM).transpose(2, 0, 3, 1, 4)
        kr = kv[0:2].reshape(2, B * N_KV, S // CMP_STRIDE, fan)
        w1 = jnp.stack([cmpk_w1[l], cmpv_w1[l]])
        w1ab = jnp.concatenate([w1[:, :fan], w1[:, fan:]], axis=2)
        pos = jnp.stack([cmp_pos_k[l], cmp_pos_v[l]]).reshape(2, 1, CMP_BLOCK * HEAD_DIM)
        posflat = jnp.broadcast_to(pos, (2, SUBLANES, CMP_BLOCK * HEAD_DIM))
        kcvc = _compress(kr, w1ab, w1, posflat,
                         jnp.stack([cmpk_b1[l], cmpv_b1[l]]).reshape(2, 1, HEAD_DIM),
                         jnp.stack([cmpk_w2[l], cmpv_w2[l]]),
                         jnp.stack([cmpk_b2[l], cmpv_b2[l]]).reshape(2, 1, HEAD_DIM))
        kcvc = kcvc.reshape(2, B, N_KV, S // CMP_STRIDE, HEAD_DIM)
        y_nsa = _nsa(h_all, B, S, kcvc[0], kcvc[1], kv[2], kv[3], kv[4], kv[5], gn_nsa_g[l])

        x1 = _out_ln(y_lru, y_nsa, xf, w_out[l].astype(BF16), ln1_g[l], ln1_b[l], alpha)

        ej, parf, gates = _peer_route(x1, peer_wq[l].astype(BF16), peer_subkeys[l])
        x3 = x1.reshape(T, D_ROWS, LANES)
        wgt = _peer_u(ej, x3, _pack_table(peer_u[l]), gates, parf)
        xf = _peer_v(ej, wgt, parf, x3, _pack_table(peer_v[l]), ln2_g[l], ln2_b[l], alpha).reshape(T, D)
    return xf.reshape(B, S, D)
```

```python
import functools

import numpy as np
import jax
import jax.numpy as jnp
from jax import lax
from jax.experimental import pallas as pl
from jax.experimental.pallas import tpu as pltpu

D_MODEL = 1024
LRU_WIDTH = 512
LRU_BLOCKS = 8
LRU_BLOCK_DIM = LRU_WIDTH // LRU_BLOCKS
CONV_WIDTH = 4
LRU_C = 8.0
N_HEADS = 8
HEAD_DIM = 64
N_KV = 2
Q_PER_KV = N_HEADS // N_KV
NSA_WIDTH = N_HEADS * HEAD_DIM
KV_WIDTH = N_KV * HEAD_DIM
N_BRANCHES = 3
CMP_BLOCK = 32
CMP_STRIDE = 16
SEL_BLOCK = 64
SEL_TOPN = 16
WINDOW = 512
Q_BLOCK = 128
FORCE_BONUS = 1e4
NEG = -1e30
REMOVED = -3.0e38
PEER_HEADS = 8
PEER_KEYS = 128
PEER_TOPK = 16
PEER_KEY_DIM = 128
N_EXPERTS = PEER_KEYS * PEER_KEYS
N_PAIRS = PEER_HEADS * PEER_TOPK
LN_EPS = 1e-5
IN_SPLITS = (LRU_WIDTH, LRU_WIDTH, NSA_WIDTH) + (KV_WIDTH,) * 6 + (N_HEADS * N_BRANCHES,)
IN_COLS = sum(IN_SPLITS)

LANES = 128
SUBLANES = 8
VMEM_LIMIT_BYTES = 56 * 1024 * 1024

IN_COLS_PAD = -(-IN_COLS // LANES) * LANES
COL_LRU_X = 0
COL_LRU_GATE = LRU_WIDTH
COL_Q = 2 * LRU_WIDTH
COL_KV = COL_Q + NSA_WIDTH
COL_GATES = COL_KV + 6 * KV_WIDTH

HI = lax.Precision.HIGHEST
F32 = jnp.float32
BF16 = jnp.bfloat16

D_ROWS = D_MODEL // LANES


def _cparams(sem):
    return pltpu.CompilerParams(dimension_semantics=sem, vmem_limit_bytes=VMEM_LIMIT_BYTES)


def _dot_nt(a, b, precision=None):
    return lax.dot_general(a, b, (((1,), (1,)), ((), ())), precision=precision,
                           preferred_element_type=F32)


def _split2(a):
    hi = a.astype(BF16)
    return hi, (a - hi.astype(F32)).astype(BF16)


def _layer_norm_rows(z, g, b):
    mu = jnp.mean(z, axis=-1, keepdims=True)
    zc = z - mu
    var = jnp.mean(zc * zc, axis=-1, keepdims=True)
    return zc * lax.rsqrt(var + LN_EPS) * g + b


def _in_proj_kernel(x_ref, w_ref, b_ref, o_ref):
    o_ref[...] = jnp.dot(x_ref[...].astype(BF16), w_ref[...], preferred_element_type=F32) + b_ref[...]


def _in_proj(xf, w_bf, b_row, tm=512):
    T = xf.shape[0]
    return pl.pallas_call(
        _in_proj_kernel,
        grid=(T // tm,),
        in_specs=[pl.BlockSpec((tm, D_MODEL), lambda i: (i, 0)),
                  pl.BlockSpec((D_MODEL, IN_COLS_PAD), lambda i: (0, 0)),
                  pl.BlockSpec((1, IN_COLS_PAD), lambda i: (0, 0))],
        out_specs=pl.BlockSpec((tm, IN_COLS_PAD), lambda i: (i, 0)),
        out_shape=jax.ShapeDtypeStruct((T, IN_COLS_PAD), F32),
        compiler_params=_cparams(("arbitrary",)),
        name="in_proj",
    )(xf, w_bf, b_row)


def _rglru_kernel(x_ref, gate_ref, cw_ref, cb_ref, wa_ref, ba_ref, wx_ref, bx_ref, lam_ref, gn_ref,
                  o_ref, xprev_ref, h_ref, *, ts):
    i = pl.program_id(1)

    @pl.when(i == 0)
    def _():
        xprev_ref[...] = jnp.zeros_like(xprev_ref)
        h_ref[...] = jnp.zeros_like(h_ref)

    xb = x_ref[...]
    xp = xprev_ref[...]
    row = lax.broadcasted_iota(jnp.int32, xb.shape, 0)
    xc = cw_ref[CONV_WIDTH - 1:CONV_WIDTH, :] * xb + cb_ref[...]
    for j in range(1, CONV_WIDTH):
        shifted = jnp.where(row >= j, pltpu.roll(xb, j, axis=0), pltpu.roll(xp, j, axis=0))
        xc = xc + cw_ref[CONV_WIDTH - 1 - j:CONV_WIDTH - j, :] * shifted
    xprev_ref[...] = xb

    r = jax.nn.sigmoid(jnp.dot(xc, wa_ref[...], precision=HI, preferred_element_type=F32) + ba_ref[...])
    ig = jax.nn.sigmoid(jnp.dot(xc, wx_ref[...], precision=HI, preferred_element_type=F32) + bx_ref[...])
    lam = lam_ref[...]
    softplus_neg_lam = jnp.maximum(-lam, 0.0) + jnp.log1p(jnp.exp(-jnp.abs(lam)))
    log_a = -LRU_C * r * softplus_neg_lam
    a = jnp.exp(log_a)
    th = jnp.tanh(log_a)
    u = jnp.sqrt(-2.0 * th / (1.0 - th)) * (ig * xc)

    d = 1
    while d < ts:
        keep = row >= d
        a_sh = pltpu.roll(a, d, axis=0)
        u_sh = pltpu.roll(u, d, axis=0)
        u = jnp.where(keep, a * u_sh + u, u)
        a = jnp.where(keep, a * a_sh, a)
        d *= 2
    h = u + a * h_ref[0:1, :]
    h_ref[0:1, :] = h[ts - 1:ts, :]

    y = h * jax.nn.gelu(gate_ref[...])
    y = y * lax.rsqrt(jnp.mean(y * y, axis=-1, keepdims=True) + LN_EPS) * gn_ref[...]
    o_ref[...] = y


def _rglru(h_all, B, S, cw, cb, wa_bd, ba, wx_bd, bx, lam, gn, ts=512):
    ts = min(ts, S)
    nt = S // ts
    C = LRU_WIDTH
    row = lambda v: v.reshape(1, C)
    const = lambda shape: pl.BlockSpec(shape, lambda b, i: (0,) * len(shape))
    return pl.pallas_call(
        functools.partial(_rglru_kernel, ts=ts),
        grid=(B, nt),
        in_specs=[pl.BlockSpec((ts, C), lambda b, i: (b * nt + i, COL_LRU_X // C)),
                  pl.BlockSpec((ts, C), lambda b, i: (b * nt + i, COL_LRU_GATE // C)),
                  const((CONV_WIDTH, C)), const((1, C)), const((C, C)), const((1, C)),
                  const((C, C)), const((1, C)), const((1, C)), const((1, C))],
        out_specs=pl.BlockSpec((ts, C), lambda b, i: (b * nt + i, 0)),
        out_shape=jax.ShapeDtypeStruct((B * S, C), F32),
        scratch_shapes=[pltpu.VMEM((ts, C), F32), pltpu.VMEM((SUBLANES, C), F32)],
        compiler_params=_cparams(("arbitrary", "arbitrary")),
        name="rglru",
    )(h_all, h_all, cw, row(cb), wa_bd, row(ba), wx_bd, row(bx), row(lam), row(gn))


def _block_diag(w):
    n, d, e = w.shape
    eye = jnp.eye(n, dtype=w.dtype)
    return (w[:, :, None, :] * eye[:, None, :, None]).reshape(n * d, n * e)


def _compress_kernel(kr_ref, w1ab_ref, w1_ref, pos_ref, b1_ref, w2_ref, b2_ref, o_ref):
    kr = kr_ref[0, 0]
    nr = kr.shape[0]
    ab = jnp.dot(kr, w1ab_ref[0], precision=HI, preferred_element_type=F32)
    ab_next = pltpu.roll(ab, nr - 1, axis=0)
    pre = ab[:, :HEAD_DIM] + ab_next[:, HEAD_DIM:]
    posc = jnp.dot(pos_ref[0], w1_ref[0], precision=HI, preferred_element_type=F32)[0:1, :]
    h1 = jax.nn.gelu(pre + posc + b1_ref[0])
    o_ref[0, 0] = jnp.dot(h1, w2_ref[0], precision=HI, preferred_element_type=F32) + b2_ref[0]


def _compress(kr, w1ab, w1, posflat, b1, w2, b2):
    _, BG, nr, fan = kr.shape
    wspec = lambda shape: pl.BlockSpec((1,) + shape, lambda s, i: (s,) + (0,) * len(shape))
    return pl.pallas_call(
        _compress_kernel,
        grid=(2, BG),
        in_specs=[pl.BlockSpec((1, 1, nr, fan), lambda s, i: (s, i, 0, 0)),
                  wspec((fan, 2 * HEAD_DIM)), wspec((2 * fan, HEAD_DIM)), wspec((SUBLANES, 2 * fan)),
                  wspec((1, HEAD_DIM)), wspec((HEAD_DIM, HEAD_DIM)), wspec((1, HEAD_DIM))],
        out_specs=pl.BlockSpec((1, 1, nr, HEAD_DIM), lambda s, i: (s, i, 0, 0)),
        out_shape=jax.ShapeDtypeStruct((2, BG, nr, HEAD_DIM), F32),
        compiler_params=_cparams(("arbitrary", "arbitrary")),
        name="compress",
    )(kr, w1ab, w1, posflat, b1, w2, b2)


def _selection_matrix(n_rows, n_cmp, n_sel):
    cs = np.arange(n_rows)[:, None] * CMP_STRIDE
    ss = np.arange(n_sel)[None, :] * SEL_BLOCK
    ov = np.clip(np.minimum(cs + CMP_BLOCK, ss + SEL_BLOCK) - np.maximum(cs, ss), 0, None)
    ov = (ov / CMP_STRIDE).astype(np.float32)
    ov[n_cmp:] = 0.0
    return ov


V_AUG_ROWS = HEAD_DIM + 16
SEL_TILE = 4 * Q_BLOCK
WIN_SPAN = WINDOW + Q_BLOCK


def _nsa_kernel(q_ref, gt_ref, kc_ref, vct_ref, ks_ref, vst_ref, kw_ref, vwt_ref, slope_ref, bias_ref,
                selmat_ref, emat_ref, gn_ref, o_ref, *, n_cmp, n_top, win_span):
    qt = pl.program_id(1)
    q0 = qt * Q_BLOCK
    R = Q_PER_KV
    cols = R * Q_BLOCK
    n_rows_c = kc_ref.shape[2]
    n_sel = selmat_ref.shape[0]
    q_loc = lax.broadcasted_iota(jnp.int32, (1, cols), 1) % Q_BLOCK
    t_col = q0 + q_loc
    gates_t = jax.nn.sigmoid(gt_ref[...]).T

    def step(carry, qs_bf, slope, bias, k_bf, v_tiles, offset, ok):
        m, acc = carry
        s = jnp.where(ok, _dot_nt(k_bf, qs_bf) + bias, NEG)
        c = slope * (-offset).astype(F32)
        m_new = jnp.maximum(m, jnp.max(s, axis=0, keepdims=True) + c)
        p = jnp.exp(s + (c - m_new)).astype(BF16)
        acc = jnp.exp(m - m_new) * acc
        for i, v in enumerate(v_tiles):
            acc = acc + jnp.dot(v, p[i * Q_BLOCK:(i + 1) * Q_BLOCK, :], preferred_element_type=F32)
        return m_new, acc

    init = (jnp.full((1, cols), NEG, F32), jnp.zeros((V_AUG_ROWS, cols), F32))
    finish = lambda acc: acc[0:HEAD_DIM, :] / acc[HEAD_DIM:HEAD_DIM + 1, :]
    y_rows = []
    for g in range(N_KV):
        qg = q_ref[:, g * R * HEAD_DIM:(g + 1) * R * HEAD_DIM] * (HEAD_DIM ** -0.5)
        qs = jnp.concatenate([qg[:, r * HEAD_DIM:(r + 1) * HEAD_DIM] for r in range(R)], axis=0)
        qs_bf = qs.astype(BF16)
        slope = slope_ref[g]

        kc = kc_ref[0, g]
        c_idx = lax.broadcasted_iota(jnp.int32, (n_rows_c, 1), 0)
        cmp_end = c_idx * CMP_STRIDE + (CMP_BLOCK - 1)
        valid_c = (cmp_end <= t_col) & (c_idx < n_cmp)
        q_hi, q_lo = _split2(qs)
        k_hi, k_lo = _split2(kc)
        s = _dot_nt(k_hi, q_hi) + (_dot_nt(k_hi, q_lo) + _dot_nt(k_lo, q_hi))
        s = s - slope * jnp.abs(t_col - cmp_end).astype(F32)
        s = jnp.where(valid_c, s, NEG)
        e = jnp.exp(s - jnp.max(s, axis=0, keepdims=True))
        p_c = e / jnp.sum(e, axis=0, keepdims=True)
        p_c = p_c * (t_col >= CMP_BLOCK - 1).astype(F32)
        o_c = jnp.dot(vct_ref[0, g].astype(BF16), p_c.astype(BF16), preferred_element_type=F32)

        p_sum = p_c[:, 0:Q_BLOCK]
        for r in range(1, R):
            p_sum = p_sum + p_c[:, r * Q_BLOCK:(r + 1) * Q_BLOCK]
        p_hi, p_lo = _split2(p_sum)
        dot = lambda a, b: jnp.dot(a, b, preferred_element_type=F32)
        imp_t = dot(selmat_ref[...], p_hi) + dot(selmat_ref[...], p_lo)
        j_idx = lax.broadcasted_iota(jnp.int32, (n_sel, Q_BLOCK), 0)
        j_f = j_idx.astype(F32)
        cur = t_col[:, 0:Q_BLOCK] // SEL_BLOCK
        forced = (j_idx == 0) | (j_idx == cur) | (j_idx == cur - 1)
        work = jnp.where(j_idx <= cur, imp_t + jnp.where(forced, FORCE_BONUS, 0.0), NEG)
        msel_t = jnp.zeros((n_sel, Q_BLOCK), F32)
        for _ in range(n_top):
            mx = jnp.max(work, axis=0, keepdims=True)
            first = jnp.min(jnp.where(work == mx, j_f, float(LANES)), axis=0, keepdims=True)
            pick = j_f == first
            msel_t = jnp.where(pick, 1.0, msel_t)
            work = jnp.where(pick, REMOVED, work)
        if n_sel < LANES:
            msel_t = jnp.concatenate([msel_t, jnp.zeros((LANES - n_sel, Q_BLOCK), F32)], axis=0)
        msel_bf = msel_t.astype(BF16)

        sel_key = lax.broadcasted_iota(jnp.int32, (SEL_TILE, cols), 0)
        bias_sel = bias_ref[g, 0:SEL_TILE, :]

        def sel_span(kt, carry, causal):
            k0 = pl.multiple_of(kt * SEL_TILE, SEL_TILE)
            ch = jnp.dot(emat_ref[kt], msel_bf, preferred_element_type=F32)
            ok = jnp.concatenate([ch] * R, axis=1) > 0.5
            if causal:
                ok = ok & (sel_key <= q_loc + (q0 - k0))
            v_tiles = [vst_ref[0, g, kt * (SEL_TILE // Q_BLOCK) + i] for i in range(SEL_TILE // Q_BLOCK)]
            return step(carry, qs_bf, slope, bias_sel, ks_ref[0, g, pl.ds(k0, SEL_TILE), :], v_tiles, q0 - k0, ok)

        kt_diag = q0 // SEL_TILE
        carry = sel_span(kt_diag, init, True)
        carry = lax.fori_loop(0, kt_diag, lambda kt, c: sel_span(kt, c, False), carry)
        o_s = finish(carry[1])

        w0 = pl.multiple_of(jnp.maximum(q0 + Q_BLOCK - win_span, 0), Q_BLOCK)
        win_key = lax.broadcasted_iota(jnp.int32, (win_span, cols), 0)
        reach = q_loc + (q0 - w0)
        ok_w = (win_key <= reach) & (win_key > reach - WINDOW)
        v_tiles = [vwt_ref[0, g, w0 // Q_BLOCK + i] for i in range(win_span // Q_BLOCK)]
        carry = step(init, qs_bf, slope, bias_ref[g, 0:win_span, :], kw_ref[0, g, pl.ds(w0, win_span), :],
                     v_tiles, q0 - w0, ok_w)
        o_w = finish(carry[1])

        for r in range(R):
            col = (g * R + r) * N_BRANCHES
            sl = slice(r * Q_BLOCK, (r + 1) * Q_BLOCK)
            y_rows.append(gates_t[col:col + 1, :] * o_c[:, sl] + gates_t[col + 1:col + 2, :] * o_s[:, sl]
                          + gates_t[col + 2:col + 3, :] * o_w[:, sl])
    y_t = jnp.concatenate(y_rows, axis=0)
    y_t = y_t * lax.rsqrt(jnp.mean(y_t * y_t, axis=0, keepdims=True) + LN_EPS) * gn_ref[...]
    o_ref[...] = y_t.T


def _value_tiles(v):
    B, G, S, hd = v.shape
    vt = v.transpose(0, 1, 3, 2)
    pad = jnp.zeros((B, G, V_AUG_ROWS - hd, S), v.dtype).at[:, :, 0, :].set(1.0)
    vt = jnp.concatenate([vt, pad], axis=2).reshape(B, G, V_AUG_ROWS, S // Q_BLOCK, Q_BLOCK)
    return vt.transpose(0, 1, 3, 2, 4).astype(BF16)


def _nsa(h_all, B, S, kc, vc, ks, vs, kw, vw, gn):
    nq = S // Q_BLOCK
    n_cmp = (S - CMP_BLOCK) // CMP_STRIDE + 1
    n_rows_c = kc.shape[2]
    n_sel = S // SEL_BLOCK
    n_top = min(SEL_TOPN, n_sel)
    n_span = S // SEL_TILE
    win_span = min(WIN_SPAN, S)
    n_bias = max(win_span, SEL_TILE)
    cols = Q_PER_KV * Q_BLOCK
    slopes = np.array([2.0 ** (-8.0 * (h + 1) / N_HEADS) for h in range(N_HEADS)], np.float32)
    slope_cols = np.repeat(slopes.reshape(N_KV, Q_PER_KV), Q_BLOCK, axis=1)[:, None, :]
    rel = (np.arange(cols)[None, :] % Q_BLOCK - np.arange(n_bias)[:, None]).astype(np.float32)
    bias = -slope_cols * rel[None]
    selmat_t = jnp.asarray(_selection_matrix(n_rows_c, n_cmp, n_sel).T, dtype=BF16)
    key_blk = (np.arange(S) // SEL_BLOCK).reshape(n_span, SEL_TILE, 1)
    emat_t = jnp.asarray((key_blk == np.arange(LANES)[None, None, :]).astype(np.float32), dtype=BF16)
    qw = Q_PER_KV * HEAD_DIM * N_KV
    per_batch = lambda *dims: pl.BlockSpec((1,) + dims, lambda b, i: (b,) + (0,) * len(dims))
    const = lambda shape: pl.BlockSpec(shape, lambda b, i: (0,) * len(shape))
    return pl.pallas_call(
        functools.partial(_nsa_kernel, n_cmp=n_cmp, n_top=n_top, win_span=win_span),
        grid=(B, nq),
        in_specs=[pl.BlockSpec((Q_BLOCK, qw), lambda b, i: (b * nq + i, COL_Q // qw)),
                  pl.BlockSpec((Q_BLOCK, LANES), lambda b, i: (b * nq + i, COL_GATES // LANES)),
                  per_batch(N_KV, n_rows_c, HEAD_DIM), per_batch(N_KV, HEAD_DIM, n_rows_c),
                  per_batch(N_KV, S, HEAD_DIM), per_batch(N_KV, nq, V_AUG_ROWS, Q_BLOCK),
                  per_batch(N_KV, S, HEAD_DIM), per_batch(N_KV, nq, V_AUG_ROWS, Q_BLOCK),
                  const((N_KV, 1, cols)), const((N_KV, n_bias, cols)), const((n_sel, n_rows_c)),
                  const((n_span, SEL_TILE, LANES)), const((NSA_WIDTH, 1))],
        out_specs=pl.BlockSpec((Q_BLOCK, NSA_WIDTH), lambda b, i: (b * nq + i, 0)),
        out_shape=jax.ShapeDtypeStruct((B * S, NSA_WIDTH), F32),
        compiler_params=_cparams(("arbitrary", "arbitrary")),
        name="nsa",
    )(h_all, h_all, kc, vc.transpose(0, 1, 3, 2), ks.astype(BF16), _value_tiles(vs), kw.astype(BF16),
      _value_tiles(vw), jnp.asarray(slope_cols), jnp.asarray(bias), selmat_t, emat_t, gn.reshape(NSA_WIDTH, 1))


def _out_ln_kernel(ylru_ref, ynsa_ref, x_ref, w_ref, g_ref, b_ref, o_ref, *, alpha):
    mix = jnp.dot(ylru_ref[...].astype(BF16), w_ref[0:LRU_WIDTH, :], preferred_element_type=F32)
    mix = mix + jnp.dot(ynsa_ref[...].astype(BF16), w_ref[LRU_WIDTH:, :], preferred_element_type=F32)
    o_ref[...] = _layer_norm_rows(alpha * x_ref[...] + mix, g_ref[...], b_ref[...])


def _out_ln(ylru, ynsa, xf, w_bf, g, b, alpha, tm=512):
    T = xf.shape[0]
    const = lambda shape: pl.BlockSpec(shape, lambda i: (0,) * len(shape))
    return pl.pallas_call(
        functools.partial(_out_ln_kernel, alpha=alpha),
        grid=(T // tm,),
        in_specs=[pl.BlockSpec((tm, LRU_WIDTH), lambda i: (i, 0)),
                  pl.BlockSpec((tm, NSA_WIDTH), lambda i: (i, 0)),
                  pl.BlockSpec((tm, D_MODEL), lambda i: (i, 0)),
                  const((D_MODEL, D_MODEL)), const((1, D_MODEL)), const((1, D_MODEL))],
        out_specs=pl.BlockSpec((tm, D_MODEL), lambda i: (i, 0)),
        out_shape=jax.ShapeDtypeStruct((T, D_MODEL), F32),
        compiler_params=_cparams(("arbitrary",)),
        name="out_ln",
    )(ylru, ynsa, xf, w_bf, g.reshape(1, D_MODEL), b.reshape(1, D_MODEL))


def _topk_rows(vals, order, payload, n_keep):
    n = vals.shape[1]
    kidx = lax.broadcasted_iota(jnp.int32, (n_keep, n), 0)
    never = float(1 << 20)
    kept_v = jnp.zeros((n_keep, n), F32)
    kept_p = jnp.zeros((n_keep, n), F32)
    work = vals
    for r in range(n_keep):
        mx = jnp.max(work, axis=0, keepdims=True)
        first = jnp.min(jnp.where(work == mx, order, never), axis=0, keepdims=True)
        pick = order == first
        kept_v = jnp.where(kidx == r, mx, kept_v)
        if payload is None:
            kept_p = jnp.where(kidx == r, first, kept_p)
        else:
            kept_p = jnp.where(kidx == r, jnp.max(jnp.where(pick, payload, -1.0), axis=0, keepdims=True), kept_p)
        work = jnp.where(pick, REMOVED, work)
    return kept_v, kept_p


def _candidate_rows():
    ij = [(i, 0) for i in range(PEER_TOPK)]
    for j in range(1, SUBLANES):
        ij += [(i, j) for i in range(SUBLANES)]
    ij += [(0, j) for j in range(SUBLANES, PEER_TOPK)]
    flat = np.array([i * PEER_TOPK + j for i, j in ij], np.float32)
    valid = np.array([(i + 1) * (j + 1) <= PEER_TOPK for i, j in ij])
    return flat[:, None], np.where(valid, 0.0, REMOVED).astype(np.float32)[:, None]


def _candidates(a0, a1):
    return ([(a0, a1[0:1, :])]
            + [(a0[0:SUBLANES, :], a1[j:j + 1, :]) for j in range(1, SUBLANES)]
            + [(a0[0:1, :], a1[SUBLANES:PEER_TOPK, :])])


def _peer_route_kernel(x_ref, wq_ref, sk_ref, flat_ref, pad_ref, e_ref, g_ref):
    tm = x_ref.shape[0]
    q = jnp.dot(x_ref[...].astype(BF16), wq_ref[...], preferred_element_type=F32)
    key_idx = lax.broadcasted_iota(jnp.int32, (PEER_KEYS, tm), 0).astype(F32)
    flat = jnp.broadcast_to(flat_ref[...], (flat_ref.shape[0], tm))
    experts, gates = [], []
    for h in range(PEER_HEADS):
        sv, si = [], []
        for c in range(2):
            col = (h * 2 + c) * PEER_KEY_DIM
            s_t = _dot_nt(sk_ref[c], q[:, col:col + PEER_KEY_DIM], HI)
            v, i = _topk_rows(s_t, key_idx, None, PEER_TOPK)
            sv.append(v)
            si.append(i)
        cand = jnp.concatenate([a + b for a, b in _candidates(sv[0], sv[1])], axis=0) + pad_ref[...]
        ecand = jnp.concatenate([a * float(PEER_KEYS) + b for a, b in _candidates(si[0], si[1])], axis=0)
        cv, ce = _topk_rows(cand, flat, ecand, PEER_TOPK)
        ex = jnp.exp(cv - cv[0:1, :])
        gates.append(ex / jnp.sum(ex, axis=0, keepdims=True))
        experts.append(ce)
    e_ref[...] = jnp.concatenate(experts, axis=0).T.astype(jnp.int32)
    g_ref[...] = jnp.concatenate(gates, axis=0).T


def _peer_route(x1, wq_bf, subkeys, tm=256):
    T = x1.shape[0]
    tm = min(tm, T)
    nq = wq_bf.shape[1]
    flat, pad = _candidate_rows()
    n_cand = flat.shape[0]
    out = lambda: pl.BlockSpec((tm, N_PAIRS), lambda i: (i, 0))
    return pl.pallas_call(
        _peer_route_kernel,
        grid=(T // tm,),
        in_specs=[pl.BlockSpec((tm, D_MODEL), lambda i: (i, 0)),
                  pl.BlockSpec((D_MODEL, nq), lambda i: (0, 0)),
                  pl.BlockSpec((2, PEER_KEYS, PEER_KEY_DIM), lambda i: (0, 0, 0)),
                  pl.BlockSpec((n_cand, 1), lambda i: (0, 0)),
                  pl.BlockSpec((n_cand, 1), lambda i: (0, 0))],
        out_specs=[out(), out()],
        out_shape=[jax.ShapeDtypeStruct((T, N_PAIRS), jnp.int32),
                   jax.ShapeDtypeStruct((T, N_PAIRS), F32)],
        compiler_params=_cparams(("arbitrary",)),
        name="peer_route",
    )(x1, wq_bf, subkeys, jnp.asarray(flat), jnp.asarray(pad))


TOKENS_PER_MATMUL = 8


def _pack_table(tab):
    return tab.astype(BF16).reshape(N_EXPERTS, D_ROWS, LANES)


def _sublane_sums(ps, roll, where, sub):
    lvl = ps
    for r in (1, 2, 4):
        m = (sub % (2 * r)) < r
        half = len(lvl) // 2
        lvl = [where(m, lvl[i], lvl[i + half]) + roll(where(m, lvl[i + half], lvl[i]), r) for i in range(half)]
    return lvl[0]


def _sum_src():
    sub = np.arange(SUBLANES)[:, None] * np.ones((1, LANES), np.int64)
    ps = [np.full((SUBLANES, LANES), 10.0 ** i) for i in range(SUBLANES)]
    out = _sublane_sums(ps, lambda v, r: np.roll(v, r, axis=0), np.where, sub)
    return [int(round(np.log10(out[s, 0] / SUBLANES))) for s in range(SUBLANES)]


_SUM_SRC = _sum_src()


def _split_dot_ones(d, ones_bf):
    hi = d.astype(BF16)
    r1 = d - hi.astype(F32)
    mid = r1.astype(BF16)
    lo = (r1 - mid.astype(F32)).astype(BF16)
    dot = lambda a: jnp.dot(a, ones_bf, preferred_element_type=F32)
    return (dot(hi) + dot(mid)) + dot(lo)


def _peer_u_kernel(e_ref, x_ref, tab_ref, gate_ref, w_ref, part_ref, zt_ref, *, tb):
    sub = lax.broadcasted_iota(jnp.int32, (SUBLANES, LANES), 0)
    lane = lax.broadcasted_iota(jnp.int32, (N_PAIRS, tb), 1)
    ones_bf = jnp.ones((LANES, LANES), BF16)

    def gather_dots(t):
        xt = x_ref[t]
        for grp in range(N_PAIRS // SUBLANES):
            prods = [None] * SUBLANES
            for s in range(SUBLANES):
                prods[_SUM_SRC[s]] = tab_ref[e_ref[t * N_PAIRS + grp * SUBLANES + s]].astype(F32) * xt
            part_ref[t, grp * SUBLANES:(grp + 1) * SUBLANES, :] = _sublane_sums(
                prods, lambda v, r: pltpu.roll(v, r, axis=0), jnp.where, sub)

    def two_tokens(i, carry):
        gather_dots(2 * i)
        gather_dots(2 * i + 1)
        return carry

    lax.fori_loop(0, tb // 2, two_tokens, 0)

    zt_ref[...] = jnp.zeros_like(zt_ref)

    def lane_sums(i, carry):
        t0 = pl.multiple_of(i * TOKENS_PER_MATMUL, TOKENS_PER_MATMUL)
        part = part_ref[pl.ds(t0, TOKENS_PER_MATMUL)].reshape(TOKENS_PER_MATMUL * N_PAIRS, LANES)
        z = _split_dot_ones(part, ones_bf)
        zt = zt_ref[...]
        for j in range(TOKENS_PER_MATMUL):
            zt = jnp.where(lane == t0 + j, z[j * N_PAIRS:(j + 1) * N_PAIRS, :], zt)
        zt_ref[...] = zt
        return carry

    lax.fori_loop(0, tb // TOKENS_PER_MATMUL, lane_sums, 0)
    w_ref[...] = jax.nn.gelu(zt_ref[...].T) * gate_ref[...]


def _peer_u(experts, x3, tab, gates, tb=128):
    T = x3.shape[0]
    vm = lambda: pl.BlockSpec((tb, N_PAIRS), lambda i: (i, 0))
    return pl.pallas_call(
        functools.partial(_peer_u_kernel, tb=tb),
        grid=(T // tb,),
        in_specs=[pl.BlockSpec((tb * N_PAIRS,), lambda i: (i,), memory_space=pltpu.SMEM),
                  pl.BlockSpec((tb, D_ROWS, LANES), lambda i: (i, 0, 0)),
                  pl.BlockSpec((N_EXPERTS, D_ROWS, LANES), lambda i: (0, 0, 0), pipeline_mode=pl.Buffered(1)),
                  vm()],
        out_specs=vm(),
        out_shape=jax.ShapeDtypeStruct((T, N_PAIRS), F32),
        scratch_shapes=[pltpu.VMEM((tb, N_PAIRS, LANES), F32), pltpu.VMEM((N_PAIRS, tb), F32)],
        compiler_params=_cparams(("arbitrary",)),
        name="peer_u",
    )(experts.reshape(T * N_PAIRS), x3, tab, gates)


def _peer_v_kernel(e_ref, w_ref, x_ref, tab_ref, g_ref, b_ref, o_ref, y_ref, wrep_ref, *, tb, alpha):
    n_chain = 4
    eye = (lax.broadcasted_iota(jnp.int32, (N_PAIRS, LANES), 0)
           == lax.broadcasted_iota(jnp.int32, (N_PAIRS, LANES), 1))
    ones_bf = jnp.ones((LANES, LANES), BF16)

    def replicate(i, carry):
        t0 = pl.multiple_of(i * TOKENS_PER_MATMUL, TOKENS_PER_MATMUL)
        diag = [jnp.where(eye, jnp.broadcast_to(w_ref[pl.ds(t0 + j, 1), :], (N_PAIRS, LANES)), 0.0)
                for j in range(TOKENS_PER_MATMUL)]
        wrep = _split_dot_ones(jnp.concatenate(diag, axis=0), ones_bf)
        wrep_ref[pl.ds(t0, TOKENS_PER_MATMUL)] = wrep.reshape(TOKENS_PER_MATMUL, N_PAIRS, LANES)
        return carry

    lax.fori_loop(0, tb // TOKENS_PER_MATMUL, replicate, 0)

    def one_token(t, carry):
        zero = jnp.zeros((D_ROWS, LANES), F32)
        accs = [zero] * n_chain
        for k in range(N_PAIRS):
            accs[k % n_chain] = accs[k % n_chain] + (tab_ref[e_ref[t * N_PAIRS + k]].astype(F32)
                                                     * wrep_ref[t, k:k + 1, :])
        y_ref[t] = (accs[0] + accs[1]) + (accs[2] + accs[3])
        return carry

    lax.fori_loop(0, tb, one_token, 0)
    z = alpha * x_ref[...] + y_ref[...]
    inv_d = 1.0 / D_MODEL
    mu = jnp.sum(jnp.sum(z, axis=2, keepdims=True), axis=1, keepdims=True) * inv_d
    zc = z - mu
    var = jnp.sum(jnp.sum(zc * zc, axis=2, keepdims=True), axis=1, keepdims=True) * inv_d
    o_ref[...] = zc * lax.rsqrt(var + LN_EPS) * g_ref[...] + b_ref[...]


def _peer_v(experts, wgt, x3, tab, g, b, alpha, tb=128):
    T = x3.shape[0]
    return pl.pallas_call(
        functools.partial(_peer_v_kernel, tb=tb, alpha=alpha),
        grid=(T // tb,),
        in_specs=[pl.BlockSpec((tb * N_PAIRS,), lambda i: (i,), memory_space=pltpu.SMEM),
                  pl.BlockSpec((tb, N_PAIRS), lambda i: (i, 0)),
                  pl.BlockSpec((tb, D_ROWS, LANES), lambda i: (i, 0, 0)),
                  pl.BlockSpec((N_EXPERTS, D_ROWS, LANES), lambda i: (0, 0, 0), pipeline_mode=pl.Buffered(1)),
                  pl.BlockSpec((1, D_ROWS, LANES), lambda i: (0, 0, 0)),
                  pl.BlockSpec((1, D_ROWS, LANES), lambda i: (0, 0, 0))],
        out_specs=pl.BlockSpec((tb, D_ROWS, LANES), lambda i: (i, 0, 0)),
        out_shape=jax.ShapeDtypeStruct((T, D_ROWS, LANES), F32),
        scratch_shapes=[pltpu.VMEM((tb, D_ROWS, LANES), F32), pltpu.VMEM((tb, N_PAIRS, LANES), F32)],
        compiler_params=_cparams(("arbitrary",)),
        name="peer_v",
    )(experts.reshape(T * N_PAIRS), wgt, x3, tab, g.reshape(1, D_ROWS, LANES), b.reshape(1, D_ROWS, LANES))


def kernel(x, w_in, b_in, conv_w, conv_b, lru_wa, lru_ba, lru_wx, lru_bx, lru_lambda, cmp_pos_k, cmpk_w1, cmpk_b1, cmpk_w2, cmpk_b2, cmp_pos_v, cmpv_w1, cmpv_b1, cmpv_w2, cmpv_b2, gn_lru_g, gn_nsa_g, w_out, ln1_g, ln1_b, peer_wq, peer_subkeys, peer_u, peer_v, ln2_g, ln2_b):
    B, S, D = x.shape
    T = B * S
    depth = w_in.shape[0]
    alpha = (2 * depth) ** 0.25
    fan = CMP_STRIDE * HEAD_DIM
    xf = x.reshape(T, D)
    for l in range(depth):
        w_in_bf = jnp.pad(w_in[l], ((0, 0), (0, IN_COLS_PAD - IN_COLS))).astype(BF16)
        b_in_row = jnp.pad(b_in[l], (0, IN_COLS_PAD - IN_COLS)).reshape(1, IN_COLS_PAD)
        h_all = _in_proj(xf, w_in_bf, b_in_row)

        y_lru = _rglru(h_all, B, S, conv_w[l], conv_b[l], _block_diag(lru_wa[l]), lru_ba[l],
                       _block_diag(lru_wx[l]), lru_bx[l], lru_lambda[l], gn_lru_g[l])

        kv = h_all[:, COL_KV:COL_KV + 6 * KV_WIDTH].reshape(B, S, 6, N_KV, HEAD_DIM).transpose(2, 0, 3, 1, 4)
        kr = kv[0:2].reshape(2, B * N_KV, S // CMP_STRIDE, fan)
        w1 = jnp.stack([cmpk_w1[l], cmpv_w1[l]])
        w1ab = jnp.concatenate([w1[:, :fan], w1[:, fan:]], axis=2)
        pos = jnp.stack([cmp_pos_k[l], cmp_pos_v[l]]).reshape(2, 1, CMP_BLOCK * HEAD_DIM)
        posflat = jnp.broadcast_to(pos, (2, SUBLANES, CMP_BLOCK * HEAD_DIM))
        kcvc = _compress(kr, w1ab, w1, posflat,
                         jnp.stack([cmpk_b1[l], cmpv_b1[l]]).reshape(2, 1, HEAD_DIM),
                         jnp.stack([cmpk_w2[l], cmpv_w2[l]]),
                         jnp.stack([cmpk_b2[l], cmpv_b2[l]]).reshape(2, 1, HEAD_DIM))
        kcvc = kcvc.reshape(2, B, N_KV, S // CMP_STRIDE, HEAD_DIM)
        y_nsa = _nsa(h_all, B, S, kcvc[0], kcvc[1], kv[2], kv[3], kv[4], kv[5], gn_nsa_g[l])

        x1 = _out_ln(y_lru, y_nsa, xf, w_out[l].astype(BF16), ln1_g[l], ln1_b[l], alpha)

        experts, gates = _peer_route(x1, peer_wq[l].astype(BF16), peer_subkeys[l])
        x3 = x1.reshape(T, D_ROWS, LANES)
        wgt = _peer_u(experts, x3, _pack_table(peer_u[l]), gates)
        xf = _peer_v(experts, wgt, x3, _pack_table(peer_v[l]), ln2_g[l], ln2_b[l], alpha).reshape(T, D)
    return xf.reshape(B, S, D)
```

```python
import functools

import numpy as np
import jax
import jax.numpy as jnp
from jax import lax
from jax.experimental import pallas as pl
from jax.experimental.pallas import tpu as pltpu

D_MODEL = 1024
LRU_WIDTH = 512
LRU_BLOCKS = 8
LRU_BLOCK_DIM = LRU_WIDTH // LRU_BLOCKS
CONV_WIDTH = 4
LRU_C = 8.0
N_HEADS = 8
HEAD_DIM = 64
N_KV = 2
Q_PER_KV = N_HEADS // N_KV
NSA_WIDTH = N_HEADS * HEAD_DIM
KV_WIDTH = N_KV * HEAD_DIM
N_BRANCHES = 3
CMP_BLOCK = 32
CMP_STRIDE = 16
SEL_BLOCK = 64
SEL_TOPN = 16
WINDOW = 512
Q_BLOCK = 128
FORCE_BONUS = 1e4
NEG = -1e30
REMOVED = -3.0e38
PEER_HEADS = 8
PEER_KEYS = 128
PEER_TOPK = 16
PEER_KEY_DIM = 128
N_EXPERTS = PEER_KEYS * PEER_KEYS
N_PAIRS = PEER_HEADS * PEER_TOPK
LN_EPS = 1e-5
IN_SPLITS = (LRU_WIDTH, LRU_WIDTH, NSA_WIDTH) + (KV_WIDTH,) * 6 + (N_HEADS * N_BRANCHES,)
IN_COLS = sum(IN_SPLITS)

LANES = 128
SUBLANES = 8
VMEM_LIMIT_BYTES = 56 * 1024 * 1024

IN_COLS_PAD = -(-IN_COLS // LANES) * LANES
COL_LRU_X = 0
COL_LRU_GATE = LRU_WIDTH
COL_Q = 2 * LRU_WIDTH
COL_KV = COL_Q + NSA_WIDTH
COL_GATES = COL_KV + 6 * KV_WIDTH

HI = lax.Precision.HIGHEST
F32 = jnp.float32
BF16 = jnp.bfloat16

D_ROWS = D_MODEL // LANES


def _cparams(sem):
    return pltpu.CompilerParams(dimension_semantics=sem, vmem_limit_bytes=VMEM_LIMIT_BYTES)


def _dot_nt(a, b, precision=None):
    return lax.dot_general(a, b, (((1,), (1,)), ((), ())), precision=precision,
                           preferred_element_type=F32)


def _split2(a):
    hi = a.astype(BF16)
    return hi, (a - hi.astype(F32)).astype(BF16)


def _layer_norm_rows(z, g, b):
    mu = jnp.mean(z, axis=-1, keepdims=True)
    zc = z - mu
    var = jnp.mean(zc * zc, axis=-1, keepdims=True)
    return zc * lax.rsqrt(var + LN_EPS) * g + b


def _in_proj_kernel(x_ref, w_ref, b_ref, o_ref):
    o_ref[...] = jnp.dot(x_ref[...].astype(BF16), w_ref[...], preferred_element_type=F32) + b_ref[...]


def _in_proj(xf, w_bf, b_row, tm=512):
    T = xf.shape[0]
    return pl.pallas_call(
        _in_proj_kernel,
        grid=(T // tm,),
        in_specs=[pl.BlockSpec((tm, D_MODEL), lambda i: (i, 0)),
                  pl.BlockSpec((D_MODEL, IN_COLS_PAD), lambda i: (0, 0)),
                  pl.BlockSpec((1, IN_COLS_PAD), lambda i: (0, 0))],
        out_specs=pl.BlockSpec((tm, IN_COLS_PAD), lambda i: (i, 0)),
        out_shape=jax.ShapeDtypeStruct((T, IN_COLS_PAD), F32),
        compiler_params=_cparams(("arbitrary",)),
        name="in_proj",
    )(xf, w_bf, b_row)


def _rglru_kernel(x_ref, gate_ref, cw_ref, cb_ref, wa_ref, ba_ref, wx_ref, bx_ref, lam_ref, gn_ref,
                  o_ref, xprev_ref, h_ref, *, ts):
    i = pl.program_id(1)

    @pl.when(i == 0)
    def _():
        xprev_ref[...] = jnp.zeros_like(xprev_ref)
        h_ref[...] = jnp.zeros_like(h_ref)

    xb = x_ref[...]
    xp = xprev_ref[...]
    row = lax.broadcasted_iota(jnp.int32, xb.shape, 0)
    xc = cw_ref[CONV_WIDTH - 1:CONV_WIDTH, :] * xb + cb_ref[...]
    for j in range(1, CONV_WIDTH):
        shifted = jnp.where(row >= j, pltpu.roll(xb, j, axis=0), pltpu.roll(xp, j, axis=0))
        xc = xc + cw_ref[CONV_WIDTH - 1 - j:CONV_WIDTH - j, :] * shifted
    xprev_ref[...] = xb

    r = jax.nn.sigmoid(jnp.dot(xc, wa_ref[...], precision=HI, preferred_element_type=F32) + ba_ref[...])
    ig = jax.nn.sigmoid(jnp.dot(xc, wx_ref[...], precision=HI, preferred_element_type=F32) + bx_ref[...])
    lam = lam_ref[...]
    softplus_neg_lam = jnp.maximum(-lam, 0.0) + jnp.log1p(jnp.exp(-jnp.abs(lam)))
    log_a = -LRU_C * r * softplus_neg_lam
    a = jnp.exp(log_a)
    th = jnp.tanh(log_a)
    u = jnp.sqrt(-2.0 * th / (1.0 - th)) * (ig * xc)

    d = 1
    while d < ts:
        keep = row >= d
        a_sh = pltpu.roll(a, d, axis=0)
        u_sh = pltpu.roll(u, d, axis=0)
        u = jnp.where(keep, a * u_sh + u, u)
        a = jnp.where(keep, a * a_sh, a)
        d *= 2
    h = u + a * h_ref[0:1, :]
    h_ref[0:1, :] = h[ts - 1:ts, :]

    y = h * jax.nn.gelu(gate_ref[...])
    y = y * lax.rsqrt(jnp.mean(y * y, axis=-1, keepdims=True) + LN_EPS) * gn_ref[...]
    o_ref[...] = y


def _rglru(h_all, B, S, cw, cb, wa_bd, ba, wx_bd, bx, lam, gn, ts=512):
    ts = min(ts, S)
    nt = S // ts
    C = LRU_WIDTH
    row = lambda v: v.reshape(1, C)
    const = lambda shape: pl.BlockSpec(shape, lambda b, i: (0,) * len(shape))
    return pl.pallas_call(
        functools.partial(_rglru_kernel, ts=ts),
        grid=(B, nt),
        in_specs=[pl.BlockSpec((ts, C), lambda b, i: (b * nt + i, COL_LRU_X // C)),
                  pl.BlockSpec((ts, C), lambda b, i: (b * nt + i, COL_LRU_GATE // C)),
                  const((CONV_WIDTH, C)), const((1, C)), const((C, C)), const((1, C)),
                  const((C, C)), const((1, C)), const((1, C)), const((1, C))],
        out_specs=pl.BlockSpec((ts, C), lambda b, i: (b * nt + i, 0)),
        out_shape=jax.ShapeDtypeStruct((B * S, C), F32),
        scratch_shapes=[pltpu.VMEM((ts, C), F32), pltpu.VMEM((SUBLANES, C), F32)],
        compiler_params=_cparams(("arbitrary", "arbitrary")),
        name="rglru",
    )(h_all, h_all, cw, row(cb), wa_bd, row(ba), wx_bd, row(bx), row(lam), row(gn))


def _block_diag(w):
    n, d, e = w.shape
    eye = jnp.eye(n, dtype=w.dtype)
    return (w[:, :, None, :] * eye[:, None, :, None]).reshape(n * d, n * e)


def _compress_kernel(kr_ref, w1ab_ref, w1_ref, pos_ref, b1_ref, w2_ref, b2_ref, o_ref):
    kr = kr_ref[0, 0]
    nr = kr.shape[0]
    ab = jnp.dot(kr, w1ab_ref[0], precision=HI, preferred_element_type=F32)
    ab_next = pltpu.roll(ab, nr - 1, axis=0)
    pre = ab[:, :HEAD_DIM] + ab_next[:, HEAD_DIM:]
    posc = jnp.dot(pos_ref[0], w1_ref[0], precision=HI, preferred_element_type=F32)[0:1, :]
    h1 = jax.nn.gelu(pre + posc + b1_ref[0])
    o_ref[0, 0] = jnp.dot(h1, w2_ref[0], precision=HI, preferred_element_type=F32) + b2_ref[0]


def _compress(kr, w1ab, w1, posflat, b1, w2, b2):
    _, BG, nr, fan = kr.shape
    wspec = lambda shape: pl.BlockSpec((1,) + shape, lambda s, i: (s,) + (0,) * len(shape))
    return pl.pallas_call(
        _compress_kernel,
        grid=(2, BG),
        in_specs=[pl.BlockSpec((1, 1, nr, fan), lambda s, i: (s, i, 0, 0)),
                  wspec((fan, 2 * HEAD_DIM)), wspec((2 * fan, HEAD_DIM)), wspec((SUBLANES, 2 * fan)),
                  wspec((1, HEAD_DIM)), wspec((HEAD_DIM, HEAD_DIM)), wspec((1, HEAD_DIM))],
        out_specs=pl.BlockSpec((1, 1, nr, HEAD_DIM), lambda s, i: (s, i, 0, 0)),
        out_shape=jax.ShapeDtypeStruct((2, BG, nr, HEAD_DIM), F32),
        compiler_params=_cparams(("arbitrary", "arbitrary")),
        name="compress",
    )(kr, w1ab, w1, posflat, b1, w2, b2)


def _selection_matrix(n_rows, n_cmp, n_sel):
    cs = np.arange(n_rows)[:, None] * CMP_STRIDE
    ss = np.arange(n_sel)[None, :] * SEL_BLOCK
    ov = np.clip(np.minimum(cs + CMP_BLOCK, ss + SEL_BLOCK) - np.maximum(cs, ss), 0, None)
    ov = (ov / CMP_STRIDE).astype(np.float32)
    ov[n_cmp:] = 0.0
    return ov


V_AUG_ROWS = HEAD_DIM + 16
SEL_TILE = 4 * Q_BLOCK
WIN_SPAN = WINDOW + Q_BLOCK


def _nsa_kernel(q_ref, gt_ref, kc_ref, vct_ref, ks_ref, vst_ref, kw_ref, vwt_ref, slope_ref, bias_ref,
                selmat_ref, emat_ref, gn_ref, o_ref, *, n_cmp, n_top, win_span):
    qt = pl.program_id(1)
    q0 = qt * Q_BLOCK
    R = Q_PER_KV
    cols = R * Q_BLOCK
    n_rows_c = kc_ref.shape[2]
    n_sel = selmat_ref.shape[0]
    q_loc = lax.broadcasted_iota(jnp.int32, (1, cols), 1) % Q_BLOCK
    t_col = q0 + q_loc
    gates_t = jax.nn.sigmoid(gt_ref[...]).T

    def step(carry, qs_bf, slope, bias, k_bf, v_tiles, offset, ok):
        m, acc = carry
        s = jnp.where(ok, _dot_nt(k_bf, qs_bf) + bias, NEG)
        c = slope * (-offset).astype(F32)
        m_new = jnp.maximum(m, jnp.max(s, axis=0, keepdims=True) + c)
        p = jnp.exp(s + (c - m_new)).astype(BF16)
        acc = jnp.exp(m - m_new) * acc
        for i, v in enumerate(v_tiles):
            acc = acc + jnp.dot(v, p[i * Q_BLOCK:(i + 1) * Q_BLOCK, :], preferred_element_type=F32)
        return m_new, acc

    init = (jnp.full((1, cols), NEG, F32), jnp.zeros((V_AUG_ROWS, cols), F32))
    finish = lambda acc: acc[0:HEAD_DIM, :] / acc[HEAD_DIM:HEAD_DIM + 1, :]
    y_rows = []
    for g in range(N_KV):
        qg = q_ref[:, g * R * HEAD_DIM:(g + 1) * R * HEAD_DIM] * (HEAD_DIM ** -0.5)
        qs = jnp.concatenate([qg[:, r * HEAD_DIM:(r + 1) * HEAD_DIM] for r in range(R)], axis=0)
        qs_bf = qs.astype(BF16)
        slope = slope_ref[g]

        kc = kc_ref[0, g]
        c_idx = lax.broadcasted_iota(jnp.int32, (n_rows_c, 1), 0)
        cmp_end = c_idx * CMP_STRIDE + (CMP_BLOCK - 1)
        valid_c = (cmp_end <= t_col) & (c_idx < n_cmp)
        q_hi, q_lo = _split2(qs)
        k_hi, k_lo = _split2(kc)
        s = _dot_nt(k_hi, q_hi) + (_dot_nt(k_hi, q_lo) + _dot_nt(k_lo, q_hi))
        s = s - slope * jnp.abs(t_col - cmp_end).astype(F32)
        s = jnp.where(valid_c, s, NEG)
        e = jnp.exp(s - jnp.max(s, axis=0, keepdims=True))
        p_c = e / jnp.sum(e, axis=0, keepdims=True)
        p_c = p_c * (t_col >= CMP_BLOCK - 1).astype(F32)
        o_c = jnp.dot(vct_ref[0, g].astype(BF16), p_c.astype(BF16), preferred_element_type=F32)

        p_sum = p_c[:, 0:Q_BLOCK]
        for r in range(1, R):
            p_sum = p_sum + p_c[:, r * Q_BLOCK:(r + 1) * Q_BLOCK]
        p_hi, p_lo = _split2(p_sum)
        dot = lambda a, b: jnp.dot(a, b, preferred_element_type=F32)
        imp_t = dot(selmat_ref[...], p_hi) + dot(selmat_ref[...], p_lo)
        j_idx = lax.broadcasted_iota(jnp.int32, (n_sel, Q_BLOCK), 0)
        j_f = j_idx.astype(F32)
        cur = t_col[:, 0:Q_BLOCK] // SEL_BLOCK
        forced = (j_idx == 0) | (j_idx == cur) | (j_idx == cur - 1)
        work = jnp.where(j_idx <= cur, imp_t + jnp.where(forced, FORCE_BONUS, 0.0), NEG)
        msel_t = jnp.zeros((n_sel, Q_BLOCK), F32)
        for _ in range(n_top):
            mx = jnp.max(work, axis=0, keepdims=True)
            first = jnp.min(jnp.where(work == mx, j_f, float(LANES)), axis=0, keepdims=True)
            pick = j_f == first
            msel_t = jnp.where(pick, 1.0, msel_t)
            work = jnp.where(pick, REMOVED, work)
        if n_sel < LANES:
            msel_t = jnp.concatenate([msel_t, jnp.zeros((LANES - n_sel, Q_BLOCK), F32)], axis=0)
        msel_bf = msel_t.astype(BF16)

        sel_key = lax.broadcasted_iota(jnp.int32, (SEL_TILE, cols), 0)
        bias_sel = bias_ref[g, 0:SEL_TILE, :]

        def sel_span(kt, carry, causal):
            k0 = pl.multiple_of(kt * SEL_TILE, SEL_TILE)
            ch = jnp.dot(emat_ref[kt], msel_bf, preferred_element_type=F32)
            ok = jnp.concatenate([ch] * R, axis=1) > 0.5
            if causal:
                ok = ok & (sel_key <= q_loc + (q0 - k0))
            v_tiles = [vst_ref[0, g, kt * (SEL_TILE // Q_BLOCK) + i] for i in range(SEL_TILE // Q_BLOCK)]
            return step(carry, qs_bf, slope, bias_sel, ks_ref[0, g, pl.ds(k0, SEL_TILE), :], v_tiles, q0 - k0, ok)

        kt_diag = q0 // SEL_TILE
        carry = sel_span(kt_diag, init, True)
        carry = lax.fori_loop(0, kt_diag, lambda kt, c: sel_span(kt, c, False), carry)
        o_s = finish(carry[1])

        w0 = pl.multiple_of(jnp.maximum(q0 + Q_BLOCK - win_span, 0), Q_BLOCK)
        win_key = lax.broadcasted_iota(jnp.int32, (win_span, cols), 0)
        reach = q_loc + (q0 - w0)
        ok_w = (win_key <= reach) & (win_key > reach - WINDOW)
        v_tiles = [vwt_ref[0, g, w0 // Q_BLOCK + i] for i in range(win_span // Q_BLOCK)]
        carry = step(init, qs_bf, slope, bias_ref[g, 0:win_span, :], kw_ref[0, g, pl.ds(w0, win_span), :],
                     v_tiles, q0 - w0, ok_w)
        o_w = finish(carry[1])

        for r in range(R):
            col = (g * R + r) * N_BRANCHES
            sl = slice(r * Q_BLOCK, (r + 1) * Q_BLOCK)
            y_rows.append(gates_t[col:col + 1, :] * o_c[:, sl] + gates_t[col + 1:col + 2, :] * o_s[:, sl]
                          + gates_t[col + 2:col + 3, :] * o_w[:, sl])
    y_t = jnp.concatenate(y_rows, axis=0)
    y_t = y_t * lax.rsqrt(jnp.mean(y_t * y_t, axis=0, keepdims=True) + LN_EPS) * gn_ref[...]
    o_ref[...] = y_t.T


def _value_tiles(v):
    B, G, S, hd = v.shape
    vt = v.transpose(0, 1, 3, 2)
    pad = jnp.zeros((B, G, V_AUG_ROWS - hd, S), v.dtype).at[:, :, 0, :].set(1.0)
    vt = jnp.concatenate([vt, pad], axis=2).reshape(B, G, V_AUG_ROWS, S // Q_BLOCK, Q_BLOCK)
    return vt.transpose(0, 1, 3, 2, 4).astype(BF16)


def _nsa(h_all, B, S, kc, vc, ks, vs, kw, vw, gn):
    nq = S // Q_BLOCK
    n_cmp = (S - CMP_BLOCK) // CMP_STRIDE + 1
    n_rows_c = kc.shape[2]
    n_sel = S // SEL_BLOCK
    n_top = min(SEL_TOPN, n_sel)
    n_span = S // SEL_TILE
    win_span = min(WIN_SPAN, S)
    n_bias = max(win_span, SEL_TILE)
    cols = Q_PER_KV * Q_BLOCK
    slopes = np.array([2.0 ** (-8.0 * (h + 1) / N_HEADS) for h in range(N_HEADS)], np.float32)
    slope_cols = np.repeat(slopes.reshape(N_KV, Q_PER_KV), Q_BLOCK, axis=1)[:, None, :]
    rel = (np.arange(cols)[None, :] % Q_BLOCK - np.arange(n_bias)[:, None]).astype(np.float32)
    bias = -slope_cols * rel[None]
    selmat_t = jnp.asarray(_selection_matrix(n_rows_c, n_cmp, n_sel).T, dtype=BF16)
    key_blk = (np.arange(S) // SEL_BLOCK).reshape(n_span, SEL_TILE, 1)
    emat_t = jnp.asarray((key_blk == np.arange(LANES)[None, None, :]).astype(np.float32), dtype=BF16)
    qw = Q_PER_KV * HEAD_DIM * N_KV
    per_batch = lambda *dims: pl.BlockSpec((1,) + dims, lambda b, i: (b,) + (0,) * len(dims))
    const = lambda shape: pl.BlockSpec(shape, lambda b, i: (0,) * len(shape))
    return pl.pallas_call(
        functools.partial(_nsa_kernel, n_cmp=n_cmp, n_top=n_top, win_span=win_span),
        grid=(B, nq),
        in_specs=[pl.BlockSpec((Q_BLOCK, qw), lambda b, i: (b * nq + i, COL_Q // qw)),
                  pl.BlockSpec((Q_BLOCK, LANES), lambda b, i: (b * nq + i, COL_GATES // LANES)),
                  per_batch(N_KV, n_rows_c, HEAD_DIM), per_batch(N_KV, HEAD_DIM, n_rows_c),
                  per_batch(N_KV, S, HEAD_DIM), per_batch(N_KV, nq, V_AUG_ROWS, Q_BLOCK),
                  per_batch(N_KV, S, HEAD_DIM), per_batch(N_KV, nq, V_AUG_ROWS, Q_BLOCK),
                  const((N_KV, 1, cols)), const((N_KV, n_bias, cols)), const((n_sel, n_rows_c)),
                  const((n_span, SEL_TILE, LANES)), const((NSA_WIDTH, 1))],
        out_specs=pl.BlockSpec((Q_BLOCK, NSA_WIDTH), lambda b, i: (b * nq + i, 0)),
        out_shape=jax.ShapeDtypeStruct((B * S, NSA_WIDTH), F32),
        compiler_params=_cparams(("arbitrary", "arbitrary")),
        name="nsa",
    )(h_all, h_all, kc, vc.transpose(0, 1, 3, 2), ks.astype(BF16), _value_tiles(vs), kw.astype(BF16),
      _value_tiles(vw), jnp.asarray(slope_cols), jnp.asarray(bias), selmat_t, emat_t, gn.reshape(NSA_WIDTH, 1))


def _out_ln_kernel(ylru_ref, ynsa_ref, x_ref, w_ref, g_ref, b_ref, o_ref, *, alpha):
    mix = jnp.dot(ylru_ref[...].astype(BF16), w_ref[0:LRU_WIDTH, :], preferred_element_type=F32)
    mix = mix + jnp.dot(ynsa_ref[...].astype(BF16), w_ref[LRU_WIDTH:, :], preferred_element_type=F32)
    o_ref[...] = _layer_norm_rows(alpha * x_ref[...] + mix, g_ref[...], b_ref[...])


def _out_ln(ylru, ynsa, xf, w_bf, g, b, alpha, tm=512):
    T = xf.shape[0]
    const = lambda shape: pl.BlockSpec(shape, lambda i: (0,) * len(shape))
    return pl.pallas_call(
        functools.partial(_out_ln_kernel, alpha=alpha),
        grid=(T // tm,),
        in_specs=[pl.BlockSpec((tm, LRU_WIDTH), lambda i: (i, 0)),
                  pl.BlockSpec((tm, NSA_WIDTH), lambda i: (i, 0)),
                  pl.BlockSpec((tm, D_MODEL), lambda i: (i, 0)),
                  const((D_MODEL, D_MODEL)), const((1, D_MODEL)), const((1, D_MODEL))],
        out_specs=pl.BlockSpec((tm, D_MODEL), lambda i: (i, 0)),
        out_shape=jax.ShapeDtypeStruct((T, D_MODEL), F32),
        compiler_params=_cparams(("arbitrary",)),
        name="out_ln",
    )(ylru, ynsa, xf, w_bf, g.reshape(1, D_MODEL), b.reshape(1, D_MODEL))


def _topk_rows(vals, order, payload, n_keep):
    n = vals.shape[1]
    kidx = lax.broadcasted_iota(jnp.int32, (n_keep, n), 0)
    never = float(1 << 20)
    kept_v = jnp.zeros((n_keep, n), F32)
    kept_p = jnp.zeros((n_keep, n), F32)
    work = vals
    for r in range(n_keep):
        mx = jnp.max(work, axis=0, keepdims=True)
        first = jnp.min(jnp.where(work == mx, order, never), axis=0, keepdims=True)
        pick = order == first
        kept_v = jnp.where(kidx == r, mx, kept_v)
        if payload is None:
            kept_p = jnp.where(kidx == r, first, kept_p)
        else:
            kept_p = jnp.where(kidx == r, jnp.max(jnp.where(pick, payload, -1.0), axis=0, keepdims=True), kept_p)
        work = jnp.where(pick, REMOVED, work)
    return kept_v, kept_p


def _candidate_rows():
    ij = [(i, 0) for i in range(PEER_TOPK)]
    for j in range(1, SUBLANES):
        ij += [(i, j) for i in range(SUBLANES)]
    ij += [(0, j) for j in range(SUBLANES, PEER_TOPK)]
    flat = np.array([i * PEER_TOPK + j for i, j in ij], np.float32)
    valid = np.array([(i + 1) * (j + 1) <= PEER_TOPK for i, j in ij])
    return flat[:, None], np.where(valid, 0.0, REMOVED).astype(np.float32)[:, None]


def _candidates(a0, a1):
    return ([(a0, a1[0:1, :])]
            + [(a0[0:SUBLANES, :], a1[j:j + 1, :]) for j in range(1, SUBLANES)]
            + [(a0[0:1, :], a1[SUBLANES:PEER_TOPK, :])])


def _peer_route_kernel(x_ref, wq_ref, sk_ref, flat_ref, pad_ref, e_ref, g_ref):
    tm = x_ref.shape[0]
    q = jnp.dot(x_ref[...].astype(BF16), wq_ref[...], preferred_element_type=F32)
    key_idx = lax.broadcasted_iota(jnp.int32, (PEER_KEYS, tm), 0).astype(F32)
    flat = jnp.broadcast_to(flat_ref[...], (flat_ref.shape[0], tm))
    experts, gates = [], []
    for h in range(PEER_HEADS):
        sv, si = [], []
        for c in range(2):
            col = (h * 2 + c) * PEER_KEY_DIM
            s_t = _dot_nt(sk_ref[c], q[:, col:col + PEER_KEY_DIM], HI)
            v, i = _topk_rows(s_t, key_idx, None, PEER_TOPK)
            sv.append(v)
            si.append(i)
        cand = jnp.concatenate([a + b for a, b in _candidates(sv[0], sv[1])], axis=0) + pad_ref[...]
        ecand = jnp.concatenate([a * float(PEER_KEYS) + b for a, b in _candidates(si[0], si[1])], axis=0)
        cv, ce = _topk_rows(cand, flat, ecand, PEER_TOPK)
        ex = jnp.exp(cv - cv[0:1, :])
        gates.append(ex / jnp.sum(ex, axis=0, keepdims=True))
        experts.append(ce)
    e_ref[...] = jnp.concatenate(experts, axis=0).T.astype(jnp.int32)
    g_ref[...] = jnp.concatenate(gates, axis=0).T


def _peer_route(x1, wq_bf, subkeys, tm=256):
    T = x1.shape[0]
    tm = min(tm, T)
    nq = wq_bf.shape[1]
    flat, pad = _candidate_rows()
    n_cand = flat.shape[0]
    out = lambda: pl.BlockSpec((tm, N_PAIRS), lambda i: (i, 0))
    return pl.pallas_call(
        _peer_route_kernel,
        grid=(T // tm,),
        in_specs=[pl.BlockSpec((tm, D_MODEL), lambda i: (i, 0)),
                  pl.BlockSpec((D_MODEL, nq), lambda i: (0, 0)),
                  pl.BlockSpec((2, PEER_KEYS, PEER_KEY_DIM), lambda i: (0, 0, 0)),
                  pl.BlockSpec((n_cand, 1), lambda i: (0, 0)),
                  pl.BlockSpec((n_cand, 1), lambda i: (0, 0))],
        out_specs=[out(), out()],
        out_shape=[jax.ShapeDtypeStruct((T, N_PAIRS), jnp.int32),
                   jax.ShapeDtypeStruct((T, N_PAIRS), F32)],
        compiler_params=_cparams(("arbitrary",)),
        name="peer_route",
    )(x1, wq_bf, subkeys, jnp.asarray(flat), jnp.asarray(pad))


TOKENS_PER_MATMUL = 8


def _pack_table(tab):
    return tab.astype(BF16).reshape(N_EXPERTS, D_ROWS, LANES)


def _sublane_sums(ps, roll, where, sub):
    lvl = ps
    for r in (1, 2, 4):
        m = (sub % (2 * r)) < r
        half = len(lvl) // 2
        lvl = [where(m, lvl[i], lvl[i + half]) + roll(where(m, lvl[i + half], lvl[i]), r) for i in range(half)]
    return lvl[0]


def _sum_src():
    sub = np.arange(SUBLANES)[:, None] * np.ones((1, LANES), np.int64)
    ps = [np.full((SUBLANES, LANES), 10.0 ** i) for i in range(SUBLANES)]
    out = _sublane_sums(ps, lambda v, r: np.roll(v, r, axis=0), np.where, sub)
    return [int(round(np.log10(out[s, 0] / SUBLANES))) for s in range(SUBLANES)]


_SUM_SRC = _sum_src()


def _peer_u_kernel(e_ref, x_ref, tab_ref, gate_ref, w_ref, part_ref, zt_ref, *, tb):
    sub = lax.broadcasted_iota(jnp.int32, (SUBLANES, LANES), 0)
    lane = lax.broadcasted_iota(jnp.int32, (N_PAIRS, tb), 1)

    def gather_dots(t, slot):
        xt = x_ref[t]
        for grp in range(N_PAIRS // SUBLANES):
            prods = [None] * SUBLANES
            for s in range(SUBLANES):
                prods[_SUM_SRC[s]] = tab_ref[e_ref[t * N_PAIRS + grp * SUBLANES + s]].astype(F32) * xt
            part_ref[slot, grp * SUBLANES:(grp + 1) * SUBLANES, :] = _sublane_sums(
                prods, lambda v, r: pltpu.roll(v, r, axis=0), jnp.where, sub)

    def lane_sums(t0):
        z0 = jnp.sum(part_ref[0], axis=1, keepdims=True)
        z1 = jnp.sum(part_ref[1], axis=1, keepdims=True)
        zt_ref[...] = jnp.where(lane == t0, z0, jnp.where(lane == t0 + 1, z1, zt_ref[...]))

    part_ref[...] = jnp.zeros_like(part_ref)
    zt_ref[...] = jnp.zeros_like(zt_ref)

    def two_tokens(i, carry):
        lane_sums(2 * i - 2)
        gather_dots(2 * i, 0)
        gather_dots(2 * i + 1, 1)
        return carry

    lax.fori_loop(0, tb // 2, two_tokens, 0)
    lane_sums(tb - 2)
    w_ref[...] = jax.nn.gelu(zt_ref[...].T) * gate_ref[...]


def _peer_u(experts, x3, tab, gates, tb=128):
    T = x3.shape[0]
    vm = lambda: pl.BlockSpec((tb, N_PAIRS), lambda i: (i, 0))
    return pl.pallas_call(
        functools.partial(_peer_u_kernel, tb=tb),
        grid=(T // tb,),
        in_specs=[pl.BlockSpec((tb * N_PAIRS,), lambda i: (i,), memory_space=pltpu.SMEM),
                  pl.BlockSpec((tb, D_ROWS, LANES), lambda i: (i, 0, 0)),
                  pl.BlockSpec((N_EXPERTS, D_ROWS, LANES), lambda i: (0, 0, 0), pipeline_mode=pl.Buffered(1)),
                  vm()],
        out_specs=vm(),
        out_shape=jax.ShapeDtypeStruct((T, N_PAIRS), F32),
        scratch_shapes=[pltpu.VMEM((2, N_PAIRS, LANES), F32), pltpu.VMEM((N_PAIRS, tb), F32)],
        compiler_params=_cparams(("arbitrary",)),
        name="peer_u",
    )(experts.reshape(T * N_PAIRS), x3, tab, gates)


def _expand_matrix():
    k = np.arange(N_PAIRS)[:, None]
    c = np.arange(N_PAIRS * D_ROWS)[None, :]
    return (c // D_ROWS == k).astype(np.float32)


def _peer_v_kernel(e_ref, w_ref, x_ref, tab_ref, expand_ref, g_ref, b_ref, o_ref, y_ref, wide_ref, *, tb, alpha):
    n_col = N_PAIRS * D_ROWS
    diag = (lax.broadcasted_iota(jnp.int32, (D_ROWS, n_col), 1) % D_ROWS
            == lax.broadcasted_iota(jnp.int32, (D_ROWS, n_col), 0))
    part_row = lax.broadcasted_iota(jnp.int32, (SUBLANES, LANES), 0)

    def expand_tokens(i, carry):
        t0 = pl.multiple_of(i * TOKENS_PER_MATMUL, TOKENS_PER_MATMUL)
        parts = []
        for j in range(TOKENS_PER_MATMUL):
            wrow = w_ref[pl.ds(t0 + j, 1), :]
            hi = wrow.astype(BF16).astype(F32)
            mid = (wrow - hi).astype(BF16).astype(F32)
            lo = ((wrow - hi) - mid).astype(BF16).astype(F32)
            bc = lambda v: jnp.broadcast_to(v, (SUBLANES, LANES))
            parts.append(jnp.where(part_row == 0, bc(hi), jnp.where(part_row == 1, bc(mid),
                                                                    jnp.where(part_row == 2, bc(lo), 0.0))))
        lhs = jnp.concatenate(parts, axis=0).astype(BF16)
        wide = jnp.dot(lhs, expand_ref[...], preferred_element_type=F32)
        wide_ref[pl.ds(t0, TOKENS_PER_MATMUL)] = wide.reshape(TOKENS_PER_MATMUL, SUBLANES, n_col)
        return carry

    lax.fori_loop(0, tb // TOKENS_PER_MATMUL, expand_tokens, 0)

    def left_operand(t):
        rows = [jnp.where(diag, jnp.broadcast_to(wide_ref[t, p:p + 1, :], (D_ROWS, n_col)), 0.0) for p in range(3)]
        return jnp.concatenate(rows, axis=0).astype(BF16)

    def weighted_sum(t, a):
        rows = [tab_ref[e_ref[t * N_PAIRS + k]].astype(F32) for k in range(N_PAIRS)]
        rhs = jnp.concatenate(rows, axis=0).astype(BF16)
        y3 = jnp.dot(a, rhs, preferred_element_type=F32)
        return (y3[0:D_ROWS] + y3[D_ROWS:2 * D_ROWS]) + y3[2 * D_ROWS:]

    tokens_per_step = 4

    def some_tokens(i, carry):
        ts = [tokens_per_step * i + j for j in range(tokens_per_step)]
        ys = [weighted_sum(t, left_operand(t)) for t in ts]
        for t, y in zip(ts, ys):
            y_ref[t] = y
        return carry

    lax.fori_loop(0, tb // tokens_per_step, some_tokens, 0)
    z = alpha * x_ref[...] + y_ref[...]
    inv_d = 1.0 / D_MODEL
    mu = jnp.sum(jnp.sum(z, axis=2, keepdims=True), axis=1, keepdims=True) * inv_d
    zc = z - mu
    var = jnp.sum(jnp.sum(zc * zc, axis=2, keepdims=True), axis=1, keepdims=True) * inv_d
    o_ref[...] = zc * lax.rsqrt(var + LN_EPS) * g_ref[...] + b_ref[...]


def _peer_v(experts, wgt, x3, tab, g, b, alpha, tb=128):
    T = x3.shape[0]
    return pl.pallas_call(
        functools.partial(_peer_v_kernel, tb=tb, alpha=alpha),
        grid=(T // tb,),
        in_specs=[pl.BlockSpec((tb * N_PAIRS,), lambda i: (i,), memory_space=pltpu.SMEM),
                  pl.BlockSpec((tb, N_PAIRS), lambda i: (i, 0)),
                  pl.BlockSpec((tb, D_ROWS, LANES), lambda i: (i, 0, 0)),
                  pl.BlockSpec((N_EXPERTS, D_ROWS, LANES), lambda i: (0, 0, 0), pipeline_mode=pl.Buffered(1)),
                  pl.BlockSpec((N_PAIRS, N_PAIRS * D_ROWS), lambda i: (0, 0)),
                  pl.BlockSpec((1, D_ROWS, LANES), lambda i: (0, 0, 0)),
                  pl.BlockSpec((1, D_ROWS, LANES), lambda i: (0, 0, 0))],
        out_specs=pl.BlockSpec((tb, D_ROWS, LANES), lambda i: (i, 0, 0)),
        out_shape=jax.ShapeDtypeStruct((T, D_ROWS, LANES), F32),
        scratch_shapes=[pltpu.VMEM((tb, D_ROWS, LANES), F32), pltpu.VMEM((tb, SUBLANES, N_PAIRS * D_ROWS), F32)],
        compiler_params=_cparams(("arbitrary",)),
        name="peer_v",
    )(experts.reshape(T * N_PAIRS), wgt, x3, tab, jnp.asarray(_expand_matrix(), dtype=BF16),
      g.reshape(1, D_ROWS, LANES), b.reshape(1, D_ROWS, LANES))


def kernel(x, w_in, b_in, conv_w, conv_b, lru_wa, lru_ba, lru_wx, lru_bx, lru_lambda, cmp_pos_k, cmpk_w1, cmpk_b1, cmpk_w2, cmpk_b2, cmp_pos_v, cmpv_w1, cmpv_b1, cmpv_w2, cmpv_b2, gn_lru_g, gn_nsa_g, w_out, ln1_g, ln1_b, peer_wq, peer_subkeys, peer_u, peer_v, ln2_g, ln2_b):
    B, S, D = x.shape
    T = B * S
    depth = w_in.shape[0]
    alpha = (2 * depth) ** 0.25
    fan = CMP_STRIDE * HEAD_DIM
    xf = x.reshape(T, D)
    for l in range(depth):
        w_in_bf = jnp.pad(w_in[l], ((0, 0), (0, IN_COLS_PAD - IN_COLS))).astype(BF16)
        b_in_row = jnp.pad(b_in[l], (0, IN_COLS_PAD - IN_COLS)).reshape(1, IN_COLS_PAD)
        h_all = _in_proj(xf, w_in_bf, b_in_row)

        y_lru = _rglru(h_all, B, S, conv_w[l], conv_b[l], _block_diag(lru_wa[l]), lru_ba[l],
                       _block_diag(lru_wx[l]), lru_bx[l], lru_lambda[l], gn_lru_g[l])

        kv = h_all[:, COL_KV:COL_KV + 6 * KV_WIDTH].reshape(B, S, 6, N_KV, HEAD_DIM).transpose(2, 0, 3, 1, 4)
        kr = kv[0:2].reshape(2, B * N_KV, S // CMP_STRIDE, fan)
        w1 = jnp.stack([cmpk_w1[l], cmpv_w1[l]])
        w1ab = jnp.concatenate([w1[:, :fan], w1[:, fan:]], axis=2)
        pos = jnp.stack([cmp_pos_k[l], cmp_pos_v[l]]).reshape(2, 1, CMP_BLOCK * HEAD_DIM)
        posflat = jnp.broadcast_to(pos, (2, SUBLANES, CMP_BLOCK * HEAD_DIM))
        kcvc = _compress(kr, w1ab, w1, posflat,
                         jnp.stack([cmpk_b1[l], cmpv_b1[l]]).reshape(2, 1, HEAD_DIM),
                         jnp.stack([cmpk_w2[l], cmpv_w2[l]]),
                         jnp.stack([cmpk_b2[l], cmpv_b2[l]]).reshape(2, 1, HEAD_DIM))
        kcvc = kcvc.reshape(2, B, N_KV, S // CMP_STRIDE, HEAD_DIM)
        y_nsa = _nsa(h_all, B, S, kcvc[0], kcvc[1], kv[2], kv[3], kv[4], kv[5], gn_nsa_g[l])

        x1 = _out_ln(y_lru, y_nsa, xf, w_out[l].astype(BF16), ln1_g[l], ln1_b[l], alpha)

        experts, gates = _peer_route(x1, peer_wq[l].astype(BF16), peer_subkeys[l])
        x3 = x1.reshape(T, D_ROWS, LANES)
        wgt = _peer_u(experts, x3, _pack_table(peer_u[l]), gates)
        xf = _peer_v(experts, wgt, x3, _pack_table(peer_v[l]), ln2_g[l], ln2_b[l], alpha).reshape(T, D)
    return xf.reshape(B, S, D)
```

```python
import functools

import numpy as np
import jax
import jax.numpy as jnp
from jax import lax
from jax.experimental import pallas as pl
from jax.experimental.pallas import tpu as pltpu

D_MODEL = 1024
LRU_WIDTH = 512
LRU_BLOCKS = 8
LRU_BLOCK_DIM = LRU_WIDTH // LRU_BLOCKS
CONV_WIDTH = 4
LRU_C = 8.0
N_HEADS = 8
HEAD_DIM = 64
N_KV = 2
Q_PER_KV = N_HEADS // N_KV
NSA_WIDTH = N_HEADS * HEAD_DIM
KV_WIDTH = N_KV * HEAD_DIM
N_BRANCHES = 3
CMP_BLOCK = 32
CMP_STRIDE = 16
SEL_BLOCK = 64
SEL_TOPN = 16
WINDOW = 512
Q_BLOCK = 128
FORCE_BONUS = 1e4
NEG = -1e30
REMOVED = -3.0e38
PEER_HEADS = 8
PEER_KEYS = 128
PEER_TOPK = 16
PEER_KEY_DIM = 128
N_EXPERTS = PEER_KEYS * PEER_KEYS
N_PAIRS = PEER_HEADS * PEER_TOPK
LN_EPS = 1e-5
IN_SPLITS = (LRU_WIDTH, LRU_WIDTH, NSA_WIDTH) + (KV_WIDTH,) * 6 + (N_HEADS * N_BRANCHES,)
IN_COLS = sum(IN_SPLITS)

LANES = 128
SUBLANES = 8
VMEM_LIMIT_BYTES = 56 * 1024 * 1024

IN_COLS_PAD = -(-IN_COLS // LANES) * LANES
COL_LRU_X = 0
COL_LRU_GATE = LRU_WIDTH
COL_Q = 2 * LRU_WIDTH
COL_KV = COL_Q + NSA_WIDTH
COL_GATES = COL_KV + 6 * KV_WIDTH

HI = lax.Precision.HIGHEST
F32 = jnp.float32
BF16 = jnp.bfloat16

D_ROWS = D_MODEL // LANES


def _cparams(sem):
    return pltpu.CompilerParams(dimension_semantics=sem, vmem_limit_bytes=VMEM_LIMIT_BYTES)


def _dot_nt(a, b, precision=None):
    return lax.dot_general(a, b, (((1,), (1,)), ((), ())), precision=precision,
                           preferred_element_type=F32)


def _split2(a):
    hi = a.astype(BF16)
    return hi, (a - hi.astype(F32)).astype(BF16)


def _layer_norm_rows(z, g, b):
    mu = jnp.mean(z, axis=-1, keepdims=True)
    zc = z - mu
    var = jnp.mean(zc * zc, axis=-1, keepdims=True)
    return zc * lax.rsqrt(var + LN_EPS) * g + b


V_AUG_ROWS = HEAD_DIM + 16


def _in_proj_kernel(x_ref, w_ref, b_ref, o_ref, kraw_ref, kbf_ref, vt_ref):
    h = jnp.dot(x_ref[...].astype(BF16), w_ref[...], preferred_element_type=F32) + b_ref[...]
    o_ref[...] = h
    group_cols = lambda col, g: slice(col + g * HEAD_DIM, col + (g + 1) * HEAD_DIM)
    for j in range(2):
        for g in range(N_KV):
            kraw_ref[j, 0, g] = h[:, group_cols(COL_KV + j * KV_WIDTH, g)]
            kbf_ref[j, 0, g] = h[:, group_cols(COL_KV + (2 + 2 * j) * KV_WIDTH, g)].astype(BF16)
    ones_row = (lax.broadcasted_iota(jnp.int32, (V_AUG_ROWS - HEAD_DIM, Q_BLOCK), 0) == 0).astype(F32)
    for j in range(2):
        col = COL_KV + (3 + 2 * j) * KV_WIDTH
        for qb in range(h.shape[0] // Q_BLOCK):
            v_t = h[qb * Q_BLOCK:(qb + 1) * Q_BLOCK, col:col + KV_WIDTH].T
            for g in range(N_KV):
                tile = jnp.concatenate([v_t[g * HEAD_DIM:(g + 1) * HEAD_DIM, :], ones_row], axis=0)
                vt_ref[j, 0, g, qb] = tile.astype(BF16)


def _in_proj(xf, w_bf, b_row, B, S, tm=512):
    T = xf.shape[0]
    tm = min(tm, S)
    nt = S // tm
    per_tile = lambda *dims: pl.BlockSpec((2, 1, N_KV) + dims, lambda i: (0, i // nt, 0, i % nt) + (0,) * (len(dims) - 1))
    return pl.pallas_call(
        _in_proj_kernel,
        grid=(T // tm,),
        in_specs=[pl.BlockSpec((tm, D_MODEL), lambda i: (i, 0)),
                  pl.BlockSpec((D_MODEL, IN_COLS_PAD), lambda i: (0, 0)),
                  pl.BlockSpec((1, IN_COLS_PAD), lambda i: (0, 0))],
        out_specs=[pl.BlockSpec((tm, IN_COLS_PAD), lambda i: (i, 0)),
                   per_tile(tm, HEAD_DIM), per_tile(tm, HEAD_DIM),
                   per_tile(tm // Q_BLOCK, V_AUG_ROWS, Q_BLOCK)],
        out_shape=[jax.ShapeDtypeStruct((T, IN_COLS_PAD), F32),
                   jax.ShapeDtypeStruct((2, B, N_KV, S, HEAD_DIM), F32),
                   jax.ShapeDtypeStruct((2, B, N_KV, S, HEAD_DIM), BF16),
                   jax.ShapeDtypeStruct((2, B, N_KV, S // Q_BLOCK, V_AUG_ROWS, Q_BLOCK), BF16)],
        compiler_params=_cparams(("arbitrary",)),
        name="in_proj",
    )(xf, w_bf, b_row)


def _rglru_kernel(x_ref, gate_ref, cw_ref, cb_ref, wa_ref, ba_ref, wx_ref, bx_ref, lam_ref, gn_ref,
                  o_ref, xprev_ref, h_ref, *, ts):
    i = pl.program_id(1)

    @pl.when(i == 0)
    def _():
        xprev_ref[...] = jnp.zeros_like(xprev_ref)
        h_ref[...] = jnp.zeros_like(h_ref)

    xb = x_ref[...]
    xp = xprev_ref[...]
    row = lax.broadcasted_iota(jnp.int32, xb.shape, 0)
    xc = cw_ref[CONV_WIDTH - 1:CONV_WIDTH, :] * xb + cb_ref[...]
    for j in range(1, CONV_WIDTH):
        shifted = jnp.where(row >= j, pltpu.roll(xb, j, axis=0), pltpu.roll(xp, j, axis=0))
        xc = xc + cw_ref[CONV_WIDTH - 1 - j:CONV_WIDTH - j, :] * shifted
    xprev_ref[...] = xb

    r = jax.nn.sigmoid(jnp.dot(xc, wa_ref[...], precision=HI, preferred_element_type=F32) + ba_ref[...])
    ig = jax.nn.sigmoid(jnp.dot(xc, wx_ref[...], precision=HI, preferred_element_type=F32) + bx_ref[...])
    lam = lam_ref[...]
    softplus_neg_lam = jnp.maximum(-lam, 0.0) + jnp.log1p(jnp.exp(-jnp.abs(lam)))
    log_a = -LRU_C * r * softplus_neg_lam
    a = jnp.exp(log_a)
    th = jnp.tanh(log_a)
    u = jnp.sqrt(-2.0 * th / (1.0 - th)) * (ig * xc)

    d = 1
    while d < ts:
        keep = row >= d
        a_sh = pltpu.roll(a, d, axis=0)
        u_sh = pltpu.roll(u, d, axis=0)
        u = jnp.where(keep, a * u_sh + u, u)
        a = jnp.where(keep, a * a_sh, a)
        d *= 2
    h = u + a * h_ref[0:1, :]
    h_ref[0:1, :] = h[ts - 1:ts, :]

    y = h * jax.nn.gelu(gate_ref[...])
    y = y * lax.rsqrt(jnp.mean(y * y, axis=-1, keepdims=True) + LN_EPS) * gn_ref[...]
    o_ref[...] = y


def _rglru(h_all, B, S, cw, cb, wa_bd, ba, wx_bd, bx, lam, gn, ts=512):
    ts = min(ts, S)
    nt = S // ts
    C = LRU_WIDTH
    row = lambda v: v.reshape(1, C)
    const = lambda shape: pl.BlockSpec(shape, lambda b, i: (0,) * len(shape))
    return pl.pallas_call(
        functools.partial(_rglru_kernel, ts=ts),
        grid=(B, nt),
        in_specs=[pl.BlockSpec((ts, C), lambda b, i: (b * nt + i, COL_LRU_X // C)),
                  pl.BlockSpec((ts, C), lambda b, i: (b * nt + i, COL_LRU_GATE // C)),
                  const((CONV_WIDTH, C)), const((1, C)), const((C, C)), const((1, C)),
                  const((C, C)), const((1, C)), const((1, C)), const((1, C))],
        out_specs=pl.BlockSpec((ts, C), lambda b, i: (b * nt + i, 0)),
        out_shape=jax.ShapeDtypeStruct((B * S, C), F32),
        scratch_shapes=[pltpu.VMEM((ts, C), F32), pltpu.VMEM((SUBLANES, C), F32)],
        compiler_params=_cparams(("arbitrary", "arbitrary")),
        name="rglru",
    )(h_all, h_all, cw, row(cb), wa_bd, row(ba), wx_bd, row(bx), row(lam), row(gn))


def _block_diag(w):
    n, d, e = w.shape
    eye = jnp.eye(n, dtype=w.dtype)
    return (w[:, :, None, :] * eye[:, None, :, None]).reshape(n * d, n * e)


def _compress_kernel(k_ref, w1l_ref, w1_ref, pos_ref, b1_ref, w2_ref, b2_ref, o_ref):
    nr = o_ref.shape[2]
    ab = jnp.zeros((nr, 2 * HEAD_DIM), F32)
    for l in range(CMP_STRIDE):
        k_l = k_ref[0, 0, 0, pl.ds(l, nr, stride=CMP_STRIDE), :]
        ab = ab + jnp.dot(k_l, w1l_ref[0, l], precision=HI, preferred_element_type=F32)
    ab_next = pltpu.roll(ab, nr - 1, axis=0)
    pre = ab[:, :HEAD_DIM] + ab_next[:, HEAD_DIM:]
    posc = jnp.dot(pos_ref[0], w1_ref[0], precision=HI, preferred_element_type=F32)[0:1, :]
    h1 = jax.nn.gelu(pre + posc + b1_ref[0])
    o_ref[0, 0] = jnp.dot(h1, w2_ref[0], precision=HI, preferred_element_type=F32) + b2_ref[0]


def _compress(kraw, w1l, w1, posflat, b1, w2, b2):
    _, B, G, S, _ = kraw.shape
    nr = S // CMP_STRIDE
    fan = CMP_BLOCK * HEAD_DIM
    wspec = lambda shape: pl.BlockSpec((1,) + shape, lambda s, i: (s,) + (0,) * len(shape))
    return pl.pallas_call(
        _compress_kernel,
        grid=(2, B * G),
        in_specs=[pl.BlockSpec((1, 1, 1, S, HEAD_DIM), lambda s, i: (s, i // G, i % G, 0, 0)),
                  wspec((CMP_STRIDE, HEAD_DIM, 2 * HEAD_DIM)), wspec((fan, HEAD_DIM)), wspec((SUBLANES, fan)),
                  wspec((1, HEAD_DIM)), wspec((HEAD_DIM, HEAD_DIM)), wspec((1, HEAD_DIM))],
        out_specs=pl.BlockSpec((1, 1, nr, HEAD_DIM), lambda s, i: (s, i, 0, 0)),
        out_shape=jax.ShapeDtypeStruct((2, B * G, nr, HEAD_DIM), F32),
        compiler_params=_cparams(("arbitrary", "arbitrary")),
        name="compress",
    )(kraw, w1l, w1, posflat, b1, w2, b2)


def _selection_matrix(n_rows, n_cmp, n_sel):
    cs = np.arange(n_rows)[:, None] * CMP_STRIDE
    ss = np.arange(n_sel)[None, :] * SEL_BLOCK
    ov = np.clip(np.minimum(cs + CMP_BLOCK, ss + SEL_BLOCK) - np.maximum(cs, ss), 0, None)
    ov = (ov / CMP_STRIDE).astype(np.float32)
    ov[n_cmp:] = 0.0
    return ov


SEL_TILE = 4 * Q_BLOCK
WIN_SPAN = WINDOW + Q_BLOCK


def _nsa_kernel(q_ref, gt_ref, kc_ref, vct_ref, ks_ref, vst_ref, kw_ref, vwt_ref, slope_ref, bias_ref,
                selmat_ref, emat_ref, gn_ref, o_ref, *, n_cmp, n_top, win_span):
    qt = pl.program_id(1)
    q0 = qt * Q_BLOCK
    R = Q_PER_KV
    cols = R * Q_BLOCK
    n_rows_c = kc_ref.shape[2]
    n_sel = selmat_ref.shape[0]
    q_loc = lax.broadcasted_iota(jnp.int32, (1, cols), 1) % Q_BLOCK
    t_col = q0 + q_loc
    gates_t = jax.nn.sigmoid(gt_ref[...]).T

    def step(carry, qs_bf, slope, bias, k_bf, v_tiles, offset, ok):
        m, acc = carry
        s = jnp.where(ok, _dot_nt(k_bf, qs_bf) + bias, NEG)
        c = slope * (-offset).astype(F32)
        m_new = jnp.maximum(m, jnp.max(s, axis=0, keepdims=True) + c)
        p = jnp.exp(s + (c - m_new)).astype(BF16)
        acc = jnp.exp(m - m_new) * acc
        for i, v in enumerate(v_tiles):
            acc = acc + jnp.dot(v, p[i * Q_BLOCK:(i + 1) * Q_BLOCK, :], preferred_element_type=F32)
        return m_new, acc

    init = (jnp.full((1, cols), NEG, F32), jnp.zeros((V_AUG_ROWS, cols), F32))
    finish = lambda acc: acc[0:HEAD_DIM, :] / acc[HEAD_DIM:HEAD_DIM + 1, :]
    y_rows = []
    for g in range(N_KV):
        qg = q_ref[:, g * R * HEAD_DIM:(g + 1) * R * HEAD_DIM] * (HEAD_DIM ** -0.5)
        qs = jnp.concatenate([qg[:, r * HEAD_DIM:(r + 1) * HEAD_DIM] for r in range(R)], axis=0)
        qs_bf = qs.astype(BF16)
        slope = slope_ref[g]

        kc = kc_ref[0, g]
        c_idx = lax.broadcasted_iota(jnp.int32, (n_rows_c, 1), 0)
        cmp_end = c_idx * CMP_STRIDE + (CMP_BLOCK - 1)
        valid_c = (cmp_end <= t_col) & (c_idx < n_cmp)
        q_hi, q_lo = _split2(qs)
        k_hi, k_lo = _split2(kc)
        s = _dot_nt(k_hi, q_hi) + (_dot_nt(k_hi, q_lo) + _dot_nt(k_lo, q_hi))
        s = s - slope * jnp.abs(t_col - cmp_end).astype(F32)
        s = jnp.where(valid_c, s, NEG)
        e = jnp.exp(s - jnp.max(s, axis=0, keepdims=True))
        p_c = e / jnp.sum(e, axis=0, keepdims=True)
        p_c = p_c * (t_col >= CMP_BLOCK - 1).astype(F32)
        o_c = jnp.dot(vct_ref[0, g].astype(BF16), p_c.astype(BF16), preferred_element_type=F32)

        p_sum = p_c[:, 0:Q_BLOCK]
        for r in range(1, R):
            p_sum = p_sum + p_c[:, r * Q_BLOCK:(r + 1) * Q_BLOCK]
        p_hi, p_lo = _split2(p_sum)
        dot = lambda a, b: jnp.dot(a, b, preferred_element_type=F32)
        imp_t = dot(selmat_ref[...], p_hi) + dot(selmat_ref[...], p_lo)
        j_idx = lax.broadcasted_iota(jnp.int32, (n_sel, Q_BLOCK), 0)
        j_f = j_idx.astype(F32)
        cur = t_col[:, 0:Q_BLOCK] // SEL_BLOCK
        forced = (j_idx == 0) | (j_idx == cur) | (j_idx == cur - 1)
        work = jnp.where(j_idx <= cur, imp_t + jnp.where(forced, FORCE_BONUS, 0.0), NEG)
        msel_t = jnp.zeros((n_sel, Q_BLOCK), F32)
        for _ in range(n_top):
            mx = jnp.max(work, axis=0, keepdims=True)
            first = jnp.min(jnp.where(work == mx, j_f, float(LANES)), axis=0, keepdims=True)
            pick = j_f == first
            msel_t = jnp.where(pick, 1.0, msel_t)
            work = jnp.where(pick, REMOVED, work)
        if n_sel < LANES:
            msel_t = jnp.concatenate([msel_t, jnp.zeros((LANES - n_sel, Q_BLOCK), F32)], axis=0)
        msel_bf = msel_t.astype(BF16)

        sel_key = lax.broadcasted_iota(jnp.int32, (SEL_TILE, cols), 0)
        bias_sel = bias_ref[g, 0:SEL_TILE, :]

        def sel_span(kt, carry, causal):
            k0 = pl.multiple_of(kt * SEL_TILE, SEL_TILE)
            ch = jnp.dot(emat_ref[kt], msel_bf, preferred_element_type=F32)
            ok = jnp.concatenate([ch] * R, axis=1) > 0.5
            if causal:
                ok = ok & (sel_key <= q_loc + (q0 - k0))
            v_tiles = [vst_ref[0, g, kt * (SEL_TILE // Q_BLOCK) + i] for i in range(SEL_TILE // Q_BLOCK)]
            return step(carry, qs_bf, slope, bias_sel, ks_ref[0, g, pl.ds(k0, SEL_TILE), :], v_tiles, q0 - k0, ok)

        kt_diag = q0 // SEL_TILE
        carry = sel_span(kt_diag, init, True)
        carry = lax.fori_loop(0, kt_diag, lambda kt, c: sel_span(kt, c, False), carry)
        o_s = finish(carry[1])

        w0 = pl.multiple_of(jnp.maximum(q0 + Q_BLOCK - win_span, 0), Q_BLOCK)
        win_key = lax.broadcasted_iota(jnp.int32, (win_span, cols), 0)
        reach = q_loc + (q0 - w0)
        ok_w = (win_key <= reach) & (win_key > reach - WINDOW)
        v_tiles = [vwt_ref[0, g, w0 // Q_BLOCK + i] for i in range(win_span // Q_BLOCK)]
        carry = step(init, qs_bf, slope, bias_ref[g, 0:win_span, :], kw_ref[0, g, pl.ds(w0, win_span), :],
                     v_tiles, q0 - w0, ok_w)
        o_w = finish(carry[1])

        for r in range(R):
            col = (g * R + r) * N_BRANCHES
            sl = slice(r * Q_BLOCK, (r + 1) * Q_BLOCK)
            y_rows.append(gates_t[col:col + 1, :] * o_c[:, sl] + gates_t[col + 1:col + 2, :] * o_s[:, sl]
                          + gates_t[col + 2:col + 3, :] * o_w[:, sl])
    y_t = jnp.concatenate(y_rows, axis=0)
    y_t = y_t * lax.rsqrt(jnp.mean(y_t * y_t, axis=0, keepdims=True) + LN_EPS) * gn_ref[...]
    o_ref[...] = y_t.T


def _nsa(h_all, B, S, kc, vc, ks_bf, vs_tiles, kw_bf, vw_tiles, gn):
    nq = S // Q_BLOCK
    n_cmp = (S - CMP_BLOCK) // CMP_STRIDE + 1
    n_rows_c = kc.shape[2]
    n_sel = S // SEL_BLOCK
    n_top = min(SEL_TOPN, n_sel)
    n_span = S // SEL_TILE
    win_span = min(WIN_SPAN, S)
    n_bias = max(win_span, SEL_TILE)
    cols = Q_PER_KV * Q_BLOCK
    slopes = np.array([2.0 ** (-8.0 * (h + 1) / N_HEADS) for h in range(N_HEADS)], np.float32)
    slope_cols = np.repeat(slopes.reshape(N_KV, Q_PER_KV), Q_BLOCK, axis=1)[:, None, :]
    rel = (np.arange(cols)[None, :] % Q_BLOCK - np.arange(n_bias)[:, None]).astype(np.float32)
    bias = -slope_cols * rel[None]
    selmat_t = jnp.asarray(_selection_matrix(n_rows_c, n_cmp, n_sel).T, dtype=BF16)
    key_blk = (np.arange(S) // SEL_BLOCK).reshape(n_span, SEL_TILE, 1)
    emat_t = jnp.asarray((key_blk == np.arange(LANES)[None, None, :]).astype(np.float32), dtype=BF16)
    qw = Q_PER_KV * HEAD_DIM * N_KV
    per_batch = lambda *dims: pl.BlockSpec((1,) + dims, lambda b, i: (b,) + (0,) * len(dims))
    const = lambda shape: pl.BlockSpec(shape, lambda b, i: (0,) * len(shape))
    return pl.pallas_call(
        functools.partial(_nsa_kernel, n_cmp=n_cmp, n_top=n_top, win_span=win_span),
        grid=(B, nq),
        in_specs=[pl.BlockSpec((Q_BLOCK, qw), lambda b, i: (b * nq + i, COL_Q // qw)),
                  pl.BlockSpec((Q_BLOCK, LANES), lambda b, i: (b * nq + i, COL_GATES // LANES)),
                  per_batch(N_KV, n_rows_c, HEAD_DIM), per_batch(N_KV, HEAD_DIM, n_rows_c),
                  per_batch(N_KV, S, HEAD_DIM), per_batch(N_KV, nq, V_AUG_ROWS, Q_BLOCK),
                  per_batch(N_KV, S, HEAD_DIM), per_batch(N_KV, nq, V_AUG_ROWS, Q_BLOCK),
                  const((N_KV, 1, cols)), const((N_KV, n_bias, cols)), const((n_sel, n_rows_c)),
                  const((n_span, SEL_TILE, LANES)), const((NSA_WIDTH, 1))],
        out_specs=pl.BlockSpec((Q_BLOCK, NSA_WIDTH), lambda b, i: (b * nq + i, 0)),
        out_shape=jax.ShapeDtypeStruct((B * S, NSA_WIDTH), F32),
        compiler_params=_cparams(("arbitrary", "arbitrary")),
        name="nsa",
    )(h_all, h_all, kc, vc.transpose(0, 1, 3, 2), ks_bf, vs_tiles, kw_bf, vw_tiles,
      jnp.asarray(slope_cols), jnp.asarray(bias), selmat_t, emat_t, gn.reshape(NSA_WIDTH, 1))


def _out_ln_kernel(ylru_ref, ynsa_ref, x_ref, w_ref, g_ref, b_ref, o_ref, *, alpha):
    mix = jnp.dot(ylru_ref[...].astype(BF16), w_ref[0:LRU_WIDTH, :], preferred_element_type=F32)
    mix = mix + jnp.dot(ynsa_ref[...].astype(BF16), w_ref[LRU_WIDTH:, :], preferred_element_type=F32)
    o_ref[...] = _layer_norm_rows(alpha * x_ref[...] + mix, g_ref[...], b_ref[...])


def _out_ln(ylru, ynsa, xf, w_bf, g, b, alpha, tm=512):
    T = xf.shape[0]
    const = lambda shape: pl.BlockSpec(shape, lambda i: (0,) * len(shape))
    return pl.pallas_call(
        functools.partial(_out_ln_kernel, alpha=alpha),
        grid=(T // tm,),
        in_specs=[pl.BlockSpec((tm, LRU_WIDTH), lambda i: (i, 0)),
                  pl.BlockSpec((tm, NSA_WIDTH), lambda i: (i, 0)),
                  pl.BlockSpec((tm, D_MODEL), lambda i: (i, 0)),
                  const((D_MODEL, D_MODEL)), const((1, D_MODEL)), const((1, D_MODEL))],
        out_specs=pl.BlockSpec((tm, D_MODEL), lambda i: (i, 0)),
        out_shape=jax.ShapeDtypeStruct((T, D_MODEL), F32),
        compiler_params=_cparams(("arbitrary",)),
        name="out_ln",
    )(ylru, ynsa, xf, w_bf, g.reshape(1, D_MODEL), b.reshape(1, D_MODEL))


def _topk_rows(vals, order, payload, n_keep):
    n = vals.shape[1]
    kidx = lax.broadcasted_iota(jnp.int32, (n_keep, n), 0)
    never = float(1 << 20)
    kept_v = jnp.zeros((n_keep, n), F32)
    kept_p = jnp.zeros((n_keep, n), F32)
    work = vals
    for r in range(n_keep):
        mx = jnp.max(work, axis=0, keepdims=True)
        first = jnp.min(jnp.where(work == mx, order, never), axis=0, keepdims=True)
        pick = order == first
        kept_v = jnp.where(kidx == r, mx, kept_v)
        if payload is None:
            kept_p = jnp.where(kidx == r, first, kept_p)
        else:
            kept_p = jnp.where(kidx == r, jnp.max(jnp.where(pick, payload, -1.0), axis=0, keepdims=True), kept_p)
        work = jnp.where(pick, REMOVED, work)
    return kept_v, kept_p


def _candidate_rows():
    ij = [(i, 0) for i in range(PEER_TOPK)]
    for j in range(1, SUBLANES):
        ij += [(i, j) for i in range(SUBLANES)]
    ij += [(0, j) for j in range(SUBLANES, PEER_TOPK)]
    flat = np.array([i * PEER_TOPK + j for i, j in ij], np.float32)
    valid = np.array([(i + 1) * (j + 1) <= PEER_TOPK for i, j in ij])
    return flat[:, None], np.where(valid, 0.0, REMOVED).astype(np.float32)[:, None]


def _candidates(a0, a1):
    return ([(a0, a1[0:1, :])]
            + [(a0[0:SUBLANES, :], a1[j:j + 1, :]) for j in range(1, SUBLANES)]
            + [(a0[0:1, :], a1[SUBLANES:PEER_TOPK, :])])


def _peer_route_kernel(x_ref, wq_ref, sk_ref, flat_ref, pad_ref, e_ref, g_ref):
    tm = x_ref.shape[0]
    q = jnp.dot(x_ref[...].astype(BF16), wq_ref[...], preferred_element_type=F32)
    key_idx = lax.broadcasted_iota(jnp.int32, (PEER_KEYS, tm), 0).astype(F32)
    flat = jnp.broadcast_to(flat_ref[...], (flat_ref.shape[0], tm))
    experts, gates = [], []
    for h in range(PEER_HEADS):
        sv, si = [], []
        for c in range(2):
            col = (h * 2 + c) * PEER_KEY_DIM
            s_t = _dot_nt(sk_ref[c], q[:, col:col + PEER_KEY_DIM], HI)
            v, i = _topk_rows(s_t, key_idx, None, PEER_TOPK)
            sv.append(v)
            si.append(i)
        cand = jnp.concatenate([a + b for a, b in _candidates(sv[0], sv[1])], axis=0) + pad_ref[...]
        ecand = jnp.concatenate([a * float(PEER_KEYS) + b for a, b in _candidates(si[0], si[1])], axis=0)
        cv, ce = _topk_rows(cand, flat, ecand, PEER_TOPK)
        ex = jnp.exp(cv - cv[0:1, :])
        gates.append(ex / jnp.sum(ex, axis=0, keepdims=True))
        experts.append(ce)
    e_ref[...] = jnp.concatenate(experts, axis=0).T.astype(jnp.int32)
    g_ref[...] = jnp.concatenate(gates, axis=0).T


def _peer_route(x1, wq_bf, subkeys, tm=256):
    T = x1.shape[0]
    tm = min(tm, T)
    nq = wq_bf.shape[1]
    flat, pad = _candidate_rows()
    n_cand = flat.shape[0]
    out = lambda: pl.BlockSpec((tm, N_PAIRS), lambda i: (i, 0))
    return pl.pallas_call(
        _peer_route_kernel,
        grid=(T // tm,),
        in_specs=[pl.BlockSpec((tm, D_MODEL), lambda i: (i, 0)),
                  pl.BlockSpec((D_MODEL, nq), lambda i: (0, 0)),
                  pl.BlockSpec((2, PEER_KEYS, PEER_KEY_DIM), lambda i: (0, 0, 0)),
                  pl.BlockSpec((n_cand, 1), lambda i: (0, 0)),
                  pl.BlockSpec((n_cand, 1), lambda i: (0, 0))],
        out_specs=[out(), out()],
        out_shape=[jax.ShapeDtypeStruct((T, N_PAIRS), jnp.int32),
                   jax.ShapeDtypeStruct((T, N_PAIRS), F32)],
        compiler_params=_cparams(("arbitrary",)),
        name="peer_route",
    )(x1, wq_bf, subkeys, jnp.asarray(flat), jnp.asarray(pad))


TOKENS_PER_MATMUL = 8


def _pack_table(tab):
    return tab.astype(BF16).reshape(N_EXPERTS, D_ROWS, LANES)


def _sublane_sums(ps, roll, where, sub):
    lvl = ps
    for r in (1, 2, 4):
        m = (sub % (2 * r)) < r
        half = len(lvl) // 2
        lvl = [where(m, lvl[i], lvl[i + half]) + roll(where(m, lvl[i + half], lvl[i]), r) for i in range(half)]
    return lvl[0]


def _sum_src():
    sub = np.arange(SUBLANES)[:, None] * np.ones((1, LANES), np.int64)
    ps = [np.full((SUBLANES, LANES), 10.0 ** i) for i in range(SUBLANES)]
    out = _sublane_sums(ps, lambda v, r: np.roll(v, r, axis=0), np.where, sub)
    return [int(round(np.log10(out[s, 0] / SUBLANES))) for s in range(SUBLANES)]


_SUM_SRC = _sum_src()


def _peer_u_kernel(e_ref, x_ref, tab_ref, gate_ref, w_ref, part_ref, zt_ref, *, tb):
    sub = lax.broadcasted_iota(jnp.int32, (SUBLANES, LANES), 0)
    lane = lax.broadcasted_iota(jnp.int32, (N_PAIRS, tb), 1)

    def gather_dots(t, slot):
        xt = x_ref[t]
        for grp in range(N_PAIRS // SUBLANES):
            prods = [None] * SUBLANES
            for s in range(SUBLANES):
                prods[_SUM_SRC[s]] = tab_ref[e_ref[t * N_PAIRS + grp * SUBLANES + s]].astype(F32) * xt
            part_ref[slot, grp * SUBLANES:(grp + 1) * SUBLANES, :] = _sublane_sums(
                prods, lambda v, r: pltpu.roll(v, r, axis=0), jnp.where, sub)

    def lane_sums(t0):
        z0 = jnp.sum(part_ref[0], axis=1, keepdims=True)
        z1 = jnp.sum(part_ref[1], axis=1, keepdims=True)
        zt_ref[...] = jnp.where(lane == t0, z0, jnp.where(lane == t0 + 1, z1, zt_ref[...]))

    part_ref[...] = jnp.zeros_like(part_ref)
    zt_ref[...] = jnp.zeros_like(zt_ref)

    def two_tokens(i, carry):
        lane_sums(2 * i - 2)
        gather_dots(2 * i, 0)
        gather_dots(2 * i + 1, 1)
        return carry

    lax.fori_loop(0, tb // 2, two_tokens, 0)
    lane_sums(tb - 2)
    w_ref[...] = jax.nn.gelu(zt_ref[...].T) * gate_ref[...]


def _peer_u(experts, x3, tab, gates, tb=128):
    T = x3.shape[0]
    vm = lambda: pl.BlockSpec((tb, N_PAIRS), lambda i: (i, 0))
    return pl.pallas_call(
        functools.partial(_peer_u_kernel, tb=tb),
        grid=(T // tb,),
        in_specs=[pl.BlockSpec((tb * N_PAIRS,), lambda i: (i,), memory_space=pltpu.SMEM),
                  pl.BlockSpec((tb, D_ROWS, LANES), lambda i: (i, 0, 0)),
                  pl.BlockSpec((N_EXPERTS, D_ROWS, LANES), lambda i: (0, 0, 0), pipeline_mode=pl.Buffered(1)),
                  vm()],
        out_specs=vm(),
        out_shape=jax.ShapeDtypeStruct((T, N_PAIRS), F32),
        scratch_shapes=[pltpu.VMEM((2, N_PAIRS, LANES), F32), pltpu.VMEM((N_PAIRS, tb), F32)],
        compiler_params=_cparams(("arbitrary",)),
        name="peer_u",
    )(experts.reshape(T * N_PAIRS), x3, tab, gates)


def _expand_matrix():
    k = np.arange(N_PAIRS)[:, None]
    c = np.arange(N_PAIRS * D_ROWS)[None, :]
    return (c // D_ROWS == k).astype(np.float32)


def _peer_v_kernel(e_ref, w_ref, x_ref, tab_ref, expand_ref, g_ref, b_ref, o_ref, y_ref, wide_ref, *, tb, alpha):
    n_col = N_PAIRS * D_ROWS
    diag = (lax.broadcasted_iota(jnp.int32, (D_ROWS, n_col), 1) % D_ROWS
            == lax.broadcasted_iota(jnp.int32, (D_ROWS, n_col), 0))
    part_row = lax.broadcasted_iota(jnp.int32, (SUBLANES, LANES), 0)

    def expand_tokens(i, carry):
        t0 = pl.multiple_of(i * TOKENS_PER_MATMUL, TOKENS_PER_MATMUL)
        parts = []
        for j in range(TOKENS_PER_MATMUL):
            wrow = w_ref[pl.ds(t0 + j, 1), :]
            hi = wrow.astype(BF16).astype(F32)
            mid = (wrow - hi).astype(BF16).astype(F32)
            lo = ((wrow - hi) - mid).astype(BF16).astype(F32)
            bc = lambda v: jnp.broadcast_to(v, (SUBLANES, LANES))
            parts.append(jnp.where(part_row == 0, bc(hi), jnp.where(part_row == 1, bc(mid),
                                                                    jnp.where(part_row == 2, bc(lo), 0.0))))
        lhs = jnp.concatenate(parts, axis=0).astype(BF16)
        wide = jnp.dot(lhs, expand_ref[...], preferred_element_type=F32)
        wide_ref[pl.ds(t0, TOKENS_PER_MATMUL)] = wide.reshape(TOKENS_PER_MATMUL, SUBLANES, n_col)
        return carry

    lax.fori_loop(0, tb // TOKENS_PER_MATMUL, expand_tokens, 0)

    def left_operand(t):
        rows = [jnp.where(diag, jnp.broadcast_to(wide_ref[t, p:p + 1, :], (D_ROWS, n_col)), 0.0) for p in range(3)]
        return jnp.concatenate(rows, axis=0).astype(BF16)

    def weighted_sum(t, a):
        rows = [tab_ref[e_ref[t * N_PAIRS + k]].astype(F32) for k in range(N_PAIRS)]
        rhs = jnp.concatenate(rows, axis=0).astype(BF16)
        y3 = jnp.dot(a, rhs, preferred_element_type=F32)
        return (y3[0:D_ROWS] + y3[D_ROWS:2 * D_ROWS]) + y3[2 * D_ROWS:]

    tokens_per_step = 8

    def some_tokens(i, carry):
        ts = [tokens_per_step * i + j for j in range(tokens_per_step)]
        ys = [weighted_sum(t, left_operand(t)) for t in ts]
        for t, y in zip(ts, ys):
            y_ref[t] = y
        return carry

    lax.fori_loop(0, tb // tokens_per_step, some_tokens, 0)
    z = alpha * x_ref[...] + y_ref[...]
    inv_d = 1.0 / D_MODEL
    mu = jnp.sum(jnp.sum(z, axis=2, keepdims=True), axis=1, keepdims=True) * inv_d
    zc = z - mu
    var = jnp.sum(jnp.sum(zc * zc, axis=2, keepdims=True), axis=1, keepdims=True) * inv_d
    o_ref[...] = zc * lax.rsqrt(var + LN_EPS) * g_ref[...] + b_ref[...]


def _peer_v(experts, wgt, x3, tab, g, b, alpha, tb=128):
    T = x3.shape[0]
    return pl.pallas_call(
        functools.partial(_peer_v_kernel, tb=tb, alpha=alpha),
        grid=(T // tb,),
        in_specs=[pl.BlockSpec((tb * N_PAIRS,), lambda i: (i,), memory_space=pltpu.SMEM),
                  pl.BlockSpec((tb, N_PAIRS), lambda i: (i, 0)),
                  pl.BlockSpec((tb, D_ROWS, LANES), lambda i: (i, 0, 0)),
                  pl.BlockSpec((N_EXPERTS, D_ROWS, LANES), lambda i: (0, 0, 0), pipeline_mode=pl.Buffered(1)),
                  pl.BlockSpec((N_PAIRS, N_PAIRS * D_ROWS), lambda i: (0, 0)),
                  pl.BlockSpec((1, D_ROWS, LANES), lambda i: (0, 0, 0)),
                  pl.BlockSpec((1, D_ROWS, LANES), lambda i: (0, 0, 0))],
        out_specs=pl.BlockSpec((tb, D_ROWS, LANES), lambda i: (i, 0, 0)),
        out_shape=jax.ShapeDtypeStruct((T, D_ROWS, LANES), F32),
        scratch_shapes=[pltpu.VMEM((tb, D_ROWS, LANES), F32), pltpu.VMEM((tb, SUBLANES, N_PAIRS * D_ROWS), F32)],
        compiler_params=_cparams(("arbitrary",)),
        name="peer_v",
    )(experts.reshape(T * N_PAIRS), wgt, x3, tab, jnp.asarray(_expand_matrix(), dtype=BF16),
      g.reshape(1, D_ROWS, LANES), b.reshape(1, D_ROWS, LANES))


def kernel(x, w_in, b_in, conv_w, conv_b, lru_wa, lru_ba, lru_wx, lru_bx, lru_lambda, cmp_pos_k, cmpk_w1, cmpk_b1, cmpk_w2, cmpk_b2, cmp_pos_v, cmpv_w1, cmpv_b1, cmpv_w2, cmpv_b2, gn_lru_g, gn_nsa_g, w_out, ln1_g, ln1_b, peer_wq, peer_subkeys, peer_u, peer_v, ln2_g, ln2_b):
    B, S, D = x.shape
    T = B * S
    depth = w_in.shape[0]
    alpha = (2 * depth) ** 0.25
    xf = x.reshape(T, D)
    for l in range(depth):
        w_in_bf = jnp.pad(w_in[l], ((0, 0), (0, IN_COLS_PAD - IN_COLS))).astype(BF16)
        b_in_row = jnp.pad(b_in[l], (0, IN_COLS_PAD - IN_COLS)).reshape(1, IN_COLS_PAD)
        h_all, kraw, kbf, vtiles = _in_proj(xf, w_in_bf, b_in_row, B, S)

        y_lru = _rglru(h_all, B, S, conv_w[l], conv_b[l], _block_diag(lru_wa[l]), lru_ba[l],
                       _block_diag(lru_wx[l]), lru_bx[l], lru_lambda[l], gn_lru_g[l])

        w1 = jnp.stack([cmpk_w1[l], cmpv_w1[l]])
        w1_pos = w1.reshape(2, 2, CMP_STRIDE, HEAD_DIM, HEAD_DIM)
        w1l = jnp.concatenate([w1_pos[:, 0], w1_pos[:, 1]], axis=-1)
        pos = jnp.stack([cmp_pos_k[l], cmp_pos_v[l]]).reshape(2, 1, CMP_BLOCK * HEAD_DIM)
        posflat = jnp.broadcast_to(pos, (2, SUBLANES, CMP_BLOCK * HEAD_DIM))
        kcvc = _compress(kraw, w1l, w1, posflat,
                         jnp.stack([cmpk_b1[l], cmpv_b1[l]]).reshape(2, 1, HEAD_DIM),
                         jnp.stack([cmpk_w2[l], cmpv_w2[l]]),
                         jnp.stack([cmpk_b2[l], cmpv_b2[l]]).reshape(2, 1, HEAD_DIM))
        kcvc = kcvc.reshape(2, B, N_KV, S // CMP_STRIDE, HEAD_DIM)
        y_nsa = _nsa(h_all, B, S, kcvc[0], kcvc[1], kbf[0], vtiles[0], kbf[1], vtiles[1], gn_nsa_g[l])

        x1 = _out_ln(y_lru, y_nsa, xf, w_out[l].astype(BF16), ln1_g[l], ln1_b[l], alpha)

        experts, gates = _peer_route(x1, peer_wq[l].astype(BF16), peer_subkeys[l])
        x3 = x1.reshape(T, D_ROWS, LANES)
        wgt = _peer_u(experts, x3, _pack_table(peer_u[l]), gates)
        xf = _peer_v(experts, wgt, x3, _pack_table(peer_v[l]), ln2_g[l], ln2_b[l], alpha).reshape(T, D)
    return xf.reshape(B, S, D)
```

```python
import functools

import numpy as np
import jax
import jax.numpy as jnp
from jax import lax
from jax.experimental import pallas as pl
from jax.experimental.pallas import tpu as pltpu

D_MODEL = 1024
LRU_WIDTH = 512
LRU_BLOCKS = 8
LRU_BLOCK_DIM = LRU_WIDTH // LRU_BLOCKS
CONV_WIDTH = 4
LRU_C = 8.0
N_HEADS = 8
HEAD_DIM = 64
N_KV = 2
Q_PER_KV = N_HEADS // N_KV
NSA_WIDTH = N_HEADS * HEAD_DIM
KV_WIDTH = N_KV * HEAD_DIM
N_BRANCHES = 3
CMP_BLOCK = 32
CMP_STRIDE = 16
SEL_BLOCK = 64
SEL_TOPN = 16
WINDOW = 512
Q_BLOCK = 128
FORCE_BONUS = 1e4
NEG = -1e30
REMOVED = -3.0e38
PEER_HEADS = 8
PEER_KEYS = 128
PEER_TOPK = 16
PEER_KEY_DIM = 128
N_EXPERTS = PEER_KEYS * PEER_KEYS
N_PAIRS = PEER_HEADS * PEER_TOPK
LN_EPS = 1e-5
IN_SPLITS = (LRU_WIDTH, LRU_WIDTH, NSA_WIDTH) + (KV_WIDTH,) * 6 + (N_HEADS * N_BRANCHES,)
IN_COLS = sum(IN_SPLITS)

LANES = 128
SUBLANES = 8
VMEM_LIMIT_BYTES = 56 * 1024 * 1024

IN_COLS_PAD = -(-IN_COLS // LANES) * LANES
COL_LRU_X = 0
COL_LRU_GATE = LRU_WIDTH
COL_Q = 2 * LRU_WIDTH
COL_KV = COL_Q + NSA_WIDTH
COL_GATES = COL_KV + 6 * KV_WIDTH

HI = lax.Precision.HIGHEST
F32 = jnp.float32
BF16 = jnp.bfloat16

D_ROWS = D_MODEL // LANES


def _cparams(sem):
    return pltpu.CompilerParams(dimension_semantics=sem, vmem_limit_bytes=VMEM_LIMIT_BYTES)


def _dot_nt(a, b, precision=None):
    return lax.dot_general(a, b, (((1,), (1,)), ((), ())), precision=precision,
                           preferred_element_type=F32)


def _split2(a):
    hi = a.astype(BF16)
    return hi, (a - hi.astype(F32)).astype(BF16)


def _dot_split(a, b):
    a_hi, a_lo = _split2(a)
    b_hi, b_lo = _split2(b)
    dot = lambda u, v: jnp.dot(u, v, preferred_element_type=F32)
    return dot(a_hi, b_hi) + (dot(a_hi, b_lo) + dot(a_lo, b_hi))


def _layer_norm_rows(z, g, b):
    mu = jnp.mean(z, axis=-1, keepdims=True)
    zc = z - mu
    var = jnp.mean(zc * zc, axis=-1, keepdims=True)
    return zc * lax.rsqrt(var + LN_EPS) * g + b


V_AUG_ROWS = HEAD_DIM + 16


def _in_proj_kernel(x_ref, w_ref, b_ref, o_ref, kraw_ref, kbf_ref, vt_ref):
    h = jnp.dot(x_ref[...].astype(BF16), w_ref[...], preferred_element_type=F32) + b_ref[...]
    o_ref[...] = h
    group_cols = lambda col, g: slice(col + g * HEAD_DIM, col + (g + 1) * HEAD_DIM)
    for j in range(2):
        for g in range(N_KV):
            kraw_ref[j, 0, g] = h[:, group_cols(COL_KV + j * KV_WIDTH, g)]
            kbf_ref[j, 0, g] = h[:, group_cols(COL_KV + (2 + 2 * j) * KV_WIDTH, g)].astype(BF16)
    ones_row = (lax.broadcasted_iota(jnp.int32, (V_AUG_ROWS - HEAD_DIM, Q_BLOCK), 0) == 0).astype(F32)
    for j in range(2):
        col = COL_KV + (3 + 2 * j) * KV_WIDTH
        for qb in range(h.shape[0] // Q_BLOCK):
            v_t = h[qb * Q_BLOCK:(qb + 1) * Q_BLOCK, col:col + KV_WIDTH].T
            for g in range(N_KV):
                tile = jnp.concatenate([v_t[g * HEAD_DIM:(g + 1) * HEAD_DIM, :], ones_row], axis=0)
                vt_ref[j, 0, g, qb] = tile.astype(BF16)


def _in_proj(xf, w_bf, b_row, B, S, tm=512):
    T = xf.shape[0]
    tm = min(tm, S)
    nt = S // tm
    per_tile = lambda *dims: pl.BlockSpec((2, 1, N_KV) + dims, lambda i: (0, i // nt, 0, i % nt) + (0,) * (len(dims) - 1))
    return pl.pallas_call(
        _in_proj_kernel,
        grid=(T // tm,),
        in_specs=[pl.BlockSpec((tm, D_MODEL), lambda i: (i, 0)),
                  pl.BlockSpec((D_MODEL, IN_COLS_PAD), lambda i: (0, 0)),
                  pl.BlockSpec((1, IN_COLS_PAD), lambda i: (0, 0))],
        out_specs=[pl.BlockSpec((tm, IN_COLS_PAD), lambda i: (i, 0)),
                   per_tile(tm, HEAD_DIM), per_tile(tm, HEAD_DIM),
                   per_tile(tm // Q_BLOCK, V_AUG_ROWS, Q_BLOCK)],
        out_shape=[jax.ShapeDtypeStruct((T, IN_COLS_PAD), F32),
                   jax.ShapeDtypeStruct((2, B, N_KV, S, HEAD_DIM), F32),
                   jax.ShapeDtypeStruct((2, B, N_KV, S, HEAD_DIM), BF16),
                   jax.ShapeDtypeStruct((2, B, N_KV, S // Q_BLOCK, V_AUG_ROWS, Q_BLOCK), BF16)],
        compiler_params=_cparams(("arbitrary",)),
        name="in_proj",
    )(xf, w_bf, b_row)


def _rglru_kernel(x_ref, gate_ref, cw_ref, cb_ref, wa_ref, ba_ref, wx_ref, bx_ref, lam_ref, gn_ref,
                  o_ref, xprev_ref, h_ref, *, ts):
    i = pl.program_id(1)

    @pl.when(i == 0)
    def _():
        xprev_ref[...] = jnp.zeros_like(xprev_ref)
        h_ref[...] = jnp.zeros_like(h_ref)

    xb = x_ref[...]
    xp = xprev_ref[...]
    row = lax.broadcasted_iota(jnp.int32, xb.shape, 0)
    xc = cw_ref[CONV_WIDTH - 1:CONV_WIDTH, :] * xb + cb_ref[...]
    for j in range(1, CONV_WIDTH):
        shifted = jnp.where(row >= j, pltpu.roll(xb, j, axis=0), pltpu.roll(xp, j, axis=0))
        xc = xc + cw_ref[CONV_WIDTH - 1 - j:CONV_WIDTH - j, :] * shifted
    xprev_ref[...] = xb

    r = jax.nn.sigmoid(_dot_split(xc, wa_ref[...]) + ba_ref[...])
    ig = jax.nn.sigmoid(_dot_split(xc, wx_ref[...]) + bx_ref[...])
    lam = lam_ref[...]
    softplus_neg_lam = jnp.maximum(-lam, 0.0) + jnp.log1p(jnp.exp(-jnp.abs(lam)))
    log_a = -LRU_C * r * softplus_neg_lam
    a = jnp.exp(log_a)
    th = jnp.tanh(log_a)
    u = jnp.sqrt(-2.0 * th / (1.0 - th)) * (ig * xc)

    d = 1
    while d < ts:
        keep = row >= d
        a_sh = pltpu.roll(a, d, axis=0)
        u_sh = pltpu.roll(u, d, axis=0)
        u = jnp.where(keep, a * u_sh + u, u)
        a = jnp.where(keep, a * a_sh, a)
        d *= 2
    h = u + a * h_ref[0:1, :]
    h_ref[0:1, :] = h[ts - 1:ts, :]

    y = h * jax.nn.gelu(gate_ref[...])
    y = y * lax.rsqrt(jnp.mean(y * y, axis=-1, keepdims=True) + LN_EPS) * gn_ref[...]
    o_ref[...] = y


def _rglru(h_all, B, S, cw, cb, wa_bd, ba, wx_bd, bx, lam, gn, ts=512):
    ts = min(ts, S)
    nt = S // ts
    C = LRU_WIDTH
    row = lambda v: v.reshape(1, C)
    const = lambda shape: pl.BlockSpec(shape, lambda b, i: (0,) * len(shape))
    return pl.pallas_call(
        functools.partial(_rglru_kernel, ts=ts),
        grid=(B, nt),
        in_specs=[pl.BlockSpec((ts, C), lambda b, i: (b * nt + i, COL_LRU_X // C)),
                  pl.BlockSpec((ts, C), lambda b, i: (b * nt + i, COL_LRU_GATE // C)),
                  const((CONV_WIDTH, C)), const((1, C)), const((C, C)), const((1, C)),
                  const((C, C)), const((1, C)), const((1, C)), const((1, C))],
        out_specs=pl.BlockSpec((ts, C), lambda b, i: (b * nt + i, 0)),
        out_shape=jax.ShapeDtypeStruct((B * S, C), F32),
        scratch_shapes=[pltpu.VMEM((ts, C), F32), pltpu.VMEM((SUBLANES, C), F32)],
        compiler_params=_cparams(("arbitrary", "arbitrary")),
        name="rglru",
    )(h_all, h_all, cw, row(cb), wa_bd, row(ba), wx_bd, row(bx), row(lam), row(gn))


def _block_diag(w):
    n, d, e = w.shape
    eye = jnp.eye(n, dtype=w.dtype)
    return (w[:, :, None, :] * eye[:, None, :, None]).reshape(n * d, n * e)


def _compress_kernel(k_ref, w1l_ref, w1_ref, pos_ref, b1_ref, w2_ref, b2_ref, o_ref):
    nr = o_ref.shape[2]
    ab = jnp.zeros((nr, 2 * HEAD_DIM), F32)
    for l in range(CMP_STRIDE):
        k_l = k_ref[0, 0, 0, pl.ds(l, nr, stride=CMP_STRIDE), :]
        ab = ab + jnp.dot(k_l, w1l_ref[0, l], precision=HI, preferred_element_type=F32)
    ab_next = pltpu.roll(ab, nr - 1, axis=0)
    pre = ab[:, :HEAD_DIM] + ab_next[:, HEAD_DIM:]
    posc = jnp.dot(pos_ref[0], w1_ref[0], precision=HI, preferred_element_type=F32)[0:1, :]
    h1 = jax.nn.gelu(pre + posc + b1_ref[0])
    o_ref[0, 0] = jnp.dot(h1, w2_ref[0], precision=HI, preferred_element_type=F32) + b2_ref[0]


def _compress(kraw, w1l, w1, posflat, b1, w2, b2):
    _, B, G, S, _ = kraw.shape
    nr = S // CMP_STRIDE
    fan = CMP_BLOCK * HEAD_DIM
    wspec = lambda shape: pl.BlockSpec((1,) + shape, lambda s, i: (s,) + (0,) * len(shape))
    return pl.pallas_call(
        _compress_kernel,
        grid=(2, B * G),
        in_specs=[pl.BlockSpec((1, 1, 1, S, HEAD_DIM), lambda s, i: (s, i // G, i % G, 0, 0)),
                  wspec((CMP_STRIDE, HEAD_DIM, 2 * HEAD_DIM)), wspec((fan, HEAD_DIM)), wspec((SUBLANES, fan)),
                  wspec((1, HEAD_DIM)), wspec((HEAD_DIM, HEAD_DIM)), wspec((1, HEAD_DIM))],
        out_specs=pl.BlockSpec((1, 1, nr, HEAD_DIM), lambda s, i: (s, i, 0, 0)),
        out_shape=jax.ShapeDtypeStruct((2, B * G, nr, HEAD_DIM), F32),
        compiler_params=_cparams(("arbitrary", "arbitrary")),
        name="compress",
    )(kraw, w1l, w1, posflat, b1, w2, b2)


def _selection_matrix(n_rows, n_cmp, n_sel):
    cs = np.arange(n_rows)[:, None] * CMP_STRIDE
    ss = np.arange(n_sel)[None, :] * SEL_BLOCK
    ov = np.clip(np.minimum(cs + CMP_BLOCK, ss + SEL_BLOCK) - np.maximum(cs, ss), 0, None)
    ov = (ov / CMP_STRIDE).astype(np.float32)
    ov[n_cmp:] = 0.0
    return ov


SEL_TILE = 4 * Q_BLOCK
WIN_SPAN = WINDOW + Q_BLOCK


def _nsa_kernel(q_ref, gt_ref, kc_ref, vct_ref, ks_ref, vst_ref, kw_ref, vwt_ref, slope_ref, bias_ref,
                selmat_ref, emat_ref, gn_ref, o_ref, s_scr, *, n_cmp, n_top, win_span):
    qt = pl.program_id(1)
    q0 = qt * Q_BLOCK
    R = Q_PER_KV
    cols = R * Q_BLOCK
    n_rows_c = kc_ref.shape[2]
    n_sel = selmat_ref.shape[0]
    q_loc = lax.broadcasted_iota(jnp.int32, (1, cols), 1) % Q_BLOCK
    t_col = q0 + q_loc
    gates_t = jax.nn.sigmoid(gt_ref[...]).T

    def step(carry, qs_bf, slope, bias, k_bf, v_tiles, offset, ok):
        m, acc = carry
        s = jnp.where(ok, _dot_nt(k_bf, qs_bf) + bias, NEG)
        c = slope * (-offset).astype(F32)
        m_new = jnp.maximum(m, jnp.max(s, axis=0, keepdims=True) + c)
        p = jnp.exp(s + (c - m_new)).astype(BF16)
        acc = jnp.exp(m - m_new) * acc
        for i, v in enumerate(v_tiles):
            acc = acc + jnp.dot(v, p[i * Q_BLOCK:(i + 1) * Q_BLOCK, :], preferred_element_type=F32)
        return m_new, acc

    init = (jnp.full((1, cols), NEG, F32), jnp.zeros((V_AUG_ROWS, cols), F32))
    finish = lambda acc: acc[0:HEAD_DIM, :] / acc[HEAD_DIM:HEAD_DIM + 1, :]
    y_rows = []
    for g in range(N_KV):
        qg = q_ref[:, g * R * HEAD_DIM:(g + 1) * R * HEAD_DIM] * (HEAD_DIM ** -0.5)
        qs = jnp.concatenate([qg[:, r * HEAD_DIM:(r + 1) * HEAD_DIM] for r in range(R)], axis=0)
        qs_bf = qs.astype(BF16)
        slope = slope_ref[g]

        kc = kc_ref[0, g]
        c_idx = lax.broadcasted_iota(jnp.int32, (n_rows_c, 1), 0)
        cmp_end = c_idx * CMP_STRIDE + (CMP_BLOCK - 1)
        valid_c = (cmp_end <= t_col) & (c_idx < n_cmp)
        q_hi, q_lo = _split2(qs)
        k_hi, k_lo = _split2(kc)
        s = _dot_nt(k_hi, q_hi) + (_dot_nt(k_hi, q_lo) + _dot_nt(k_lo, q_hi))
        s = s - slope * jnp.abs(t_col - cmp_end).astype(F32)
        s = jnp.where(valid_c, s, NEG)
        e = jnp.exp(s - jnp.max(s, axis=0, keepdims=True))
        p_c = e / jnp.sum(e, axis=0, keepdims=True)
        p_c = p_c * (t_col >= CMP_BLOCK - 1).astype(F32)
        o_c = jnp.dot(vct_ref[0, g].astype(BF16), p_c.astype(BF16), preferred_element_type=F32)

        p_sum = p_c[:, 0:Q_BLOCK]
        for r in range(1, R):
            p_sum = p_sum + p_c[:, r * Q_BLOCK:(r + 1) * Q_BLOCK]
        p_hi, p_lo = _split2(p_sum)
        dot = lambda a, b: jnp.dot(a, b, preferred_element_type=F32)
        imp_t = dot(selmat_ref[...], p_hi) + dot(selmat_ref[...], p_lo)
        j_idx = lax.broadcasted_iota(jnp.int32, (n_sel, Q_BLOCK), 0)
        j_f = j_idx.astype(F32)
        cur = t_col[:, 0:Q_BLOCK] // SEL_BLOCK
        forced = (j_idx == 0) | (j_idx == cur) | (j_idx == cur - 1)
        work = jnp.where(j_idx <= cur, imp_t + jnp.where(forced, FORCE_BONUS, 0.0), NEG)
        msel_t = jnp.zeros((n_sel, Q_BLOCK), F32)
        for _ in range(n_top):
            mx = jnp.max(work, axis=0, keepdims=True)
            first = jnp.min(jnp.where(work == mx, j_f, float(LANES)), axis=0, keepdims=True)
            pick = j_f == first
            msel_t = jnp.where(pick, 1.0, msel_t)
            work = jnp.where(pick, REMOVED, work)
        if n_sel < LANES:
            msel_t = jnp.concatenate([msel_t, jnp.zeros((LANES - n_sel, Q_BLOCK), F32)], axis=0)
        msel_bf = msel_t.astype(BF16)

        sel_key = lax.broadcasted_iota(jnp.int32, (SEL_TILE, cols), 0)
        bias_sel = bias_ref[g, 0:SEL_TILE, :]

        def sel_scores(kt, causal):
            k0 = pl.multiple_of(kt * SEL_TILE, SEL_TILE)
            ch = jnp.dot(emat_ref[kt], msel_bf, preferred_element_type=F32)
            ok = jnp.concatenate([ch] * R, axis=1) > 0.5
            if causal:
                ok = ok & (sel_key <= q_loc + (q0 - k0))
            s = jnp.where(ok, _dot_nt(ks_ref[0, g, pl.ds(k0, SEL_TILE), :], qs_bf) + bias_sel, NEG)
            return s, jnp.max(s, axis=0, keepdims=True)

        def sel_consume(m, acc, s_max, kt):
            c = slope * (kt * SEL_TILE - q0).astype(F32)
            m_new = jnp.maximum(m, s_max + c)
            p = jnp.exp(s_scr[...] + (c - m_new)).astype(BF16)
            acc = jnp.exp(m - m_new) * acc
            for i in range(SEL_TILE // Q_BLOCK):
                acc = acc + jnp.dot(vst_ref[0, g, kt * (SEL_TILE // Q_BLOCK) + i],
                                    p[i * Q_BLOCK:(i + 1) * Q_BLOCK, :], preferred_element_type=F32)
            return m_new, acc

        kt_diag = q0 // SEL_TILE
        s_first, max_first = sel_scores(kt_diag, True)
        s_scr[...] = s_first

        def sel_body(kt, lc):
            m, acc, s_max, kt_parked = lc
            s_next, max_next = sel_scores(kt, False)
            m, acc = sel_consume(m, acc, s_max, kt_parked)
            s_scr[...] = s_next
            return m, acc, max_next, kt

        m_s, acc_s, max_last, kt_last = lax.fori_loop(0, kt_diag, sel_body, init + (max_first, kt_diag))
        _, acc_s = sel_consume(m_s, acc_s, max_last, kt_last)
        o_s = finish(acc_s)

        w0 = pl.multiple_of(jnp.maximum(q0 + Q_BLOCK - win_span, 0), Q_BLOCK)
        win_key = lax.broadcasted_iota(jnp.int32, (win_span, cols), 0)
        reach = q_loc + (q0 - w0)
        ok_w = (win_key <= reach) & (win_key > reach - WINDOW)
        v_tiles = [vwt_ref[0, g, w0 // Q_BLOCK + i] for i in range(win_span // Q_BLOCK)]
        carry = step(init, qs_bf, slope, bias_ref[g, 0:win_span, :], kw_ref[0, g, pl.ds(w0, win_span), :],
                     v_tiles, q0 - w0, ok_w)
        o_w = finish(carry[1])

        for r in range(R):
            col = (g * R + r) * N_BRANCHES
            sl = slice(r * Q_BLOCK, (r + 1) * Q_BLOCK)
            y_rows.append(gates_t[col:col + 1, :] * o_c[:, sl] + gates_t[col + 1:col + 2, :] * o_s[:, sl]
                          + gates_t[col + 2:col + 3, :] * o_w[:, sl])
    y_t = jnp.concatenate(y_rows, axis=0)
    y_t = y_t * lax.rsqrt(jnp.mean(y_t * y_t, axis=0, keepdims=True) + LN_EPS) * gn_ref[...]
    o_ref[...] = y_t.T


def _nsa(h_all, B, S, kc, vc, ks_bf, vs_tiles, kw_bf, vw_tiles, gn):
    nq = S // Q_BLOCK
    n_cmp = (S - CMP_BLOCK) // CMP_STRIDE + 1
    n_rows_c = kc.shape[2]
    n_sel = S // SEL_BLOCK
    n_top = min(SEL_TOPN, n_sel)
    n_span = S // SEL_TILE
    win_span = min(WIN_SPAN, S)
    n_bias = max(win_span, SEL_TILE)
    cols = Q_PER_KV * Q_BLOCK
    slopes = np.array([2.0 ** (-8.0 * (h + 1) / N_HEADS) for h in range(N_HEADS)], np.float32)
    slope_cols = np.repeat(slopes.reshape(N_KV, Q_PER_KV), Q_BLOCK, axis=1)[:, None, :]
    rel = (np.arange(cols)[None, :] % Q_BLOCK - np.arange(n_bias)[:, None]).astype(np.float32)
    bias = -slope_cols * rel[None]
    selmat_t = jnp.asarray(_selection_matrix(n_rows_c, n_cmp, n_sel).T, dtype=BF16)
    key_blk = (np.arange(S) // SEL_BLOCK).reshape(n_span, SEL_TILE, 1)
    emat_t = jnp.asarray((key_blk == np.arange(LANES)[None, None, :]).astype(np.float32), dtype=BF16)
    qw = Q_PER_KV * HEAD_DIM * N_KV
    per_batch = lambda *dims: pl.BlockSpec((1,) + dims, lambda b, i: (b,) + (0,) * len(dims))
    const = lambda shape: pl.BlockSpec(shape, lambda b, i: (0,) * len(shape))
    return pl.pallas_call(
        functools.partial(_nsa_kernel, n_cmp=n_cmp, n_top=n_top, win_span=win_span),
        grid=(B, nq),
        in_specs=[pl.BlockSpec((Q_BLOCK, qw), lambda b, i: (b * nq + i, COL_Q // qw)),
                  pl.BlockSpec((Q_BLOCK, LANES), lambda b, i: (b * nq + i, COL_GATES // LANES)),
                  per_batch(N_KV, n_rows_c, HEAD_DIM), per_batch(N_KV, HEAD_DIM, n_rows_c),
                  per_batch(N_KV, S, HEAD_DIM), per_batch(N_KV, nq, V_AUG_ROWS, Q_BLOCK),
                  per_batch(N_KV, S, HEAD_DIM), per_batch(N_KV, nq, V_AUG_ROWS, Q_BLOCK),
                  const((N_KV, 1, cols)), const((N_KV, n_bias, cols)), const((n_sel, n_rows_c)),
                  const((n_span, SEL_TILE, LANES)), const((NSA_WIDTH, 1))],
        out_specs=pl.BlockSpec((Q_BLOCK, NSA_WIDTH), lambda b, i: (b * nq + i, 0)),
        out_shape=jax.ShapeDtypeStruct((B * S, NSA_WIDTH), F32),
        scratch_shapes=[pltpu.VMEM((SEL_TILE, cols), F32)],
        compiler_params=_cparams(("arbitrary", "arbitrary")),
        name="nsa",
    )(h_all, h_all, kc, vc.transpose(0, 1, 3, 2), ks_bf, vs_tiles, kw_bf, vw_tiles,
      jnp.asarray(slope_cols), jnp.asarray(bias), selmat_t, emat_t, gn.reshape(NSA_WIDTH, 1))


def _out_ln_kernel(ylru_ref, ynsa_ref, x_ref, w_ref, g_ref, b_ref, o_ref, *, alpha):
    mix = jnp.dot(ylru_ref[...].astype(BF16), w_ref[0:LRU_WIDTH, :], preferred_element_type=F32)
    mix = mix + jnp.dot(ynsa_ref[...].astype(BF16), w_ref[LRU_WIDTH:, :], preferred_element_type=F32)
    o_ref[...] = _layer_norm_rows(alpha * x_ref[...] + mix, g_ref[...], b_ref[...])


def _out_ln(ylru, ynsa, xf, w_bf, g, b, alpha, tm=512):
    T = xf.shape[0]
    const = lambda shape: pl.BlockSpec(shape, lambda i: (0,) * len(shape))
    return pl.pallas_call(
        functools.partial(_out_ln_kernel, alpha=alpha),
        grid=(T // tm,),
        in_specs=[pl.BlockSpec((tm, LRU_WIDTH), lambda i: (i, 0)),
                  pl.BlockSpec((tm, NSA_WIDTH), lambda i: (i, 0)),
                  pl.BlockSpec((tm, D_MODEL), lambda i: (i, 0)),
                  const((D_MODEL, D_MODEL)), const((1, D_MODEL)), const((1, D_MODEL))],
        out_specs=pl.BlockSpec((tm, D_MODEL), lambda i: (i, 0)),
        out_shape=jax.ShapeDtypeStruct((T, D_MODEL), F32),
        compiler_params=_cparams(("arbitrary",)),
        name="out_ln",
    )(ylru, ynsa, xf, w_bf, g.reshape(1, D_MODEL), b.reshape(1, D_MODEL))


def _topk_rows(vals, order, payload, n_keep):
    n = vals.shape[1]
    kidx = lax.broadcasted_iota(jnp.int32, (n_keep, n), 0)
    never = float(1 << 20)
    kept_v = jnp.zeros((n_keep, n), F32)
    kept_p = jnp.zeros((n_keep, n), F32)
    work = vals
    for r in range(n_keep):
        mx = jnp.max(work, axis=0, keepdims=True)
        first = jnp.min(jnp.where(work == mx, order, never), axis=0, keepdims=True)
        pick = order == first
        kept_v = jnp.where(kidx == r, mx, kept_v)
        if payload is None:
            kept_p = jnp.where(kidx == r, first, kept_p)
        else:
            kept_p = jnp.where(kidx == r, jnp.max(jnp.where(pick, payload, -1.0), axis=0, keepdims=True), kept_p)
        work = jnp.where(pick, REMOVED, work)
    return kept_v, kept_p


def _candidate_rows():
    ij = [(i, 0) for i in range(PEER_TOPK)]
    for j in range(1, SUBLANES):
        ij += [(i, j) for i in range(SUBLANES)]
    ij += [(0, j) for j in range(SUBLANES, PEER_TOPK)]
    flat = np.array([i * PEER_TOPK + j for i, j in ij], np.float32)
    valid = np.array([(i + 1) * (j + 1) <= PEER_TOPK for i, j in ij])
    return flat[:, None], np.where(valid, 0.0, REMOVED).astype(np.float32)[:, None]


def _candidates(a0, a1):
    return ([(a0, a1[0:1, :])]
            + [(a0[0:SUBLANES, :], a1[j:j + 1, :]) for j in range(1, SUBLANES)]
            + [(a0[0:1, :], a1[SUBLANES:PEER_TOPK, :])])


def _peer_route_kernel(x_ref, wq_ref, sk_ref, flat_ref, pad_ref, e_ref, g_ref):
    tm = x_ref.shape[0]
    q = jnp.dot(x_ref[...].astype(BF16), wq_ref[...], preferred_element_type=F32)
    key_idx = lax.broadcasted_iota(jnp.int32, (PEER_KEYS, tm), 0).astype(F32)
    flat = jnp.broadcast_to(flat_ref[...], (flat_ref.shape[0], tm))
    experts, gates = [], []
    for h in range(PEER_HEADS):
        sv, si = [], []
        for c in range(2):
            col = (h * 2 + c) * PEER_KEY_DIM
            s_t = _dot_nt(sk_ref[c], q[:, col:col + PEER_KEY_DIM], HI)
            v, i = _topk_rows(s_t, key_idx, None, PEER_TOPK)
            sv.append(v)
            si.append(i)
        cand = jnp.concatenate([a + b for a, b in _candidates(sv[0], sv[1])], axis=0) + pad_ref[...]
        ecand = jnp.concatenate([a * float(PEER_KEYS) + b for a, b in _candidates(si[0], si[1])], axis=0)
        cv, ce = _topk_rows(cand, flat, ecand, PEER_TOPK)
        ex = jnp.exp(cv - cv[0:1, :])
        gates.append(ex / jnp.sum(ex, axis=0, keepdims=True))
        experts.append(ce)
    e_ref[...] = jnp.concatenate(experts, axis=0).T.astype(jnp.int32)
    g_ref[...] = jnp.concatenate(gates, axis=0).T


def _peer_route(x1, wq_bf, subkeys, tm=256):
    T = x1.shape[0]
    tm = min(tm, T)
    nq = wq_bf.shape[1]
    flat, pad = _candidate_rows()
    n_cand = flat.shape[0]
    out = lambda: pl.BlockSpec((tm, N_PAIRS), lambda i: (i, 0))
    return pl.pallas_call(
        _peer_route_kernel,
        grid=(T // tm,),
        in_specs=[pl.BlockSpec((tm, D_MODEL), lambda i: (i, 0)),
                  pl.BlockSpec((D_MODEL, nq), lambda i: (0, 0)),
                  pl.BlockSpec((2, PEER_KEYS, PEER_KEY_DIM), lambda i: (0, 0, 0)),
                  pl.BlockSpec((n_cand, 1), lambda i: (0, 0)),
                  pl.BlockSpec((n_cand, 1), lambda i: (0, 0))],
        out_specs=[out(), out()],
        out_shape=[jax.ShapeDtypeStruct((T, N_PAIRS), jnp.int32),
                   jax.ShapeDtypeStruct((T, N_PAIRS), F32)],
        compiler_params=_cparams(("arbitrary",)),
        name="peer_route",
    )(x1, wq_bf, subkeys, jnp.asarray(flat), jnp.asarray(pad))


TOKENS_PER_MATMUL = 8


def _pack_table(tab):
    return tab.astype(BF16).reshape(N_EXPERTS, D_ROWS, LANES)


def _sublane_sums(ps, roll, where, sub):
    lvl = ps
    for r in (1, 2, 4):
        m = (sub % (2 * r)) < r
        half = len(lvl) // 2
        lvl = [where(m, lvl[i], lvl[i + half]) + roll(where(m, lvl[i + half], lvl[i]), r) for i in range(half)]
    return lvl[0]


def _sum_src():
    sub = np.arange(SUBLANES)[:, None] * np.ones((1, LANES), np.int64)
    ps = [np.full((SUBLANES, LANES), 10.0 ** i) for i in range(SUBLANES)]
    out = _sublane_sums(ps, lambda v, r: np.roll(v, r, axis=0), np.where, sub)
    return [int(round(np.log10(out[s, 0] / SUBLANES))) for s in range(SUBLANES)]


_SUM_SRC = _sum_src()


def _peer_u_kernel(e_ref, x_ref, tab_ref, gate_ref, w_ref, part_ref, zt_ref, *, tb):
    sub = lax.broadcasted_iota(jnp.int32, (SUBLANES, LANES), 0)
    lane = lax.broadcasted_iota(jnp.int32, (N_PAIRS, tb), 1)

    def gather_dots(t, slot):
        xt = x_ref[t]
        for grp in range(N_PAIRS // SUBLANES):
            prods = [None] * SUBLANES
            for s in range(SUBLANES):
                prods[_SUM_SRC[s]] = tab_ref[e_ref[t * N_PAIRS + grp * SUBLANES + s]].astype(F32) * xt
            part_ref[slot, grp * SUBLANES:(grp + 1) * SUBLANES, :] = _sublane_sums(
                prods, lambda v, r: pltpu.roll(v, r, axis=0), jnp.where, sub)

    def lane_sums(t0):
        z0 = jnp.sum(part_ref[0], axis=1, keepdims=True)
        z1 = jnp.sum(part_ref[1], axis=1, keepdims=True)
        zt_ref[...] = jnp.where(lane == t0, z0, jnp.where(lane == t0 + 1, z1, zt_ref[...]))

    part_ref[...] = jnp.zeros_like(part_ref)
    zt_ref[...] = jnp.zeros_like(zt_ref)

    def two_tokens(i, carry):
        lane_sums(2 * i - 2)
        gather_dots(2 * i, 0)
        gather_dots(2 * i + 1, 1)
        return carry

    lax.fori_loop(0, tb // 2, two_tokens, 0)
    lane_sums(tb - 2)
    w_ref[...] = jax.nn.gelu(zt_ref[...].T) * gate_ref[...]


def _peer_u(experts, x3, tab, gates, tb=128):
    T = x3.shape[0]
    vm = lambda: pl.BlockSpec((tb, N_PAIRS), lambda i: (i, 0))
    return pl.pallas_call(
        functools.partial(_peer_u_kernel, tb=tb),
        grid=(T // tb,),
        in_specs=[pl.BlockSpec((tb * N_PAIRS,), lambda i: (i,), memory_space=pltpu.SMEM),
                  pl.BlockSpec((tb, D_ROWS, LANES), lambda i: (i, 0, 0)),
                  pl.BlockSpec((N_EXPERTS, D_ROWS, LANES), lambda i: (0, 0, 0), pipeline_mode=pl.Buffered(1)),
                  vm()],
        out_specs=vm(),
        out_shape=jax.ShapeDtypeStruct((T, N_PAIRS), F32),
        scratch_shapes=[pltpu.VMEM((2, N_PAIRS, LANES), F32), pltpu.VMEM((N_PAIRS, tb), F32)],
        compiler_params=_cparams(("arbitrary",)),
        name="peer_u",
    )(experts.reshape(T * N_PAIRS), x3, tab, gates)


def _expand_matrix():
    k = np.arange(N_PAIRS)[:, None]
    c = np.arange(N_PAIRS * D_ROWS)[None, :]
    return (c // D_ROWS == k).astype(np.float32)


def _peer_v_kernel(e_ref, w_ref, x_ref, tab_ref, expand_ref, g_ref, b_ref, o_ref, y_ref, wide_ref, *, tb, alpha):
    n_col = N_PAIRS * D_ROWS
    diag = (lax.broadcasted_iota(jnp.int32, (D_ROWS, n_col), 1) % D_ROWS
            == lax.broadcasted_iota(jnp.int32, (D_ROWS, n_col), 0))
    part_row = lax.broadcasted_iota(jnp.int32, (SUBLANES, LANES), 0)

    def expand_tokens(i, carry):
        t0 = pl.multiple_of(i * TOKENS_PER_MATMUL, TOKENS_PER_MATMUL)
        parts = []
        for j in range(TOKENS_PER_MATMUL):
            wrow = w_ref[pl.ds(t0 + j, 1), :]
            hi = wrow.astype(BF16).astype(F32)
            mid = (wrow - hi).astype(BF16).astype(F32)
            lo = ((wrow - hi) - mid).astype(BF16).astype(F32)
            bc = lambda v: jnp.broadcast_to(v, (SUBLANES, LANES))
            parts.append(jnp.where(part_row == 0, bc(hi), jnp.where(part_row == 1, bc(mid),
                                                                    jnp.where(part_row == 2, bc(lo), 0.0))))
        lhs = jnp.concatenate(parts, axis=0).astype(BF16)
        wide = jnp.dot(lhs, expand_ref[...], preferred_element_type=F32)
        wide_ref[pl.ds(t0, TOKENS_PER_MATMUL)] = wide.reshape(TOKENS_PER_MATMUL, SUBLANES, n_col)
        return carry

    lax.fori_loop(0, tb // TOKENS_PER_MATMUL, expand_tokens, 0)

    def left_operand(t):
        rows = [jnp.where(diag, jnp.broadcast_to(wide_ref[t, p:p + 1, :], (D_ROWS, n_col)), 0.0) for p in range(3)]
        return jnp.concatenate(rows, axis=0).astype(BF16)

    def weighted_sum(t, a):
        rows = [tab_ref[e_ref[t * N_PAIRS + k]].astype(F32) for k in range(N_PAIRS)]
        rhs = jnp.concatenate(rows, axis=0).astype(BF16)
        y3 = jnp.dot(a, rhs, preferred_element_type=F32)
        return (y3[0:D_ROWS] + y3[D_ROWS:2 * D_ROWS]) + y3[2 * D_ROWS:]

    tokens_per_step = 8

    def some_tokens(i, carry):
        ts = [tokens_per_step * i + j for j in range(tokens_per_step)]
        ys = [weighted_sum(t, left_operand(t)) for t in ts]
        for t, y in zip(ts, ys):
            y_ref[t] = y
        return carry

    lax.fori_loop(0, tb // tokens_per_step, some_tokens, 0)
    z = alpha * x_ref[...] + y_ref[...]
    inv_d = 1.0 / D_MODEL
    mu = jnp.sum(jnp.sum(z, axis=2, keepdims=True), axis=1, keepdims=True) * inv_d
    zc = z - mu
    var = jnp.sum(jnp.sum(zc * zc, axis=2, keepdims=True), axis=1, keepdims=True) * inv_d
    o_ref[...] = zc * lax.rsqrt(var + LN_EPS) * g_ref[...] + b_ref[...]


def _peer_v(experts, wgt, x3, tab, g, b, alpha, tb=128):
    T = x3.shape[0]
    return pl.pallas_call(
        functools.partial(_peer_v_kernel, tb=tb, alpha=alpha),
        grid=(T // tb,),
        in_specs=[pl.BlockSpec((tb * N_PAIRS,), lambda i: (i,), memory_space=pltpu.SMEM),
                  pl.BlockSpec((tb, N_PAIRS), lambda i: (i, 0)),
                  pl.BlockSpec((tb, D_ROWS, LANES), lambda i: (i, 0, 0)),
                  pl.BlockSpec((N_EXPERTS, D_ROWS, LANES), lambda i: (0, 0, 0), pipeline_mode=pl.Buffered(1)),
                  pl.BlockSpec((N_PAIRS, N_PAIRS * D_ROWS), lambda i: (0, 0)),
                  pl.BlockSpec((1, D_ROWS, LANES), lambda i: (0, 0, 0)),
                  pl.BlockSpec((1, D_ROWS, LANES), lambda i: (0, 0, 0))],
        out_specs=pl.BlockSpec((tb, D_ROWS, LANES), lambda i: (i, 0, 0)),
        out_shape=jax.ShapeDtypeStruct((T, D_ROWS, LANES), F32),
        scratch_shapes=[pltpu.VMEM((tb, D_ROWS, LANES), F32), pltpu.VMEM((tb, SUBLANES, N_PAIRS * D_ROWS), F32)],
        compiler_params=_cparams(("arbitrary",)),
        name="peer_v",
    )(experts.reshape(T * N_PAIRS), wgt, x3, tab, jnp.asarray(_expand_matrix(), dtype=BF16),
      g.reshape(1, D_ROWS, LANES), b.reshape(1, D_ROWS, LANES))


def kernel(x, w_in, b_in, conv_w, conv_b, lru_wa, lru_ba, lru_wx, lru_bx, lru_lambda, cmp_pos_k, cmpk_w1, cmpk_b1, cmpk_w2, cmpk_b2, cmp_pos_v, cmpv_w1, cmpv_b1, cmpv_w2, cmpv_b2, gn_lru_g, gn_nsa_g, w_out, ln1_g, ln1_b, peer_wq, peer_subkeys, peer_u, peer_v, ln2_g, ln2_b):
    B, S, D = x.shape
    T = B * S
    depth = w_in.shape[0]
    alpha = (2 * depth) ** 0.25
    xf = x.reshape(T, D)
    for l in range(depth):
        w_in_bf = jnp.pad(w_in[l], ((0, 0), (0, IN_COLS_PAD - IN_COLS))).astype(BF16)
        b_in_row = jnp.pad(b_in[l], (0, IN_COLS_PAD - IN_COLS)).reshape(1, IN_COLS_PAD)
        h_all, kraw, kbf, vtiles = _in_proj(xf, w_in_bf, b_in_row, B, S)

        y_lru = _rglru(h_all, B, S, conv_w[l], conv_b[l], _block_diag(lru_wa[l]), lru_ba[l],
                       _block_diag(lru_wx[l]), lru_bx[l], lru_lambda[l], gn_lru_g[l])

        w1 = jnp.stack([cmpk_w1[l], cmpv_w1[l]])
        w1_pos = w1.reshape(2, 2, CMP_STRIDE, HEAD_DIM, HEAD_DIM)
        w1l = jnp.concatenate([w1_pos[:, 0], w1_pos[:, 1]], axis=-1)
        pos = jnp.stack([cmp_pos_k[l], cmp_pos_v[l]]).reshape(2, 1, CMP_BLOCK * HEAD_DIM)
        posflat = jnp.broadcast_to(pos, (2, SUBLANES, CMP_BLOCK * HEAD_DIM))
        kcvc = _compress(kraw, w1l, w1, posflat,
                         jnp.stack([cmpk_b1[l], cmpv_b1[l]]).reshape(2, 1, HEAD_DIM),
                         jnp.stack([cmpk_w2[l], cmpv_w2[l]]),
                         jnp.stack([cmpk_b2[l], cmpv_b2[l]]).reshape(2, 1, HEAD_DIM))
        kcvc = kcvc.reshape(2, B, N_KV, S // CMP_STRIDE, HEAD_DIM)
        y_nsa = _nsa(h_all, B, S, kcvc[0], kcvc[1], kbf[0], vtiles[0], kbf[1], vtiles[1], gn_nsa_g[l])

        x1 = _out_ln(y_lru, y_nsa, xf, w_out[l].astype(BF16), ln1_g[l], ln1_b[l], alpha)

        experts, gates = _peer_route(x1, peer_wq[l].astype(BF16), peer_subkeys[l])
        x3 = x1.reshape(T, D_ROWS, LANES)
        wgt = _peer_u(experts, x3, _pack_table(peer_u[l]), gates)
        xf = _peer_v(experts, wgt, x3, _pack_table(peer_v[l]), ln2_g[l], ln2_b[l], alpha).reshape(T, D)
    return xf.reshape(B, S, D)
```

```python
import functools

import numpy as np
import jax
import jax.numpy as jnp
from jax import lax
from jax.experimental import pallas as pl
from jax.experimental.pallas import tpu as pltpu

D_MODEL = 1024
LRU_WIDTH = 512
LRU_BLOCKS = 8
LRU_BLOCK_DIM = LRU_WIDTH // LRU_BLOCKS
CONV_WIDTH = 4
LRU_C = 8.0
N_HEADS = 8
HEAD_DIM = 64
N_KV = 2
Q_PER_KV = N_HEADS // N_KV
NSA_WIDTH = N_HEADS * HEAD_DIM
KV_WIDTH = N_KV * HEAD_DIM
N_BRANCHES = 3
CMP_BLOCK = 32
CMP_STRIDE = 16
SEL_BLOCK = 64
SEL_TOPN = 16
WINDOW = 512
Q_BLOCK = 128
FORCE_BONUS = 1e4
NEG = -1e30
REMOVED = -3.0e38
PEER_HEADS = 8
PEER_KEYS = 128
PEER_TOPK = 16
PEER_KEY_DIM = 128
N_EXPERTS = PEER_KEYS * PEER_KEYS
N_PAIRS = PEER_HEADS * PEER_TOPK
LN_EPS = 1e-5
IN_SPLITS = (LRU_WIDTH, LRU_WIDTH, NSA_WIDTH) + (KV_WIDTH,) * 6 + (N_HEADS * N_BRANCHES,)
IN_COLS = sum(IN_SPLITS)

LANES = 128
SUBLANES = 8
VMEM_LIMIT_BYTES = 56 * 1024 * 1024

IN_COLS_PAD = -(-IN_COLS // LANES) * LANES
COL_LRU_X = 0
COL_LRU_GATE = LRU_WIDTH
COL_Q = 2 * LRU_WIDTH
COL_KV = COL_Q + NSA_WIDTH
COL_GATES = COL_KV + 6 * KV_WIDTH

HI = lax.Precision.HIGHEST
F32 = jnp.float32
BF16 = jnp.bfloat16

D_ROWS = D_MODEL // LANES


def _cparams(sem):
    return pltpu.CompilerParams(dimension_semantics=sem, vmem_limit_bytes=VMEM_LIMIT_BYTES)


def _dot_nt(a, b, precision=None):
    return lax.dot_general(a, b, (((1,), (1,)), ((), ())), precision=precision,
                           preferred_element_type=F32)


def _split2(a):
    hi = a.astype(BF16)
    return hi, (a - hi.astype(F32)).astype(BF16)


def _dot_split(a, b):
    a_hi, a_lo = _split2(a)
    b_hi, b_lo = _split2(b)
    dot = lambda u, v: jnp.dot(u, v, preferred_element_type=F32)
    return dot(a_hi, b_hi) + (dot(a_hi, b_lo) + dot(a_lo, b_hi))


def _layer_norm_rows(z, g, b):
    mu = jnp.mean(z, axis=-1, keepdims=True)
    zc = z - mu
    var = jnp.mean(zc * zc, axis=-1, keepdims=True)
    return zc * lax.rsqrt(var + LN_EPS) * g + b


V_AUG_ROWS = HEAD_DIM + 16


def _in_proj_kernel(x_ref, w_ref, b_ref, o_ref, kraw_ref, kbf_ref, vt_ref):
    h = jnp.dot(x_ref[...].astype(BF16), w_ref[...], preferred_element_type=F32) + b_ref[...]
    o_ref[...] = h
    group_cols = lambda col, g: slice(col + g * HEAD_DIM, col + (g + 1) * HEAD_DIM)
    for j in range(2):
        for g in range(N_KV):
            kraw_ref[j, 0, g] = h[:, group_cols(COL_KV + j * KV_WIDTH, g)]
            kbf_ref[j, 0, g] = h[:, group_cols(COL_KV + (2 + 2 * j) * KV_WIDTH, g)].astype(BF16)
    ones_row = (lax.broadcasted_iota(jnp.int32, (V_AUG_ROWS - HEAD_DIM, Q_BLOCK), 0) == 0).astype(F32)
    for j in range(2):
        col = COL_KV + (3 + 2 * j) * KV_WIDTH
        for qb in range(h.shape[0] // Q_BLOCK):
            v_t = h[qb * Q_BLOCK:(qb + 1) * Q_BLOCK, col:col + KV_WIDTH].T
            for g in range(N_KV):
                tile = jnp.concatenate([v_t[g * HEAD_DIM:(g + 1) * HEAD_DIM, :], ones_row], axis=0)
                vt_ref[j, 0, g, qb] = tile.astype(BF16)


def _in_proj(xf, w_bf, b_row, B, S, tm=512):
    T = xf.shape[0]
    tm = min(tm, S)
    nt = S // tm
    per_tile = lambda *dims: pl.BlockSpec((2, 1, N_KV) + dims, lambda i: (0, i // nt, 0, i % nt) + (0,) * (len(dims) - 1))
    return pl.pallas_call(
        _in_proj_kernel,
        grid=(T // tm,),
        in_specs=[pl.BlockSpec((tm, D_MODEL), lambda i: (i, 0)),
                  pl.BlockSpec((D_MODEL, IN_COLS_PAD), lambda i: (0, 0)),
                  pl.BlockSpec((1, IN_COLS_PAD), lambda i: (0, 0))],
        out_specs=[pl.BlockSpec((tm, IN_COLS_PAD), lambda i: (i, 0)),
                   per_tile(tm, HEAD_DIM), per_tile(tm, HEAD_DIM),
                   per_tile(tm // Q_BLOCK, V_AUG_ROWS, Q_BLOCK)],
        out_shape=[jax.ShapeDtypeStruct((T, IN_COLS_PAD), F32),
                   jax.ShapeDtypeStruct((2, B, N_KV, S, HEAD_DIM), F32),
                   jax.ShapeDtypeStruct((2, B, N_KV, S, HEAD_DIM), BF16),
                   jax.ShapeDtypeStruct((2, B, N_KV, S // Q_BLOCK, V_AUG_ROWS, Q_BLOCK), BF16)],
        compiler_params=_cparams(("arbitrary",)),
        name="in_proj",
    )(xf, w_bf, b_row)


def _rglru_kernel(x_ref, gate_ref, cw_ref, cb_ref, wa_ref, ba_ref, wx_ref, bx_ref, lam_ref, gn_ref,
                  o_ref, xprev_ref, h_ref, *, ts):
    i = pl.program_id(1)

    @pl.when(i == 0)
    def _():
        xprev_ref[...] = jnp.zeros_like(xprev_ref)
        h_ref[...] = jnp.zeros_like(h_ref)

    xb = x_ref[...]
    xp = xprev_ref[...]
    row = lax.broadcasted_iota(jnp.int32, xb.shape, 0)
    xc = cw_ref[CONV_WIDTH - 1:CONV_WIDTH, :] * xb + cb_ref[...]
    for j in range(1, CONV_WIDTH):
        shifted = jnp.where(row >= j, pltpu.roll(xb, j, axis=0), pltpu.roll(xp, j, axis=0))
        xc = xc + cw_ref[CONV_WIDTH - 1 - j:CONV_WIDTH - j, :] * shifted
    xprev_ref[...] = xb

    r = jax.nn.sigmoid(_dot_split(xc, wa_ref[...]) + ba_ref[...])
    ig = jax.nn.sigmoid(_dot_split(xc, wx_ref[...]) + bx_ref[...])
    lam = lam_ref[...]
    softplus_neg_lam = jnp.maximum(-lam, 0.0) + jnp.log1p(jnp.exp(-jnp.abs(lam)))
    log_a = -LRU_C * r * softplus_neg_lam
    a = jnp.exp(log_a)
    th = jnp.tanh(log_a)
    u = jnp.sqrt(-2.0 * th / (1.0 - th)) * (ig * xc)

    d = 1
    while d < ts:
        keep = row >= d
        a_sh = pltpu.roll(a, d, axis=0)
        u_sh = pltpu.roll(u, d, axis=0)
        u = jnp.where(keep, a * u_sh + u, u)
        a = jnp.where(keep, a * a_sh, a)
        d *= 2
    h = u + a * h_ref[0:1, :]
    h_ref[0:1, :] = h[ts - 1:ts, :]

    y = h * jax.nn.gelu(gate_ref[...])
    y = y * lax.rsqrt(jnp.mean(y * y, axis=-1, keepdims=True) + LN_EPS) * gn_ref[...]
    o_ref[...] = y


def _rglru(h_all, B, S, cw, cb, wa_bd, ba, wx_bd, bx, lam, gn, ts=512):
    ts = min(ts, S)
    nt = S // ts
    C = LRU_WIDTH
    row = lambda v: v.reshape(1, C)
    const = lambda shape: pl.BlockSpec(shape, lambda b, i: (0,) * len(shape))
    return pl.pallas_call(
        functools.partial(_rglru_kernel, ts=ts),
        grid=(B, nt),
        in_specs=[pl.BlockSpec((ts, C), lambda b, i: (b * nt + i, COL_LRU_X // C)),
                  pl.BlockSpec((ts, C), lambda b, i: (b * nt + i, COL_LRU_GATE // C)),
                  const((CONV_WIDTH, C)), const((1, C)), const((C, C)), const((1, C)),
                  const((C, C)), const((1, C)), const((1, C)), const((1, C))],
        out_specs=pl.BlockSpec((ts, C), lambda b, i: (b * nt + i, 0)),
        out_shape=jax.ShapeDtypeStruct((B * S, C), F32),
        scratch_shapes=[pltpu.VMEM((ts, C), F32), pltpu.VMEM((SUBLANES, C), F32)],
        compiler_params=_cparams(("arbitrary", "arbitrary")),
        name="rglru",
    )(h_all, h_all, cw, row(cb), wa_bd, row(ba), wx_bd, row(bx), row(lam), row(gn))


def _block_diag(w):
    n, d, e = w.shape
    eye = jnp.eye(n, dtype=w.dtype)
    return (w[:, :, None, :] * eye[:, None, :, None]).reshape(n * d, n * e)


def _compress_kernel(k_ref, w1l_ref, w1_ref, pos_ref, b1_ref, w2_ref, b2_ref, o_ref):
    nr = o_ref.shape[2]
    ab = jnp.zeros((nr, 2 * HEAD_DIM), F32)
    for l in range(CMP_STRIDE):
        k_l = k_ref[0, 0, 0, pl.ds(l, nr, stride=CMP_STRIDE), :]
        ab = ab + jnp.dot(k_l, w1l_ref[0, l], precision=HI, preferred_element_type=F32)
    ab_next = pltpu.roll(ab, nr - 1, axis=0)
    pre = ab[:, :HEAD_DIM] + ab_next[:, HEAD_DIM:]
    posc = jnp.dot(pos_ref[0], w1_ref[0], precision=HI, preferred_element_type=F32)[0:1, :]
    h1 = jax.nn.gelu(pre + posc + b1_ref[0])
    o_ref[0, 0] = jnp.dot(h1, w2_ref[0], precision=HI, preferred_element_type=F32) + b2_ref[0]


def _compress(kraw, w1l, w1, posflat, b1, w2, b2):
    _, B, G, S, _ = kraw.shape
    nr = S // CMP_STRIDE
    fan = CMP_BLOCK * HEAD_DIM
    wspec = lambda shape: pl.BlockSpec((1,) + shape, lambda s, i: (s,) + (0,) * len(shape))
    return pl.pallas_call(
        _compress_kernel,
        grid=(2, B * G),
        in_specs=[pl.BlockSpec((1, 1, 1, S, HEAD_DIM), lambda s, i: (s, i // G, i % G, 0, 0)),
                  wspec((CMP_STRIDE, HEAD_DIM, 2 * HEAD_DIM)), wspec((fan, HEAD_DIM)), wspec((SUBLANES, fan)),
                  wspec((1, HEAD_DIM)), wspec((HEAD_DIM, HEAD_DIM)), wspec((1, HEAD_DIM))],
        out_specs=pl.BlockSpec((1, 1, nr, HEAD_DIM), lambda s, i: (s, i, 0, 0)),
        out_shape=jax.ShapeDtypeStruct((2, B * G, nr, HEAD_DIM), F32),
        compiler_params=_cparams(("arbitrary", "arbitrary")),
        name="compress",
    )(kraw, w1l, w1, posflat, b1, w2, b2)


def _selection_matrix(n_rows, n_cmp, n_sel):
    cs = np.arange(n_rows)[:, None] * CMP_STRIDE
    ss = np.arange(n_sel)[None, :] * SEL_BLOCK
    ov = np.clip(np.minimum(cs + CMP_BLOCK, ss + SEL_BLOCK) - np.maximum(cs, ss), 0, None)
    ov = (ov / CMP_STRIDE).astype(np.float32)
    ov[n_cmp:] = 0.0
    return ov


SEL_TILE = 4 * Q_BLOCK
WIN_SPAN = WINDOW + Q_BLOCK


def _nsa_kernel(q_ref, gt_ref, kc_ref, vct_ref, ks_ref, vst_ref, kw_ref, vwt_ref, slope_ref, bias_ref,
                selmat_ref, emat_ref, gn_ref, o_ref, s_scr, *, n_cmp, n_top, win_span):
    qt = pl.program_id(1)
    q0 = qt * Q_BLOCK
    R = Q_PER_KV
    cols = R * Q_BLOCK
    n_rows_c = kc_ref.shape[2]
    n_sel = selmat_ref.shape[0]
    q_loc = lax.broadcasted_iota(jnp.int32, (1, cols), 1) % Q_BLOCK
    t_col = q0 + q_loc
    gates_t = jax.nn.sigmoid(gt_ref[...]).T

    def step(carry, qs_bf, slope, bias, k_bf, v_tiles, offset, ok):
        m, acc = carry
        s = jnp.where(ok, _dot_nt(k_bf, qs_bf) + bias, NEG)
        c = slope * (-offset).astype(F32)
        m_new = jnp.maximum(m, jnp.max(s, axis=0, keepdims=True) + c)
        p = jnp.exp(s + (c - m_new)).astype(BF16)
        acc = jnp.exp(m - m_new) * acc
        for i, v in enumerate(v_tiles):
            acc = acc + jnp.dot(v, p[i * Q_BLOCK:(i + 1) * Q_BLOCK, :], preferred_element_type=F32)
        return m_new, acc

    init = (jnp.full((1, cols), NEG, F32), jnp.zeros((V_AUG_ROWS, cols), F32))
    finish = lambda acc: acc[0:HEAD_DIM, :] / acc[HEAD_DIM:HEAD_DIM + 1, :]
    y_rows = []
    for g in range(N_KV):
        qg = q_ref[:, g * R * HEAD_DIM:(g + 1) * R * HEAD_DIM] * (HEAD_DIM ** -0.5)
        qs = jnp.concatenate([qg[:, r * HEAD_DIM:(r + 1) * HEAD_DIM] for r in range(R)], axis=0)
        qs_bf = qs.astype(BF16)
        slope = slope_ref[g]

        kc = kc_ref[0, g]
        c_idx = lax.broadcasted_iota(jnp.int32, (n_rows_c, 1), 0)
        cmp_end = c_idx * CMP_STRIDE + (CMP_BLOCK - 1)
        valid_c = (cmp_end <= t_col) & (c_idx < n_cmp)
        q_hi, q_lo = _split2(qs)
        k_hi, k_lo = _split2(kc)
        s = _dot_nt(k_hi, q_hi) + (_dot_nt(k_hi, q_lo) + _dot_nt(k_lo, q_hi))
        s = s - slope * jnp.abs(t_col - cmp_end).astype(F32)
        s = jnp.where(valid_c, s, NEG)
        e = jnp.exp(s - jnp.max(s, axis=0, keepdims=True))
        p_c = e / jnp.sum(e, axis=0, keepdims=True)
        p_c = p_c * (t_col >= CMP_BLOCK - 1).astype(F32)
        o_c = jnp.dot(vct_ref[0, g].astype(BF16), p_c.astype(BF16), preferred_element_type=F32)

        p_sum = p_c[:, 0:Q_BLOCK]
        for r in range(1, R):
            p_sum = p_sum + p_c[:, r * Q_BLOCK:(r + 1) * Q_BLOCK]
        p_hi, p_lo = _split2(p_sum)
        dot = lambda a, b: jnp.dot(a, b, preferred_element_type=F32)
        imp_t = dot(selmat_ref[...], p_hi) + dot(selmat_ref[...], p_lo)
        j_idx = lax.broadcasted_iota(jnp.int32, (n_sel, Q_BLOCK), 0)
        j_f = j_idx.astype(F32)
        cur = t_col[:, 0:Q_BLOCK] // SEL_BLOCK
        forced = (j_idx == 0) | (j_idx == cur) | (j_idx == cur - 1)
        work = jnp.where(j_idx <= cur, imp_t + jnp.where(forced, FORCE_BONUS, 0.0), NEG)
        msel_t = jnp.zeros((n_sel, Q_BLOCK), F32)
        for _ in range(n_top):
            mx = jnp.max(work, axis=0, keepdims=True)
            first = jnp.min(jnp.where(work == mx, j_f, float(LANES)), axis=0, keepdims=True)
            pick = j_f == first
            msel_t = jnp.where(pick, 1.0, msel_t)
            work = jnp.where(pick, REMOVED, work)
        if n_sel < LANES:
            msel_t = jnp.concatenate([msel_t, jnp.zeros((LANES - n_sel, Q_BLOCK), F32)], axis=0)
        msel_bf = msel_t.astype(BF16)

        sel_key = lax.broadcasted_iota(jnp.int32, (SEL_TILE, cols), 0)
        bias_sel = bias_ref[g, 0:SEL_TILE, :]

        def sel_scores(kt, causal):
            k0 = pl.multiple_of(kt * SEL_TILE, SEL_TILE)
            ch = jnp.dot(emat_ref[kt], msel_bf, preferred_element_type=F32)
            ok = jnp.concatenate([ch] * R, axis=1) > 0.5
            if causal:
                ok = ok & (sel_key <= q_loc + (q0 - k0))
            s = jnp.where(ok, _dot_nt(ks_ref[0, g, pl.ds(k0, SEL_TILE), :], qs_bf) + bias_sel, NEG)
            return s, jnp.max(s, axis=0, keepdims=True)

        def sel_consume(m, acc, s_max, kt):
            c = slope * (kt * SEL_TILE - q0).astype(F32)
            m_new = jnp.maximum(m, s_max + c)
            p = jnp.exp(s_scr[...] + (c - m_new)).astype(BF16)
            acc = jnp.exp(m - m_new) * acc
            for i in range(SEL_TILE // Q_BLOCK):
                acc = acc + jnp.dot(vst_ref[0, g, kt * (SEL_TILE // Q_BLOCK) + i],
                                    p[i * Q_BLOCK:(i + 1) * Q_BLOCK, :], preferred_element_type=F32)
            return m_new, acc

        kt_diag = q0 // SEL_TILE
        s_first, max_first = sel_scores(kt_diag, True)
        s_scr[...] = s_first

        def sel_body(kt, lc):
            m, acc, s_max, kt_parked = lc
            s_next, max_next = sel_scores(kt, False)
            m, acc = sel_consume(m, acc, s_max, kt_parked)
            s_scr[...] = s_next
            return m, acc, max_next, kt

        m_s, acc_s, max_last, kt_last = lax.fori_loop(0, kt_diag, sel_body, init + (max_first, kt_diag))
        _, acc_s = sel_consume(m_s, acc_s, max_last, kt_last)
        o_s = finish(acc_s)

        w0 = pl.multiple_of(jnp.maximum(q0 + Q_BLOCK - win_span, 0), Q_BLOCK)
        win_key = lax.broadcasted_iota(jnp.int32, (win_span, cols), 0)
        reach = q_loc + (q0 - w0)
        ok_w = (win_key <= reach) & (win_key > reach - WINDOW)
        v_tiles = [vwt_ref[0, g, w0 // Q_BLOCK + i] for i in range(win_span // Q_BLOCK)]
        carry = step(init, qs_bf, slope, bias_ref[g, 0:win_span, :], kw_ref[0, g, pl.ds(w0, win_span), :],
                     v_tiles, q0 - w0, ok_w)
        o_w = finish(carry[1])

        for r in range(R):
            col = (g * R + r) * N_BRANCHES
            sl = slice(r * Q_BLOCK, (r + 1) * Q_BLOCK)
            y_rows.append(gates_t[col:col + 1, :] * o_c[:, sl] + gates_t[col + 1:col + 2, :] * o_s[:, sl]
                          + gates_t[col + 2:col + 3, :] * o_w[:, sl])
    y_t = jnp.concatenate(y_rows, axis=0)
    y_t = y_t * lax.rsqrt(jnp.mean(y_t * y_t, axis=0, keepdims=True) + LN_EPS) * gn_ref[...]
    o_ref[...] = y_t.T


def _nsa(h_all, B, S, kc, vc, ks_bf, vs_tiles, kw_bf, vw_tiles, gn):
    nq = S // Q_BLOCK
    n_cmp = (S - CMP_BLOCK) // CMP_STRIDE + 1
    n_rows_c = kc.shape[2]
    n_sel = S // SEL_BLOCK
    n_top = min(SEL_TOPN, n_sel)
    n_span = S // SEL_TILE
    win_span = min(WIN_SPAN, S)
    n_bias = max(win_span, SEL_TILE)
    cols = Q_PER_KV * Q_BLOCK
    slopes = np.array([2.0 ** (-8.0 * (h + 1) / N_HEADS) for h in range(N_HEADS)], np.float32)
    slope_cols = np.repeat(slopes.reshape(N_KV, Q_PER_KV), Q_BLOCK, axis=1)[:, None, :]
    rel = (np.arange(cols)[None, :] % Q_BLOCK - np.arange(n_bias)[:, None]).astype(np.float32)
    bias = -slope_cols * rel[None]
    selmat_t = jnp.asarray(_selection_matrix(n_rows_c, n_cmp, n_sel).T, dtype=BF16)
    key_blk = (np.arange(S) // SEL_BLOCK).reshape(n_span, SEL_TILE, 1)
    emat_t = jnp.asarray((key_blk == np.arange(LANES)[None, None, :]).astype(np.float32), dtype=BF16)
    qw = Q_PER_KV * HEAD_DIM * N_KV
    per_batch = lambda *dims: pl.BlockSpec((1,) + dims, lambda b, i: (b,) + (0,) * len(dims))
    const = lambda shape: pl.BlockSpec(shape, lambda b, i: (0,) * len(shape))
    return pl.pallas_call(
        functools.partial(_nsa_kernel, n_cmp=n_cmp, n_top=n_top, win_span=win_span),
        grid=(B, nq),
        in_specs=[pl.BlockSpec((Q_BLOCK, qw), lambda b, i: (b * nq + i, COL_Q // qw)),
                  pl.BlockSpec((Q_BLOCK, LANES), lambda b, i: (b * nq + i, COL_GATES // LANES)),
                  per_batch(N_KV, n_rows_c, HEAD_DIM), per_batch(N_KV, HEAD_DIM, n_rows_c),
                  per_batch(N_KV, S, HEAD_DIM), per_batch(N_KV, nq, V_AUG_ROWS, Q_BLOCK),
                  per_batch(N_KV, S, HEAD_DIM), per_batch(N_KV, nq, V_AUG_ROWS, Q_BLOCK),
                  const((N_KV, 1, cols)), const((N_KV, n_bias, cols)), const((n_sel, n_rows_c)),
                  const((n_span, SEL_TILE, LANES)), const((NSA_WIDTH, 1))],
        out_specs=pl.BlockSpec((Q_BLOCK, NSA_WIDTH), lambda b, i: (b * nq + i, 0)),
        out_shape=jax.ShapeDtypeStruct((B * S, NSA_WIDTH), F32),
        scratch_shapes=[pltpu.VMEM((SEL_TILE, cols), F32)],
        compiler_params=_cparams(("arbitrary", "arbitrary")),
        name="nsa",
    )(h_all, h_all, kc, vc.transpose(0, 1, 3, 2), ks_bf, vs_tiles, kw_bf, vw_tiles,
      jnp.asarray(slope_cols), jnp.asarray(bias), selmat_t, emat_t, gn.reshape(NSA_WIDTH, 1))


def _out_ln_kernel(ylru_ref, ynsa_ref, x_ref, w_ref, g_ref, b_ref, o_ref, *, alpha):
    mix = jnp.dot(ylru_ref[...].astype(BF16), w_ref[0:LRU_WIDTH, :], preferred_element_type=F32)
    mix = mix + jnp.dot(ynsa_ref[...].astype(BF16), w_ref[LRU_WIDTH:, :], preferred_element_type=F32)
    o_ref[...] = _layer_norm_rows(alpha * x_ref[...] + mix, g_ref[...], b_ref[...])


def _out_ln(ylru, ynsa, xf, w_bf, g, b, alpha, tm=512):
    T = xf.shape[0]
    const = lambda shape: pl.BlockSpec(shape, lambda i: (0,) * len(shape))
    return pl.pallas_call(
        functools.partial(_out_ln_kernel, alpha=alpha),
        grid=(T // tm,),
        in_specs=[pl.BlockSpec((tm, LRU_WIDTH), lambda i: (i, 0)),
                  pl.BlockSpec((tm, NSA_WIDTH), lambda i: (i, 0)),
                  pl.BlockSpec((tm, D_MODEL), lambda i: (i, 0)),
                  const((D_MODEL, D_MODEL)), const((1, D_MODEL)), const((1, D_MODEL))],
        out_specs=pl.BlockSpec((tm, D_MODEL), lambda i: (i, 0)),
        out_shape=jax.ShapeDtypeStruct((T, D_MODEL), F32),
        compiler_params=_cparams(("arbitrary",)),
        name="out_ln",
    )(ylru, ynsa, xf, w_bf, g.reshape(1, D_MODEL), b.reshape(1, D_MODEL))


def _topk_rows(vals, order, payload, n_keep):
    n = vals.shape[1]
    kidx = lax.broadcasted_iota(jnp.int32, (n_keep, n), 0)
    never = float(1 << 20)
    kept_v = jnp.zeros((n_keep, n), F32)
    kept_p = jnp.zeros((n_keep, n), F32)
    work = vals
    for r in range(n_keep):
        mx = jnp.max(work, axis=0, keepdims=True)
        first = jnp.min(jnp.where(work == mx, order, never), axis=0, keepdims=True)
        pick = order == first
        kept_v = jnp.where(kidx == r, mx, kept_v)
        if payload is None:
            kept_p = jnp.where(kidx == r, first, kept_p)
        else:
            kept_p = jnp.where(kidx == r, jnp.max(jnp.where(pick, payload, -1.0), axis=0, keepdims=True), kept_p)
        work = jnp.where(pick, REMOVED, work)
    return kept_v, kept_p


def _candidate_rows():
    ij = [(i, 0) for i in range(PEER_TOPK)]
    for j in range(1, SUBLANES):
        ij += [(i, j) for i in range(SUBLANES)]
    ij += [(0, j) for j in range(SUBLANES, PEER_TOPK)]
    flat = np.array([i * PEER_TOPK + j for i, j in ij], np.float32)
    valid = np.array([(i + 1) * (j + 1) <= PEER_TOPK for i, j in ij])
    return flat[:, None], np.where(valid, 0.0, REMOVED).astype(np.float32)[:, None]


def _candidates(a0, a1):
    return ([(a0, a1[0:1, :])]
            + [(a0[0:SUBLANES, :], a1[j:j + 1, :]) for j in range(1, SUBLANES)]
            + [(a0[0:1, :], a1[SUBLANES:PEER_TOPK, :])])


def _peer_route_kernel(x_ref, wq_ref, sk_ref, flat_ref, pad_ref, e_ref, g_ref):
    tm = x_ref.shape[0]
    q = jnp.dot(x_ref[...].astype(BF16), wq_ref[...], preferred_element_type=F32)
    key_idx = lax.broadcasted_iota(jnp.int32, (PEER_KEYS, tm), 0).astype(F32)
    flat = jnp.broadcast_to(flat_ref[...], (flat_ref.shape[0], tm))
    experts, gates = [], []
    for h in range(PEER_HEADS):
        sv, si = [], []
        for c in range(2):
            col = (h * 2 + c) * PEER_KEY_DIM
            s_t = _dot_nt(sk_ref[c], q[:, col:col + PEER_KEY_DIM], HI)
            v, i = _topk_rows(s_t, key_idx, None, PEER_TOPK)
            sv.append(v)
            si.append(i)
        cand = jnp.concatenate([a + b for a, b in _candidates(sv[0], sv[1])], axis=0) + pad_ref[...]
        ecand = jnp.concatenate([a * float(PEER_KEYS) + b for a, b in _candidates(si[0], si[1])], axis=0)
        cv, ce = _topk_rows(cand, flat, ecand, PEER_TOPK)
        ex = jnp.exp(cv - cv[0:1, :])
        gates.append(ex / jnp.sum(ex, axis=0, keepdims=True))
        experts.append(ce)
    e_ref[...] = jnp.concatenate(experts, axis=0).T.astype(jnp.int32)
    g_ref[...] = jnp.concatenate(gates, axis=0).T


def _peer_route(x1, wq_bf, subkeys, tm=256):
    T = x1.shape[0]
    tm = min(tm, T)
    nq = wq_bf.shape[1]
    flat, pad = _candidate_rows()
    n_cand = flat.shape[0]
    out = lambda: pl.BlockSpec((tm, N_PAIRS), lambda i: (i, 0))
    return pl.pallas_call(
        _peer_route_kernel,
        grid=(T // tm,),
        in_specs=[pl.BlockSpec((tm, D_MODEL), lambda i: (i, 0)),
                  pl.BlockSpec((D_MODEL, nq), lambda i: (0, 0)),
                  pl.BlockSpec((2, PEER_KEYS, PEER_KEY_DIM), lambda i: (0, 0, 0)),
                  pl.BlockSpec((n_cand, 1), lambda i: (0, 0)),
                  pl.BlockSpec((n_cand, 1), lambda i: (0, 0))],
        out_specs=[out(), out()],
        out_shape=[jax.ShapeDtypeStruct((T, N_PAIRS), jnp.int32),
                   jax.ShapeDtypeStruct((T, N_PAIRS), F32)],
        compiler_params=_cparams(("arbitrary",)),
        name="peer_route",
    )(x1, wq_bf, subkeys, jnp.asarray(flat), jnp.asarray(pad))


TOKENS_PER_MATMUL = 16


def _pack_table(tab):
    return tab.astype(BF16).reshape(N_EXPERTS, D_ROWS, LANES)


def _sublane_sums(ps, roll, where, sub):
    lvl = ps
    for r in (1, 2, 4):
        m = (sub % (2 * r)) < r
        half = len(lvl) // 2
        lvl = [where(m, lvl[i], lvl[i + half]) + roll(where(m, lvl[i + half], lvl[i]), r) for i in range(half)]
    return lvl[0]


def _sum_src():
    sub = np.arange(SUBLANES)[:, None] * np.ones((1, LANES), np.int64)
    ps = [np.full((SUBLANES, LANES), 10.0 ** i) for i in range(SUBLANES)]
    out = _sublane_sums(ps, lambda v, r: np.roll(v, r, axis=0), np.where, sub)
    return [int(round(np.log10(out[s, 0] / SUBLANES))) for s in range(SUBLANES)]


_SUM_SRC = _sum_src()


def _peer_u_kernel(e_ref, x_ref, tab_ref, gate_ref, w_ref, part_ref, zt_ref, *, tb):
    sub = lax.broadcasted_iota(jnp.int32, (SUBLANES, LANES), 0)
    lane = lax.broadcasted_iota(jnp.int32, (N_PAIRS, tb), 1)

    def gather_dots(t, slot):
        xt = x_ref[t]
        for grp in range(N_PAIRS // SUBLANES):
            prods = [None] * SUBLANES
            for s in range(SUBLANES):
                prods[_SUM_SRC[s]] = tab_ref[e_ref[t * N_PAIRS + grp * SUBLANES + s]].astype(F32) * xt
            part_ref[slot, grp * SUBLANES:(grp + 1) * SUBLANES, :] = _sublane_sums(
                prods, lambda v, r: pltpu.roll(v, r, axis=0), jnp.where, sub)

    def lane_sums(t0):
        z0 = jnp.sum(part_ref[0], axis=1, keepdims=True)
        z1 = jnp.sum(part_ref[1], axis=1, keepdims=True)
        zt_ref[...] = jnp.where(lane == t0, z0, jnp.where(lane == t0 + 1, z1, zt_ref[...]))

    part_ref[...] = jnp.zeros_like(part_ref)
    zt_ref[...] = jnp.zeros_like(zt_ref)

    def two_tokens(i, carry):
        lane_sums(2 * i - 2)
        gather_dots(2 * i, 0)
        gather_dots(2 * i + 1, 1)
        return carry

    lax.fori_loop(0, tb // 2, two_tokens, 0)
    lane_sums(tb - 2)
    w_ref[...] = jax.nn.gelu(zt_ref[...].T) * gate_ref[...]


def _peer_u(experts, x3, tab, gates, tb=128):
    T = x3.shape[0]
    vm = lambda: pl.BlockSpec((tb, N_PAIRS), lambda i: (i, 0))
    return pl.pallas_call(
        functools.partial(_peer_u_kernel, tb=tb),
        grid=(T // tb,),
        in_specs=[pl.BlockSpec((tb * N_PAIRS,), lambda i: (i,), memory_space=pltpu.SMEM),
                  pl.BlockSpec((tb, D_ROWS, LANES), lambda i: (i, 0, 0)),
                  pl.BlockSpec((N_EXPERTS, D_ROWS, LANES), lambda i: (0, 0, 0), pipeline_mode=pl.Buffered(1)),
                  vm()],
        out_specs=vm(),
        out_shape=jax.ShapeDtypeStruct((T, N_PAIRS), F32),
        scratch_shapes=[pltpu.VMEM((2, N_PAIRS, LANES), F32), pltpu.VMEM((N_PAIRS, tb), F32)],
        compiler_params=_cparams(("arbitrary",)),
        name="peer_u",
    )(experts.reshape(T * N_PAIRS), x3, tab, gates)


def _expand_matrix():
    k = np.arange(N_PAIRS)[:, None]
    c = np.arange(N_PAIRS * D_ROWS)[None, :]
    return (c // D_ROWS == k).astype(np.float32)


def _peer_v_kernel(e_ref, w_ref, x_ref, tab_ref, expand_ref, g_ref, b_ref, o_ref, y_ref, wide_ref, *, tb, alpha):
    n_col = N_PAIRS * D_ROWS
    diag = (lax.broadcasted_iota(jnp.int32, (D_ROWS, n_col), 1) % D_ROWS
            == lax.broadcasted_iota(jnp.int32, (D_ROWS, n_col), 0))
    part_row = lax.broadcasted_iota(jnp.int32, (SUBLANES, LANES), 0)

    def expand_tokens(i, carry):
        t0 = pl.multiple_of(i * TOKENS_PER_MATMUL, TOKENS_PER_MATMUL)
        parts = []
        for j in range(TOKENS_PER_MATMUL):
            wrow = w_ref[pl.ds(t0 + j, 1), :]
            hi = wrow.astype(BF16).astype(F32)
            mid = (wrow - hi).astype(BF16).astype(F32)
            lo = ((wrow - hi) - mid).astype(BF16).astype(F32)
            bc = lambda v: jnp.broadcast_to(v, (SUBLANES, LANES))
            parts.append(jnp.where(part_row == 0, bc(hi), jnp.where(part_row == 1, bc(mid),
                                                                    jnp.where(part_row == 2, bc(lo), 0.0))))
        lhs = jnp.concatenate(parts, axis=0).astype(BF16)
        wide = jnp.dot(lhs, expand_ref[...], preferred_element_type=F32)
        wide_ref[pl.ds(t0, TOKENS_PER_MATMUL)] = wide.reshape(TOKENS_PER_MATMUL, SUBLANES, n_col)
        return carry

    lax.fori_loop(0, tb // TOKENS_PER_MATMUL, expand_tokens, 0)

    def left_operand(t):
        rows = [jnp.where(diag, jnp.broadcast_to(wide_ref[t, p:p + 1, :], (D_ROWS, n_col)), 0.0) for p in range(3)]
        return jnp.concatenate(rows, axis=0).astype(BF16)

    def weighted_sum(t, a):
        rows = [tab_ref[e_ref[t * N_PAIRS + k]].astype(F32) for k in range(N_PAIRS)]
        rhs = jnp.concatenate(rows, axis=0).astype(BF16)
        y3 = jnp.dot(a, rhs, preferred_element_type=F32)
        return (y3[0:D_ROWS] + y3[D_ROWS:2 * D_ROWS]) + y3[2 * D_ROWS:]

    tokens_per_step = 16

    def some_tokens(i, carry):
        ts = [tokens_per_step * i + j for j in range(tokens_per_step)]
        ys = [weighted_sum(t, left_operand(t)) for t in ts]
        for t, y in zip(ts, ys):
            y_ref[t] = y
        return carry

    lax.fori_loop(0, tb // tokens_per_step, some_tokens, 0)
    z = alpha * x_ref[...] + y_ref[...]
    inv_d = 1.0 / D_MODEL
    mu = jnp.sum(jnp.sum(z, axis=2, keepdims=True), axis=1, keepdims=True) * inv_d
    zc = z - mu
    var = jnp.sum(jnp.sum(zc * zc, axis=2, keepdims=True), axis=1, keepdims=True) * inv_d
    o_ref[...] = zc * lax.rsqrt(var + LN_EPS) * g_ref[...] + b_ref[...]


def _peer_v(experts, wgt, x3, tab, g, b, alpha, tb=128):
    T = x3.shape[0]
    return pl.pallas_call(
        functools.partial(_peer_v_kernel, tb=tb, alpha=alpha),
        grid=(T // tb,),
        in_specs=[pl.BlockSpec((tb * N_PAIRS,), lambda i: (i,), memory_space=pltpu.SMEM),
                  pl.BlockSpec((tb, N_PAIRS), lambda i: (i, 0)),
                  pl.BlockSpec((tb, D_ROWS, LANES), lambda i: (i, 0, 0)),
                  pl.BlockSpec((N_EXPERTS, D_ROWS, LANES), lambda i: (0, 0, 0), pipeline_mode=pl.Buffered(1)),
                  pl.BlockSpec((N_PAIRS, N_PAIRS * D_ROWS), lambda i: (0, 0)),
                  pl.BlockSpec((1, D_ROWS, LANES), lambda i: (0, 0, 0)),
                  pl.BlockSpec((1, D_ROWS, LANES), lambda i: (0, 0, 0))],
        out_specs=pl.BlockSpec((tb, D_ROWS, LANES), lambda i: (i, 0, 0)),
        out_shape=jax.ShapeDtypeStruct((T, D_ROWS, LANES), F32),
        scratch_shapes=[pltpu.VMEM((tb, D_ROWS, LANES), F32), pltpu.VMEM((tb, SUBLANES, N_PAIRS * D_ROWS), F32)],
        compiler_params=_cparams(("arbitrary",)),
        name="peer_v",
    )(experts.reshape(T * N_PAIRS), wgt, x3, tab, jnp.asarray(_expand_matrix(), dtype=BF16),
      g.reshape(1, D_ROWS, LANES), b.reshape(1, D_ROWS, LANES))


def kernel(x, w_in, b_in, conv_w, conv_b, lru_wa, lru_ba, lru_wx, lru_bx, lru_lambda, cmp_pos_k, cmpk_w1, cmpk_b1, cmpk_w2, cmpk_b2, cmp_pos_v, cmpv_w1, cmpv_b1, cmpv_w2, cmpv_b2, gn_lru_g, gn_nsa_g, w_out, ln1_g, ln1_b, peer_wq, peer_subkeys, peer_u, peer_v, ln2_g, ln2_b):
    B, S, D = x.shape
    T = B * S
    depth = w_in.shape[0]
    alpha = (2 * depth) ** 0.25
    xf = x.reshape(T, D)
    for l in range(depth):
        w_in_bf = jnp.pad(w_in[l], ((0, 0), (0, IN_COLS_PAD - IN_COLS))).astype(BF16)
        b_in_row = jnp.pad(b_in[l], (0, IN_COLS_PAD - IN_COLS)).reshape(1, IN_COLS_PAD)
        h_all, kraw, kbf, vtiles = _in_proj(xf, w_in_bf, b_in_row, B, S)

        y_lru = _rglru(h_all, B, S, conv_w[l], conv_b[l], _block_diag(lru_wa[l]), lru_ba[l],
                       _block_diag(lru_wx[l]), lru_bx[l], lru_lambda[l], gn_lru_g[l])

        w1 = jnp.stack([cmpk_w1[l], cmpv_w1[l]])
        w1_pos = w1.reshape(2, 2, CMP_STRIDE, HEAD_DIM, HEAD_DIM)
        w1l = jnp.concatenate([w1_pos[:, 0], w1_pos[:, 1]], axis=-1)
        pos = jnp.stack([cmp_pos_k[l], cmp_pos_v[l]]).reshape(2, 1, CMP_BLOCK * HEAD_DIM)
        posflat = jnp.broadcast_to(pos, (2, SUBLANES, CMP_BLOCK * HEAD_DIM))
        kcvc = _compress(kraw, w1l, w1, posflat,
                         jnp.stack([cmpk_b1[l], cmpv_b1[l]]).reshape(2, 1, HEAD_DIM),
                         jnp.stack([cmpk_w2[l], cmpv_w2[l]]),
                         jnp.stack([cmpk_b2[l], cmpv_b2[l]]).reshape(2, 1, HEAD_DIM))
        kcvc = kcvc.reshape(2, B, N_KV, S // CMP_STRIDE, HEAD_DIM)
        y_nsa = _nsa(h_all, B, S, kcvc[0], kcvc[1], kbf[0], vtiles[0], kbf[1], vtiles[1], gn_nsa_g[l])

        x1 = _out_ln(y_lru, y_nsa, xf, w_out[l].astype(BF16), ln1_g[l], ln1_b[l], alpha)

        experts, gates = _peer_route(x1, peer_wq[l].astype(BF16), peer_subkeys[l])
        x3 = x1.reshape(T, D_ROWS, LANES)
        wgt = _peer_u(experts, x3, _pack_table(peer_u[l]), gates)
        xf = _peer_v(experts, wgt, x3, _pack_table(peer_v[l]), ln2_g[l], ln2_b[l], alpha).reshape(T, D)
    return xf.reshape(B, S, D)
```

```python
import functools

import numpy as np
import jax
import jax.numpy as jnp
from jax import lax
from jax.experimental import pallas as pl
from jax.experimental.pallas import tpu as pltpu

D_MODEL = 1024
LRU_WIDTH = 512
LRU_BLOCKS = 8
LRU_BLOCK_DIM = LRU_WIDTH // LRU_BLOCKS
CONV_WIDTH = 4
LRU_C = 8.0
N_HEADS = 8
HEAD_DIM = 64
N_KV = 2
Q_PER_KV = N_HEADS // N_KV
NSA_WIDTH = N_HEADS * HEAD_DIM
KV_WIDTH = N_KV * HEAD_DIM
N_BRANCHES = 3
CMP_BLOCK = 32
CMP_STRIDE = 16
SEL_BLOCK = 64
SEL_TOPN = 16
WINDOW = 512
Q_BLOCK = 128
FORCE_BONUS = 1e4
NEG = -1e30
REMOVED = -3.0e38
PEER_HEADS = 8
PEER_KEYS = 128
PEER_TOPK = 16
PEER_KEY_DIM = 128
N_EXPERTS = PEER_KEYS * PEER_KEYS
N_PAIRS = PEER_HEADS * PEER_TOPK
LN_EPS = 1e-5
IN_SPLITS = (LRU_WIDTH, LRU_WIDTH, NSA_WIDTH) + (KV_WIDTH,) * 6 + (N_HEADS * N_BRANCHES,)
IN_COLS = sum(IN_SPLITS)

LANES = 128
SUBLANES = 8
VMEM_LIMIT_BYTES = 56 * 1024 * 1024

IN_COLS_PAD = -(-IN_COLS // LANES) * LANES
COL_LRU_X = 0
COL_LRU_GATE = LRU_WIDTH
COL_Q = 2 * LRU_WIDTH
COL_KV = COL_Q + NSA_WIDTH
COL_GATES = COL_KV + 6 * KV_WIDTH

HI = lax.Precision.HIGHEST
F32 = jnp.float32
BF16 = jnp.bfloat16

D_ROWS = D_MODEL // LANES


def _cparams(sem):
    return pltpu.CompilerParams(dimension_semantics=sem, vmem_limit_bytes=VMEM_LIMIT_BYTES)


def _dot_nt(a, b, precision=None):
    return lax.dot_general(a, b, (((1,), (1,)), ((), ())), precision=precision,
                           preferred_element_type=F32)


def _split2(a):
    hi = a.astype(BF16)
    return hi, (a - hi.astype(F32)).astype(BF16)


def _dot_split(a, b):
    a_hi, a_lo = _split2(a)
    b_hi, b_lo = _split2(b)
    dot = lambda u, v: jnp.dot(u, v, preferred_element_type=F32)
    return dot(a_hi, b_hi) + (dot(a_hi, b_lo) + dot(a_lo, b_hi))


def _layer_norm_rows(z, g, b):
    mu = jnp.mean(z, axis=-1, keepdims=True)
    zc = z - mu
    var = jnp.mean(zc * zc, axis=-1, keepdims=True)
    return zc * lax.rsqrt(var + LN_EPS) * g + b


V_AUG_ROWS = HEAD_DIM + 16


def _in_proj_kernel(x_ref, w_ref, b_ref, o_ref, kraw_ref, kbf_ref, vt_ref):
    h = jnp.dot(x_ref[...].astype(BF16), w_ref[...], preferred_element_type=F32) + b_ref[...]
    o_ref[...] = h
    group_cols = lambda col, g: slice(col + g * HEAD_DIM, col + (g + 1) * HEAD_DIM)
    for j in range(2):
        for g in range(N_KV):
            kraw_ref[j, 0, g] = h[:, group_cols(COL_KV + j * KV_WIDTH, g)]
            kbf_ref[j, 0, g] = h[:, group_cols(COL_KV + (2 + 2 * j) * KV_WIDTH, g)].astype(BF16)
    ones_row = (lax.broadcasted_iota(jnp.int32, (V_AUG_ROWS - HEAD_DIM, Q_BLOCK), 0) == 0).astype(F32)
    for j in range(2):
        col = COL_KV + (3 + 2 * j) * KV_WIDTH
        for qb in range(h.shape[0] // Q_BLOCK):
            v_t = h[qb * Q_BLOCK:(qb + 1) * Q_BLOCK, col:col + KV_WIDTH].T
            for g in range(N_KV):
                tile = jnp.concatenate([v_t[g * HEAD_DIM:(g + 1) * HEAD_DIM, :], ones_row], axis=0)
                vt_ref[j, 0, g, qb] = tile.astype(BF16)


def _in_proj(xf, w_bf, b_row, B, S, tm=512):
    T = xf.shape[0]
    tm = min(tm, S)
    nt = S // tm
    per_tile = lambda *dims: pl.BlockSpec((2, 1, N_KV) + dims, lambda i: (0, i // nt, 0, i % nt) + (0,) * (len(dims) - 1))
    return pl.pallas_call(
        _in_proj_kernel,
        grid=(T // tm,),
        in_specs=[pl.BlockSpec((tm, D_MODEL), lambda i: (i, 0)),
                  pl.BlockSpec((D_MODEL, IN_COLS_PAD), lambda i: (0, 0)),
                  pl.BlockSpec((1, IN_COLS_PAD), lambda i: (0, 0))],
        out_specs=[pl.BlockSpec((tm, IN_COLS_PAD), lambda i: (i, 0)),
                   per_tile(tm, HEAD_DIM), per_tile(tm, HEAD_DIM),
                   per_tile(tm // Q_BLOCK, V_AUG_ROWS, Q_BLOCK)],
        out_shape=[jax.ShapeDtypeStruct((T, IN_COLS_PAD), F32),
                   jax.ShapeDtypeStruct((2, B, N_KV, S, HEAD_DIM), F32),
                   jax.ShapeDtypeStruct((2, B, N_KV, S, HEAD_DIM), BF16),
                   jax.ShapeDtypeStruct((2, B, N_KV, S // Q_BLOCK, V_AUG_ROWS, Q_BLOCK), BF16)],
        compiler_params=_cparams(("arbitrary",)),
        name="in_proj",
    )(xf, w_bf, b_row)


def _rglru_kernel(x_ref, gate_ref, cw_ref, cb_ref, wa_ref, ba_ref, wx_ref, bx_ref, lam_ref, gn_ref,
                  o_ref, xprev_ref, h_ref, *, ts):
    i = pl.program_id(1)

    @pl.when(i == 0)
    def _():
        xprev_ref[...] = jnp.zeros_like(xprev_ref)
        h_ref[...] = jnp.zeros_like(h_ref)

    xb = x_ref[...]
    xp = xprev_ref[...]
    row = lax.broadcasted_iota(jnp.int32, xb.shape, 0)
    xc = cw_ref[CONV_WIDTH - 1:CONV_WIDTH, :] * xb + cb_ref[...]
    for j in range(1, CONV_WIDTH):
        shifted = jnp.where(row >= j, pltpu.roll(xb, j, axis=0), pltpu.roll(xp, j, axis=0))
        xc = xc + cw_ref[CONV_WIDTH - 1 - j:CONV_WIDTH - j, :] * shifted
    xprev_ref[...] = xb

    r = jax.nn.sigmoid(_dot_split(xc, wa_ref[...]) + ba_ref[...])
    ig = jax.nn.sigmoid(_dot_split(xc, wx_ref[...]) + bx_ref[...])
    lam = lam_ref[...]
    softplus_neg_lam = jnp.maximum(-lam, 0.0) + jnp.log1p(jnp.exp(-jnp.abs(lam)))
    log_a = -LRU_C * r * softplus_neg_lam
    a = jnp.exp(log_a)
    th = jnp.tanh(log_a)
    u = jnp.sqrt(-2.0 * th / (1.0 - th)) * (ig * xc)

    d = 1
    while d < ts:
        keep = row >= d
        a_sh = pltpu.roll(a, d, axis=0)
        u_sh = pltpu.roll(u, d, axis=0)
        u = jnp.where(keep, a * u_sh + u, u)
        a = jnp.where(keep, a * a_sh, a)
        d *= 2
    h = u + a * h_ref[0:1, :]
    h_ref[0:1, :] = h[ts - 1:ts, :]

    y = h * jax.nn.gelu(gate_ref[...])
    y = y * lax.rsqrt(jnp.mean(y * y, axis=-1, keepdims=True) + LN_EPS) * gn_ref[...]
    o_ref[...] = y


def _rglru(h_all, B, S, cw, cb, wa_bd, ba, wx_bd, bx, lam, gn, ts=512):
    ts = min(ts, S)
    nt = S // ts
    C = LRU_WIDTH
    row = lambda v: v.reshape(1, C)
    const = lambda shape: pl.BlockSpec(shape, lambda b, i: (0,) * len(shape))
    return pl.pallas_call(
        functools.partial(_rglru_kernel, ts=ts),
        grid=(B, nt),
        in_specs=[pl.BlockSpec((ts, C), lambda b, i: (b * nt + i, COL_LRU_X // C)),
                  pl.BlockSpec((ts, C), lambda b, i: (b * nt + i, COL_LRU_GATE // C)),
                  const((CONV_WIDTH, C)), const((1, C)), const((C, C)), const((1, C)),
                  const((C, C)), const((1, C)), const((1, C)), const((1, C))],
        out_specs=pl.BlockSpec((ts, C), lambda b, i: (b * nt + i, 0)),
        out_shape=jax.ShapeDtypeStruct((B * S, C), F32),
        scratch_shapes=[pltpu.VMEM((ts, C), F32), pltpu.VMEM((SUBLANES, C), F32)],
        compiler_params=_cparams(("arbitrary", "arbitrary")),
        name="rglru",
    )(h_all, h_all, cw, row(cb), wa_bd, row(ba), wx_bd, row(bx), row(lam), row(gn))


def _block_diag(w):
    n, d, e = w.shape
    eye = jnp.eye(n, dtype=w.dtype)
    return (w[:, :, None, :] * eye[:, None, :, None]).reshape(n * d, n * e)


def _compress_kernel(k_ref, w1l_ref, w1_ref, pos_ref, b1_ref, w2_ref, b2_ref, o_ref):
    nr = o_ref.shape[2]
    ab = jnp.zeros((nr, 2 * HEAD_DIM), F32)
    for l in range(CMP_STRIDE):
        k_l = k_ref[0, 0, 0, pl.ds(l, nr, stride=CMP_STRIDE), :]
        ab = ab + jnp.dot(k_l, w1l_ref[0, l], precision=HI, preferred_element_type=F32)
    ab_next = pltpu.roll(ab, nr - 1, axis=0)
    pre = ab[:, :HEAD_DIM] + ab_next[:, HEAD_DIM:]
    posc = jnp.dot(pos_ref[0], w1_ref[0], precision=HI, preferred_element_type=F32)[0:1, :]
    h1 = jax.nn.gelu(pre + posc + b1_ref[0])
    o_ref[0, 0] = jnp.dot(h1, w2_ref[0], precision=HI, preferred_element_type=F32) + b2_ref[0]


def _compress(kraw, w1l, w1, posflat, b1, w2, b2):
    _, B, G, S, _ = kraw.shape
    nr = S // CMP_STRIDE
    fan = CMP_BLOCK * HEAD_DIM
    wspec = lambda shape: pl.BlockSpec((1,) + shape, lambda s, i: (s,) + (0,) * len(shape))
    return pl.pallas_call(
        _compress_kernel,
        grid=(2, B * G),
        in_specs=[pl.BlockSpec((1, 1, 1, S, HEAD_DIM), lambda s, i: (s, i // G, i % G, 0, 0)),
                  wspec((CMP_STRIDE, HEAD_DIM, 2 * HEAD_DIM)), wspec((fan, HEAD_DIM)), wspec((SUBLANES, fan)),
                  wspec((1, HEAD_DIM)), wspec((HEAD_DIM, HEAD_DIM)), wspec((1, HEAD_DIM))],
        out_specs=pl.BlockSpec((1, 1, nr, HEAD_DIM), lambda s, i: (s, i, 0, 0)),
        out_shape=jax.ShapeDtypeStruct((2, B * G, nr, HEAD_DIM), F32),
        compiler_params=_cparams(("arbitrary", "arbitrary")),
        name="compress",
    )(kraw, w1l, w1, posflat, b1, w2, b2)


def _selection_matrix(n_rows, n_cmp, n_sel):
    cs = np.arange(n_rows)[:, None] * CMP_STRIDE
    ss = np.arange(n_sel)[None, :] * SEL_BLOCK
    ov = np.clip(np.minimum(cs + CMP_BLOCK, ss + SEL_BLOCK) - np.maximum(cs, ss), 0, None)
    ov = (ov / CMP_STRIDE).astype(np.float32)
    ov[n_cmp:] = 0.0
    return ov


SEL_TILE = 4 * Q_BLOCK
WIN_SPAN = WINDOW + Q_BLOCK


def _nsa_kernel(q_ref, gt_ref, kc_ref, vct_ref, ks_ref, vst_ref, kw_ref, vwt_ref, slope_ref, bias_ref,
                selmat_ref, emat_ref, gn_ref, o_ref, s_scr, *, n_cmp, n_top, win_span):
    qt = pl.program_id(1)
    q0 = qt * Q_BLOCK
    R = Q_PER_KV
    cols = R * Q_BLOCK
    n_rows_c = kc_ref.shape[2]
    n_sel = selmat_ref.shape[0]
    q_loc = lax.broadcasted_iota(jnp.int32, (1, cols), 1) % Q_BLOCK
    t_col = q0 + q_loc
    gates_t = jax.nn.sigmoid(gt_ref[...]).T

    def step(carry, qs_bf, slope, bias, k_bf, v_tiles, offset, ok):
        m, acc = carry
        s = jnp.where(ok, _dot_nt(k_bf, qs_bf) + bias, NEG)
        c = slope * (-offset).astype(F32)
        m_new = jnp.maximum(m, jnp.max(s, axis=0, keepdims=True) + c)
        p = jnp.exp(s + (c - m_new)).astype(BF16)
        acc = jnp.exp(m - m_new) * acc
        for i, v in enumerate(v_tiles):
            acc = acc + jnp.dot(v, p[i * Q_BLOCK:(i + 1) * Q_BLOCK, :], preferred_element_type=F32)
        return m_new, acc

    init = (jnp.full((1, cols), NEG, F32), jnp.zeros((V_AUG_ROWS, cols), F32))
    finish = lambda acc: acc[0:HEAD_DIM, :] / acc[HEAD_DIM:HEAD_DIM + 1, :]
    y_rows = []
    for g in range(N_KV):
        qg = q_ref[:, g * R * HEAD_DIM:(g + 1) * R * HEAD_DIM] * (HEAD_DIM ** -0.5)
        qs = jnp.concatenate([qg[:, r * HEAD_DIM:(r + 1) * HEAD_DIM] for r in range(R)], axis=0)
        qs_bf = qs.astype(BF16)
        slope = slope_ref[g]

        kc = kc_ref[0, g]
        c_idx = lax.broadcasted_iota(jnp.int32, (n_rows_c, 1), 0)
        cmp_end = c_idx * CMP_STRIDE + (CMP_BLOCK - 1)
        valid_c = (cmp_end <= t_col) & (c_idx < n_cmp)
        q_hi, q_lo = _split2(qs)
        k_hi, k_lo = _split2(kc)
        s = _dot_nt(k_hi, q_hi) + (_dot_nt(k_hi, q_lo) + _dot_nt(k_lo, q_hi))
        s = s - slope * jnp.abs(t_col - cmp_end).astype(F32)
        s = jnp.where(valid_c, s, NEG)
        e = jnp.exp(s - jnp.max(s, axis=0, keepdims=True))
        p_c = e / jnp.sum(e, axis=0, keepdims=True)
        p_c = p_c * (t_col >= CMP_BLOCK - 1).astype(F32)
        o_c = jnp.dot(vct_ref[0, g].astype(BF16), p_c.astype(BF16), preferred_element_type=F32)

        p_sum = p_c[:, 0:Q_BLOCK]
        for r in range(1, R):
            p_sum = p_sum + p_c[:, r * Q_BLOCK:(r + 1) * Q_BLOCK]
        p_hi, p_lo = _split2(p_sum)
        dot = lambda a, b: jnp.dot(a, b, preferred_element_type=F32)
        imp_t = dot(selmat_ref[...], p_hi) + dot(selmat_ref[...], p_lo)
        j_idx = lax.broadcasted_iota(jnp.int32, (n_sel, Q_BLOCK), 0)
        j_f = j_idx.astype(F32)
        cur = t_col[:, 0:Q_BLOCK] // SEL_BLOCK
        forced = (j_idx == 0) | (j_idx == cur) | (j_idx == cur - 1)
        work = jnp.where(j_idx <= cur, imp_t + jnp.where(forced, FORCE_BONUS, 0.0), NEG)
        msel_t = jnp.zeros((n_sel, Q_BLOCK), F32)
        for _ in range(n_top):
            mx = jnp.max(work, axis=0, keepdims=True)
            first = jnp.min(jnp.where(work == mx, j_f, float(LANES)), axis=0, keepdims=True)
            pick = j_f == first
            msel_t = jnp.where(pick, 1.0, msel_t)
            work = jnp.where(pick, REMOVED, work)
        if n_sel < LANES:
            msel_t = jnp.concatenate([msel_t, jnp.zeros((LANES - n_sel, Q_BLOCK), F32)], axis=0)
        msel_bf = msel_t.astype(BF16)

        sel_key = lax.broadcasted_iota(jnp.int32, (SEL_TILE, cols), 0)
        bias_sel = bias_ref[g, 0:SEL_TILE, :]

        def sel_scores(kt, causal):
            k0 = pl.multiple_of(kt * SEL_TILE, SEL_TILE)
            ch = jnp.dot(emat_ref[kt], msel_bf, preferred_element_type=F32)
            ok = jnp.concatenate([ch] * R, axis=1) > 0.5
            if causal:
                ok = ok & (sel_key <= q_loc + (q0 - k0))
            s = jnp.where(ok, _dot_nt(ks_ref[0, g, pl.ds(k0, SEL_TILE), :], qs_bf) + bias_sel, NEG)
            return s, jnp.max(s, axis=0, keepdims=True)

        def sel_consume(m, acc, s_max, kt):
            c = slope * (kt * SEL_TILE - q0).astype(F32)
            m_new = jnp.maximum(m, s_max + c)
            p = jnp.exp(s_scr[...] + (c - m_new)).astype(BF16)
            acc = jnp.exp(m - m_new) * acc
            for i in range(SEL_TILE // Q_BLOCK):
                acc = acc + jnp.dot(vst_ref[0, g, kt * (SEL_TILE // Q_BLOCK) + i],
                                    p[i * Q_BLOCK:(i + 1) * Q_BLOCK, :], preferred_element_type=F32)
            return m_new, acc

        kt_diag = q0 // SEL_TILE
        s_first, max_first = sel_scores(kt_diag, True)
        s_scr[...] = s_first

        def sel_body(kt, lc):
            m, acc, s_max, kt_parked = lc
            s_next, max_next = sel_scores(kt, False)
            m, acc = sel_consume(m, acc, s_max, kt_parked)
            s_scr[...] = s_next
            return m, acc, max_next, kt

        m_s, acc_s, max_last, kt_last = lax.fori_loop(0, kt_diag, sel_body, init + (max_first, kt_diag))
        _, acc_s = sel_consume(m_s, acc_s, max_last, kt_last)
        o_s = finish(acc_s)

        w0 = pl.multiple_of(jnp.maximum(q0 + Q_BLOCK - win_span, 0), Q_BLOCK)
        win_key = lax.broadcasted_iota(jnp.int32, (win_span, cols), 0)
        reach = q_loc + (q0 - w0)
        ok_w = (win_key <= reach) & (win_key > reach - WINDOW)
        v_tiles = [vwt_ref[0, g, w0 // Q_BLOCK + i] for i in range(win_span // Q_BLOCK)]
        carry = step(init, qs_bf, slope, bias_ref[g, 0:win_span, :], kw_ref[0, g, pl.ds(w0, win_span), :],
                     v_tiles, q0 - w0, ok_w)
        o_w = finish(carry[1])

        for r in range(R):
            col = (g * R + r) * N_BRANCHES
            sl = slice(r * Q_BLOCK, (r + 1) * Q_BLOCK)
            y_rows.append(gates_t[col:col + 1, :] * o_c[:, sl] + gates_t[col + 1:col + 2, :] * o_s[:, sl]
                          + gates_t[col + 2:col + 3, :] * o_w[:, sl])
    y_t = jnp.concatenate(y_rows, axis=0)
    y_t = y_t * lax.rsqrt(jnp.mean(y_t * y_t, axis=0, keepdims=True) + LN_EPS) * gn_ref[...]
    o_ref[...] = y_t.T


def _nsa(h_all, B, S, kc, vc, ks_bf, vs_tiles, kw_bf, vw_tiles, gn):
    nq = S // Q_BLOCK
    n_cmp = (S - CMP_BLOCK) // CMP_STRIDE + 1
    n_rows_c = kc.shape[2]
    n_sel = S // SEL_BLOCK
    n_top = min(SEL_TOPN, n_sel)
    n_span = S // SEL_TILE
    win_span = min(WIN_SPAN, S)
    n_bias = max(win_span, SEL_TILE)
    cols = Q_PER_KV * Q_BLOCK
    slopes = np.array([2.0 ** (-8.0 * (h + 1) / N_HEADS) for h in range(N_HEADS)], np.float32)
    slope_cols = np.repeat(slopes.reshape(N_KV, Q_PER_KV), Q_BLOCK, axis=1)[:, None, :]
    rel = (np.arange(cols)[None, :] % Q_BLOCK - np.arange(n_bias)[:, None]).astype(np.float32)
    bias = -slope_cols * rel[None]
    selmat_t = jnp.asarray(_selection_matrix(n_rows_c, n_cmp, n_sel).T, dtype=BF16)
    key_blk = (np.arange(S) // SEL_BLOCK).reshape(n_span, SEL_TILE, 1)
    emat_t = jnp.asarray((key_blk == np.arange(LANES)[None, None, :]).astype(np.float32), dtype=BF16)
    qw = Q_PER_KV * HEAD_DIM * N_KV
    per_batch = lambda *dims: pl.BlockSpec((1,) + dims, lambda b, i: (b,) + (0,) * len(dims))
    const = lambda shape: pl.BlockSpec(shape, lambda b, i: (0,) * len(shape))
    return pl.pallas_call(
        functools.partial(_nsa_kernel, n_cmp=n_cmp, n_top=n_top, win_span=win_span),
        grid=(B, nq),
        in_specs=[pl.BlockSpec((Q_BLOCK, qw), lambda b, i: (b * nq + i, COL_Q // qw)),
                  pl.BlockSpec((Q_BLOCK, LANES), lambda b, i: (b * nq + i, COL_GATES // LANES)),
                  per_batch(N_KV, n_rows_c, HEAD_DIM), per_batch(N_KV, HEAD_DIM, n_rows_c),
                  per_batch(N_KV, S, HEAD_DIM), per_batch(N_KV, nq, V_AUG_ROWS, Q_BLOCK),
                  per_batch(N_KV, S, HEAD_DIM), per_batch(N_KV, nq, V_AUG_ROWS, Q_BLOCK),
                  const((N_KV, 1, cols)), const((N_KV, n_bias, cols)), const((n_sel, n_rows_c)),
                  const((n_span, SEL_TILE, LANES)), const((NSA_WIDTH, 1))],
        out_specs=pl.BlockSpec((Q_BLOCK, NSA_WIDTH), lambda b, i: (b * nq + i, 0)),
        out_shape=jax.ShapeDtypeStruct((B * S, NSA_WIDTH), F32),
        scratch_shapes=[pltpu.VMEM((SEL_TILE, cols), F32)],
        compiler_params=_cparams(("arbitrary", "arbitrary")),
        name="nsa",
    )(h_all, h_all, kc, vc.transpose(0, 1, 3, 2), ks_bf, vs_tiles, kw_bf, vw_tiles,
      jnp.asarray(slope_cols), jnp.asarray(bias), selmat_t, emat_t, gn.reshape(NSA_WIDTH, 1))


def _out_ln_kernel(ylru_ref, ynsa_ref, x_ref, w_ref, g_ref, b_ref, o_ref, *, alpha):
    mix = jnp.dot(ylru_ref[...].astype(BF16), w_ref[0:LRU_WIDTH, :], preferred_element_type=F32)
    mix = mix + jnp.dot(ynsa_ref[...].astype(BF16), w_ref[LRU_WIDTH:, :], preferred_element_type=F32)
    o_ref[...] = _layer_norm_rows(alpha * x_ref[...] + mix, g_ref[...], b_ref[...])


def _out_ln(ylru, ynsa, xf, w_bf, g, b, alpha, tm=512):
    T = xf.shape[0]
    const = lambda shape: pl.BlockSpec(shape, lambda i: (0,) * len(shape))
    return pl.pallas_call(
        functools.partial(_out_ln_kernel, alpha=alpha),
        grid=(T // tm,),
        in_specs=[pl.BlockSpec((tm, LRU_WIDTH), lambda i: (i, 0)),
                  pl.BlockSpec((tm, NSA_WIDTH), lambda i: (i, 0)),
                  pl.BlockSpec((tm, D_MODEL), lambda i: (i, 0)),
                  const((D_MODEL, D_MODEL)), const((1, D_MODEL)), const((1, D_MODEL))],
        out_specs=pl.BlockSpec((tm, D_MODEL), lambda i: (i, 0)),
        out_shape=jax.ShapeDtypeStruct((T, D_MODEL), F32),
        compiler_params=_cparams(("arbitrary",)),
        name="out_ln",
    )(ylru, ynsa, xf, w_bf, g.reshape(1, D_MODEL), b.reshape(1, D_MODEL))


def _topk_rows(vals, order, payload, n_keep):
    n = vals.shape[1]
    kidx = lax.broadcasted_iota(jnp.int32, (n_keep, n), 0)
    never = float(1 << 20)
    kept_v = jnp.zeros((n_keep, n), F32)
    kept_p = jnp.zeros((n_keep, n), F32)
    work = vals
    for r in range(n_keep):
        mx = jnp.max(work, axis=0, keepdims=True)
        first = jnp.min(jnp.where(work == mx, order, never), axis=0, keepdims=True)
        pick = order == first
        kept_v = jnp.where(kidx == r, mx, kept_v)
        if payload is None:
            kept_p = jnp.where(kidx == r, first, kept_p)
        else:
            kept_p = jnp.where(kidx == r, jnp.max(jnp.where(pick, payload, -1.0), axis=0, keepdims=True), kept_p)
        work = jnp.where(pick, REMOVED, work)
    return kept_v, kept_p


def _candidate_rows():
    ij = [(i, 0) for i in range(PEER_TOPK)]
    for j in range(1, SUBLANES):
        ij += [(i, j) for i in range(SUBLANES)]
    ij += [(0, j) for j in range(SUBLANES, PEER_TOPK)]
    flat = np.array([i * PEER_TOPK + j for i, j in ij], np.float32)
    valid = np.array([(i + 1) * (j + 1) <= PEER_TOPK for i, j in ij])
    return flat[:, None], np.where(valid, 0.0, REMOVED).astype(np.float32)[:, None]


def _candidates(a0, a1):
    return ([(a0, a1[0:1, :])]
            + [(a0[0:SUBLANES, :], a1[j:j + 1, :]) for j in range(1, SUBLANES)]
            + [(a0[0:1, :], a1[SUBLANES:PEER_TOPK, :])])


def _peer_route_kernel(x_ref, wq_ref, sk_ref, flat_ref, pad_ref, e_ref, g_ref):
    tm = x_ref.shape[0]
    q = jnp.dot(x_ref[...].astype(BF16), wq_ref[...], preferred_element_type=F32)
    key_idx = lax.broadcasted_iota(jnp.int32, (PEER_KEYS, tm), 0).astype(F32)
    flat = jnp.broadcast_to(flat_ref[...], (flat_ref.shape[0], tm))
    experts, gates = [], []
    for h in range(PEER_HEADS):
        sv, si = [], []
        for c in range(2):
            col = (h * 2 + c) * PEER_KEY_DIM
            s_t = _dot_nt(sk_ref[c], q[:, col:col + PEER_KEY_DIM], HI)
            v, i = _topk_rows(s_t, key_idx, None, PEER_TOPK)
            sv.append(v)
            si.append(i)
        cand = jnp.concatenate([a + b for a, b in _candidates(sv[0], sv[1])], axis=0) + pad_ref[...]
        ecand = jnp.concatenate([a * float(PEER_KEYS) + b for a, b in _candidates(si[0], si[1])], axis=0)
        cv, ce = _topk_rows(cand, flat, ecand, PEER_TOPK)
        ex = jnp.exp(cv - cv[0:1, :])
        gates.append(ex / jnp.sum(ex, axis=0, keepdims=True))
        experts.append(ce)
    e_ref[...] = jnp.concatenate(experts, axis=0).astype(jnp.int32)
    g_ref[...] = jnp.concatenate(gates, axis=0).T


def _peer_route(x1, wq_bf, subkeys, tm=256):
    T = x1.shape[0]
    tm = min(tm, T)
    nq = wq_bf.shape[1]
    flat, pad = _candidate_rows()
    n_cand = flat.shape[0]
    out = lambda: pl.BlockSpec((tm, N_PAIRS), lambda i: (i, 0))
    return pl.pallas_call(
        _peer_route_kernel,
        grid=(T // tm,),
        in_specs=[pl.BlockSpec((tm, D_MODEL), lambda i: (i, 0)),
                  pl.BlockSpec((D_MODEL, nq), lambda i: (0, 0)),
                  pl.BlockSpec((2, PEER_KEYS, PEER_KEY_DIM), lambda i: (0, 0, 0)),
                  pl.BlockSpec((n_cand, 1), lambda i: (0, 0)),
                  pl.BlockSpec((n_cand, 1), lambda i: (0, 0))],
        out_specs=[pl.BlockSpec((N_PAIRS, tm), lambda i: (0, i)), out()],
        out_shape=[jax.ShapeDtypeStruct((N_PAIRS, T), jnp.int32),
                   jax.ShapeDtypeStruct((T, N_PAIRS), F32)],
        compiler_params=_cparams(("arbitrary",)),
        name="peer_route",
    )(x1, wq_bf, subkeys, jnp.asarray(flat), jnp.asarray(pad))


TOKENS_PER_MATMUL = 16
PEER_TOKENS = 2 * LANES


class _ExpertIds:
    def __init__(self, e_hbm, e_smem_halves, sems, tb):
        self.e_hbm, self.e_smem, self.sems, self.tb, self.half = e_hbm, e_smem_halves, sems, tb, tb // 2
        self.step = pl.program_id(0)
        self.n_steps = pl.num_programs(0)

    def _copy(self, block, h):
        return pltpu.make_async_copy(self.e_hbm.at[:, pl.ds(block * self.tb + h * self.half, self.half)],
                                     self.e_smem[h], self.sems.at[h])

    def start_first_block(self):
        @pl.when(self.step == 0)
        def _():
            self._copy(0, 0).start()
            self._copy(0, 1).start()

    def wait_half(self, h):
        self._copy(self.step, h).wait()

    def prefetch_half(self, h):
        @pl.when(self.step + 1 < self.n_steps)
        def _():
            self._copy(self.step + 1, h).start()

    def __call__(self, h, k, t):
        return self.e_smem[h].at[k][t - h * self.half]


def _pack_table(tab):
    return tab.astype(BF16).reshape(N_EXPERTS, D_ROWS, LANES)


def _sublane_sums(ps, roll, where, sub):
    lvl = ps
    for r in (1, 2, 4):
        m = (sub % (2 * r)) < r
        half = len(lvl) // 2
        lvl = [where(m, lvl[i], lvl[i + half]) + roll(where(m, lvl[i + half], lvl[i]), r) for i in range(half)]
    return lvl[0]


def _sum_src():
    sub = np.arange(SUBLANES)[:, None] * np.ones((1, LANES), np.int64)
    ps = [np.full((SUBLANES, LANES), 10.0 ** i) for i in range(SUBLANES)]
    out = _sublane_sums(ps, lambda v, r: np.roll(v, r, axis=0), np.where, sub)
    return [int(round(np.log10(out[s, 0] / SUBLANES))) for s in range(SUBLANES)]


_SUM_SRC = _sum_src()


def _peer_u_kernel(e_hbm, x_ref, tab_ref, gate_ref, w_ref, part_ref, zt_ref, e_smem_a, e_smem_b, sems, *, tb):
    sub = lax.broadcasted_iota(jnp.int32, (SUBLANES, LANES), 0)
    lane = lax.broadcasted_iota(jnp.int32, (N_PAIRS, tb), 1)
    ids = _ExpertIds(e_hbm, (e_smem_a, e_smem_b), sems, tb)
    ids.start_first_block()

    def gather_dots(h, t, slot):
        xt = x_ref[t]
        for grp in range(N_PAIRS // SUBLANES):
            prods = [None] * SUBLANES
            for s in range(SUBLANES):
                prods[_SUM_SRC[s]] = tab_ref[ids(h, grp * SUBLANES + s, t)].astype(F32) * xt
            part_ref[slot, grp * SUBLANES:(grp + 1) * SUBLANES, :] = _sublane_sums(
                prods, lambda v, r: pltpu.roll(v, r, axis=0), jnp.where, sub)

    def lane_sums(t0):
        z0 = jnp.sum(part_ref[0], axis=1, keepdims=True)
        z1 = jnp.sum(part_ref[1], axis=1, keepdims=True)
        zt_ref[...] = jnp.where(lane == t0, z0, jnp.where(lane == t0 + 1, z1, zt_ref[...]))

    part_ref[...] = jnp.zeros_like(part_ref)
    zt_ref[...] = jnp.zeros_like(zt_ref)

    def two_tokens(h, i, carry):
        lane_sums(2 * i - 2)
        gather_dots(h, 2 * i, 0)
        gather_dots(h, 2 * i + 1, 1)
        return carry

    for h in range(2):
        ids.wait_half(h)
        lax.fori_loop(h * tb // 4, (h + 1) * tb // 4, functools.partial(two_tokens, h), 0)
        ids.prefetch_half(h)
    lane_sums(tb - 2)
    w_ref[...] = jax.nn.gelu(zt_ref[...].T) * gate_ref[...]


def _peer_u(experts_t, x3, tab, gates, tb=PEER_TOKENS):
    T = x3.shape[0]
    vm = lambda: pl.BlockSpec((tb, N_PAIRS), lambda i: (i, 0))
    return pl.pallas_call(
        functools.partial(_peer_u_kernel, tb=tb),
        grid=(T // tb,),
        in_specs=[pl.BlockSpec(memory_space=pl.ANY),
                  pl.BlockSpec((tb, D_ROWS, LANES), lambda i: (i, 0, 0)),
                  pl.BlockSpec((N_EXPERTS, D_ROWS, LANES), lambda i: (0, 0, 0), pipeline_mode=pl.Buffered(1)),
                  vm()],
        out_specs=vm(),
        out_shape=jax.ShapeDtypeStruct((T, N_PAIRS), F32),
        scratch_shapes=[pltpu.VMEM((2, N_PAIRS, LANES), F32), pltpu.VMEM((N_PAIRS, tb), F32),
                        pltpu.SMEM((N_PAIRS, tb // 2), jnp.int32), pltpu.SMEM((N_PAIRS, tb // 2), jnp.int32),
                        pltpu.SemaphoreType.DMA((2,))],
        compiler_params=_cparams(("arbitrary",)),
        name="peer_u",
    )(experts_t, x3, tab, gates)


def _expand_matrix():
    k = np.arange(N_PAIRS)[:, None]
    c = np.arange(N_PAIRS * D_ROWS)[None, :]
    return (c // D_ROWS == k).astype(np.float32)


def _peer_v_kernel(e_hbm, w_ref, x_ref, tab_ref, expand_ref, g_ref, b_ref, o_ref, y_ref, wide_ref,
                   e_smem_a, e_smem_b, sems, *, tb, alpha):
    n_col = N_PAIRS * D_ROWS
    ids = _ExpertIds(e_hbm, (e_smem_a, e_smem_b), sems, tb)
    ids.start_first_block()
    diag = (lax.broadcasted_iota(jnp.int32, (D_ROWS, n_col), 1) % D_ROWS
            == lax.broadcasted_iota(jnp.int32, (D_ROWS, n_col), 0))
    part_row = lax.broadcasted_iota(jnp.int32, (SUBLANES, LANES), 0)

    def expand_tokens(i, carry):
        t0 = pl.multiple_of(i * TOKENS_PER_MATMUL, TOKENS_PER_MATMUL)
        parts = []
        for j in range(TOKENS_PER_MATMUL):
            wrow = w_ref[pl.ds(t0 + j, 1), :]
            hi = wrow.astype(BF16).astype(F32)
            mid = (wrow - hi).astype(BF16).astype(F32)
            lo = ((wrow - hi) - mid).astype(BF16).astype(F32)
            bc = lambda v: jnp.broadcast_to(v, (SUBLANES, LANES))
            parts.append(jnp.where(part_row == 0, bc(hi), jnp.where(part_row == 1, bc(mid),
                                                                    jnp.where(part_row == 2, bc(lo), 0.0))))
        lhs = jnp.concatenate(parts, axis=0).astype(BF16)
        wide = jnp.dot(lhs, expand_ref[...], preferred_element_type=F32)
        wide_ref[pl.ds(t0, TOKENS_PER_MATMUL)] = wide.reshape(TOKENS_PER_MATMUL, SUBLANES, n_col)
        return carry

    lax.fori_loop(0, tb // TOKENS_PER_MATMUL, expand_tokens, 0)

    def left_operand(t):
        rows = [jnp.where(diag, jnp.broadcast_to(wide_ref[t, p:p + 1, :], (D_ROWS, n_col)), 0.0) for p in range(3)]
        return jnp.concatenate(rows, axis=0).astype(BF16)

    def weighted_sum(h, t, a):
        rows = [tab_ref[ids(h, k, t)].astype(F32) for k in range(N_PAIRS)]
        rhs = jnp.concatenate(rows, axis=0).astype(BF16)
        y3 = jnp.dot(a, rhs, preferred_element_type=F32)
        return (y3[0:D_ROWS] + y3[D_ROWS:2 * D_ROWS]) + y3[2 * D_ROWS:]

    tokens_per_step = 16

    def some_tokens(h, i, carry):
        ts = [tokens_per_step * i + j for j in range(tokens_per_step)]
        ys = [weighted_sum(h, t, left_operand(t)) for t in ts]
        for t, y in zip(ts, ys):
            y_ref[t] = y
        return carry

    for h in range(2):
        ids.wait_half(h)
        lax.fori_loop(h * tb // (2 * tokens_per_step), (h + 1) * tb // (2 * tokens_per_step),
                      functools.partial(some_tokens, h), 0)
        ids.prefetch_half(h)
    z = alpha * x_ref[...] + y_ref[...]
    inv_d = 1.0 / D_MODEL
    mu = jnp.sum(jnp.sum(z, axis=2, keepdims=True), axis=1, keepdims=True) * inv_d
    zc = z - mu
    var = jnp.sum(jnp.sum(zc * zc, axis=2, keepdims=True), axis=1, keepdims=True) * inv_d
    o_ref[...] = zc * lax.rsqrt(var + LN_EPS) * g_ref[...] + b_ref[...]


def _peer_v(experts_t, wgt, x3, tab, g, b, alpha, tb=PEER_TOKENS):
    T = x3.shape[0]
    return pl.pallas_call(
        functools.partial(_peer_v_kernel, tb=tb, alpha=alpha),
        grid=(T // tb,),
        in_specs=[pl.BlockSpec(memory_space=pl.ANY),
                  pl.BlockSpec((tb, N_PAIRS), lambda i: (i, 0)),
                  pl.BlockSpec((tb, D_ROWS, LANES), lambda i: (i, 0, 0)),
                  pl.BlockSpec((N_EXPERTS, D_ROWS, LANES), lambda i: (0, 0, 0), pipeline_mode=pl.Buffered(1)),
                  pl.BlockSpec((N_PAIRS, N_PAIRS * D_ROWS), lambda i: (0, 0)),
                  pl.BlockSpec((1, D_ROWS, LANES), lambda i: (0, 0, 0)),
                  pl.BlockSpec((1, D_ROWS, LANES), lambda i: (0, 0, 0))],
        out_specs=pl.BlockSpec((tb, D_ROWS, LANES), lambda i: (i, 0, 0)),
        out_shape=jax.ShapeDtypeStruct((T, D_ROWS, LANES), F32),
        scratch_shapes=[pltpu.VMEM((tb, D_ROWS, LANES), F32), pltpu.VMEM((tb, SUBLANES, N_PAIRS * D_ROWS), F32),
                        pltpu.SMEM((N_PAIRS, tb // 2), jnp.int32), pltpu.SMEM((N_PAIRS, tb // 2), jnp.int32),
                        pltpu.SemaphoreType.DMA((2,))],
        compiler_params=_cparams(("arbitrary",)),
        name="peer_v",
    )(experts_t, wgt, x3, tab, jnp.asarray(_expand_matrix(), dtype=BF16),
      g.reshape(1, D_ROWS, LANES), b.reshape(1, D_ROWS, LANES))


def kernel(x, w_in, b_in, conv_w, conv_b, lru_wa, lru_ba, lru_wx, lru_bx, lru_lambda, cmp_pos_k, cmpk_w1, cmpk_b1, cmpk_w2, cmpk_b2, cmp_pos_v, cmpv_w1, cmpv_b1, cmpv_w2, cmpv_b2, gn_lru_g, gn_nsa_g, w_out, ln1_g, ln1_b, peer_wq, peer_subkeys, peer_u, peer_v, ln2_g, ln2_b):
    B, S, D = x.shape
    T = B * S
    depth = w_in.shape[0]
    alpha = (2 * depth) ** 0.25
    xf = x.reshape(T, D)
    for l in range(depth):
        w_in_bf = jnp.pad(w_in[l], ((0, 0), (0, IN_COLS_PAD - IN_COLS))).astype(BF16)
        b_in_row = jnp.pad(b_in[l], (0, IN_COLS_PAD - IN_COLS)).reshape(1, IN_COLS_PAD)
        h_all, kraw, kbf, vtiles = _in_proj(xf, w_in_bf, b_in_row, B, S)

        y_lru = _rglru(h_all, B, S, conv_w[l], conv_b[l], _block_diag(lru_wa[l]), lru_ba[l],
                       _block_diag(lru_wx[l]), lru_bx[l], lru_lambda[l], gn_lru_g[l])

        w1 = jnp.stack([cmpk_w1[l], cmpv_w1[l]])
        w1_pos = w1.reshape(2, 2, CMP_STRIDE, HEAD_DIM, HEAD_DIM)
        w1l = jnp.concatenate([w1_pos[:, 0], w1_pos[:, 1]], axis=-1)
        pos = jnp.stack([cmp_pos_k[l], cmp_pos_v[l]]).reshape(2, 1, CMP_BLOCK * HEAD_DIM)
        posflat = jnp.broadcast_to(pos, (2, SUBLANES, CMP_BLOCK * HEAD_DIM))
        kcvc = _compress(kraw, w1l, w1, posflat,
                         jnp.stack([cmpk_b1[l], cmpv_b1[l]]).reshape(2, 1, HEAD_DIM),
                         jnp.stack([cmpk_w2[l], cmpv_w2[l]]),
                         jnp.stack([cmpk_b2[l], cmpv_b2[l]]).reshape(2, 1, HEAD_DIM))
        kcvc = kcvc.reshape(2, B, N_KV, S // CMP_STRIDE, HEAD_DIM)
        y_nsa = _nsa(h_all, B, S, kcvc[0], kcvc[1], kbf[0], vtiles[0], kbf[1], vtiles[1], gn_nsa_g[l])

        x1 = _out_ln(y_lru, y_nsa, xf, w_out[l].astype(BF16), ln1_g[l], ln1_b[l], alpha)

        experts_t, gates = _peer_route(x1, peer_wq[l].astype(BF16), peer_subkeys[l])
        x3 = x1.reshape(T, D_ROWS, LANES)
        wgt = _peer_u(experts_t, x3, _pack_table(peer_u[l]), gates)
        xf = _peer_v(experts_t, wgt, x3, _pack_table(peer_v[l]), ln2_g[l], ln2_b[l], alpha).reshape(T, D)
    return xf.reshape(B, S, D)
```

```python
import functools

import numpy as np
import jax
import jax.numpy as jnp
from jax import lax
from jax.experimental import pallas as pl
from jax.experimental.pallas import tpu as pltpu

D_MODEL = 1024
LRU_WIDTH = 512
LRU_BLOCKS = 8
LRU_BLOCK_DIM = LRU_WIDTH // LRU_BLOCKS
CONV_WIDTH = 4
LRU_C = 8.0
N_HEADS = 8
HEAD_DIM = 64
N_KV = 2
Q_PER_KV = N_HEADS // N_KV
NSA_WIDTH = N_HEADS * HEAD_DIM
KV_WIDTH = N_KV * HEAD_DIM
N_BRANCHES = 3
CMP_BLOCK = 32
CMP_STRIDE = 16
SEL_BLOCK = 64
SEL_TOPN = 16
WINDOW = 512
Q_BLOCK = 128
FORCE_BONUS = 1e4
NEG = -1e30
REMOVED = -3.0e38
PEER_HEADS = 8
PEER_KEYS = 128
PEER_TOPK = 16
PEER_KEY_DIM = 128
N_EXPERTS = PEER_KEYS * PEER_KEYS
N_PAIRS = PEER_HEADS * PEER_TOPK
LN_EPS = 1e-5
IN_SPLITS = (LRU_WIDTH, LRU_WIDTH, NSA_WIDTH) + (KV_WIDTH,) * 6 + (N_HEADS * N_BRANCHES,)
IN_COLS = sum(IN_SPLITS)

LANES = 128
SUBLANES = 8
VMEM_LIMIT_BYTES = 56 * 1024 * 1024

IN_COLS_PAD = -(-IN_COLS // LANES) * LANES
COL_LRU_X = 0
COL_LRU_GATE = LRU_WIDTH
COL_Q = 2 * LRU_WIDTH
COL_KV = COL_Q + NSA_WIDTH
COL_GATES = COL_KV + 6 * KV_WIDTH

HI = lax.Precision.HIGHEST
F32 = jnp.float32
BF16 = jnp.bfloat16

D_ROWS = D_MODEL // LANES


def _cparams(sem):
    return pltpu.CompilerParams(dimension_semantics=sem, vmem_limit_bytes=VMEM_LIMIT_BYTES)


def _dot_nt(a, b, precision=None):
    return lax.dot_general(a, b, (((1,), (1,)), ((), ())), precision=precision,
                           preferred_element_type=F32)


def _split2(a):
    hi = a.astype(BF16)
    return hi, (a - hi.astype(F32)).astype(BF16)


def _dot_split(a, b):
    a_hi, a_lo = _split2(a)
    b_hi, b_lo = _split2(b)
    dot = lambda u, v: jnp.dot(u, v, preferred_element_type=F32)
    return dot(a_hi, b_hi) + (dot(a_hi, b_lo) + dot(a_lo, b_hi))


def _layer_norm_rows(z, g, b):
    mu = jnp.mean(z, axis=-1, keepdims=True)
    zc = z - mu
    var = jnp.mean(zc * zc, axis=-1, keepdims=True)
    return zc * lax.rsqrt(var + LN_EPS) * g + b


V_AUG_ROWS = HEAD_DIM + 16


def _in_proj_kernel(x_ref, w_ref, b_ref, o_ref, kraw_ref, kbf_ref, vt_ref):
    h = jnp.dot(x_ref[...].astype(BF16), w_ref[...], preferred_element_type=F32) + b_ref[...]
    o_ref[...] = h
    group_cols = lambda col, g: slice(col + g * HEAD_DIM, col + (g + 1) * HEAD_DIM)
    for j in range(2):
        for g in range(N_KV):
            kraw_ref[j, 0, g] = h[:, group_cols(COL_KV + j * KV_WIDTH, g)]
            kbf_ref[j, 0, g] = h[:, group_cols(COL_KV + (2 + 2 * j) * KV_WIDTH, g)].astype(BF16)
    ones_row = (lax.broadcasted_iota(jnp.int32, (V_AUG_ROWS - HEAD_DIM, Q_BLOCK), 0) == 0).astype(F32)
    for j in range(2):
        col = COL_KV + (3 + 2 * j) * KV_WIDTH
        for qb in range(h.shape[0] // Q_BLOCK):
            v_t = h[qb * Q_BLOCK:(qb + 1) * Q_BLOCK, col:col + KV_WIDTH].T
            for g in range(N_KV):
                tile = jnp.concatenate([v_t[g * HEAD_DIM:(g + 1) * HEAD_DIM, :], ones_row], axis=0)
                vt_ref[j, 0, g, qb] = tile.astype(BF16)


def _in_proj(xf, w_bf, b_row, B, S, tm=512):
    T = xf.shape[0]
    tm = min(tm, S)
    nt = S // tm
    per_tile = lambda *dims: pl.BlockSpec((2, 1, N_KV) + dims, lambda i: (0, i // nt, 0, i % nt) + (0,) * (len(dims) - 1))
    return pl.pallas_call(
        _in_proj_kernel,
        grid=(T // tm,),
        in_specs=[pl.BlockSpec((tm, D_MODEL), lambda i: (i, 0)),
                  pl.BlockSpec((D_MODEL, IN_COLS_PAD), lambda i: (0, 0)),
                  pl.BlockSpec((1, IN_COLS_PAD), lambda i: (0, 0))],
        out_specs=[pl.BlockSpec((tm, IN_COLS_PAD), lambda i: (i, 0)),
                   per_tile(tm, HEAD_DIM), per_tile(tm, HEAD_DIM),
                   per_tile(tm // Q_BLOCK, V_AUG_ROWS, Q_BLOCK)],
        out_shape=[jax.ShapeDtypeStruct((T, IN_COLS_PAD), F32),
                   jax.ShapeDtypeStruct((2, B, N_KV, S, HEAD_DIM), F32),
                   jax.ShapeDtypeStruct((2, B, N_KV, S, HEAD_DIM), BF16),
                   jax.ShapeDtypeStruct((2, B, N_KV, S // Q_BLOCK, V_AUG_ROWS, Q_BLOCK), BF16)],
        compiler_params=_cparams(("arbitrary",)),
        name="in_proj",
    )(xf, w_bf, b_row)


def _rglru_kernel(x_ref, gate_ref, cw_ref, cb_ref, wa_ref, ba_ref, wx_ref, bx_ref, lam_ref, gn_ref,
                  o_ref, xprev_ref, h_ref, *, ts):
    i = pl.program_id(1)

    @pl.when(i == 0)
    def _():
        xprev_ref[...] = jnp.zeros_like(xprev_ref)
        h_ref[...] = jnp.zeros_like(h_ref)

    xb = x_ref[...]
    xp = xprev_ref[...]
    row = lax.broadcasted_iota(jnp.int32, xb.shape, 0)
    xc = cw_ref[CONV_WIDTH - 1:CONV_WIDTH, :] * xb + cb_ref[...]
    for j in range(1, CONV_WIDTH):
        shifted = jnp.where(row >= j, pltpu.roll(xb, j, axis=0), pltpu.roll(xp, j, axis=0))
        xc = xc + cw_ref[CONV_WIDTH - 1 - j:CONV_WIDTH - j, :] * shifted
    xprev_ref[...] = xb

    r = jax.nn.sigmoid(_dot_split(xc, wa_ref[...]) + ba_ref[...])
    ig = jax.nn.sigmoid(_dot_split(xc, wx_ref[...]) + bx_ref[...])
    lam = lam_ref[...]
    softplus_neg_lam = jnp.maximum(-lam, 0.0) + jnp.log1p(jnp.exp(-jnp.abs(lam)))
    log_a = -LRU_C * r * softplus_neg_lam
    a = jnp.exp(log_a)
    th = jnp.tanh(log_a)
    u = jnp.sqrt(-2.0 * th / (1.0 - th)) * (ig * xc)

    d = 1
    while d < ts:
        keep = row >= d
        a_sh = pltpu.roll(a, d, axis=0)
        u_sh = pltpu.roll(u, d, axis=0)
        u = jnp.where(keep, a * u_sh + u, u)
        a = jnp.where(keep, a * a_sh, a)
        d *= 2
    h = u + a * h_ref[0:1, :]
    h_ref[0:1, :] = h[ts - 1:ts, :]

    y = h * jax.nn.gelu(gate_ref[...])
    y = y * lax.rsqrt(jnp.mean(y * y, axis=-1, keepdims=True) + LN_EPS) * gn_ref[...]
    o_ref[...] = y


def _rglru(h_all, B, S, cw, cb, wa_bd, ba, wx_bd, bx, lam, gn, ts=512):
    ts = min(ts, S)
    nt = S // ts
    C = LRU_WIDTH
    row = lambda v: v.reshape(1, C)
    const = lambda shape: pl.BlockSpec(shape, lambda b, i: (0,) * len(shape))
    return pl.pallas_call(
        functools.partial(_rglru_kernel, ts=ts),
        grid=(B, nt),
        in_specs=[pl.BlockSpec((ts, C), lambda b, i: (b * nt + i, COL_LRU_X // C)),
                  pl.BlockSpec((ts, C), lambda b, i: (b * nt + i, COL_LRU_GATE // C)),
                  const((CONV_WIDTH, C)), const((1, C)), const((C, C)), const((1, C)),
                  const((C, C)), const((1, C)), const((1, C)), const((1, C))],
        out_specs=pl.BlockSpec((ts, C), lambda b, i: (b * nt + i, 0)),
        out_shape=jax.ShapeDtypeStruct((B * S, C), F32),
        scratch_shapes=[pltpu.VMEM((ts, C), F32), pltpu.VMEM((SUBLANES, C), F32)],
        compiler_params=_cparams(("arbitrary", "arbitrary")),
        name="rglru",
    )(h_all, h_all, cw, row(cb), wa_bd, row(ba), wx_bd, row(bx), row(lam), row(gn))


def _block_diag(w):
    n, d, e = w.shape
    eye = jnp.eye(n, dtype=w.dtype)
    return (w[:, :, None, :] * eye[:, None, :, None]).reshape(n * d, n * e)


def _compress_kernel(k_ref, w1l_ref, w1_ref, pos_ref, b1_ref, w2_ref, b2_ref, o_ref):
    nr = o_ref.shape[2]
    ab = jnp.zeros((nr, 2 * HEAD_DIM), F32)
    for l in range(CMP_STRIDE):
        k_l = k_ref[0, 0, 0, pl.ds(l, nr, stride=CMP_STRIDE), :]
        ab = ab + jnp.dot(k_l, w1l_ref[0, l], precision=HI, preferred_element_type=F32)
    ab_next = pltpu.roll(ab, nr - 1, axis=0)
    pre = ab[:, :HEAD_DIM] + ab_next[:, HEAD_DIM:]
    posc = jnp.dot(pos_ref[0], w1_ref[0], precision=HI, preferred_element_type=F32)[0:1, :]
    h1 = jax.nn.gelu(pre + posc + b1_ref[0])
    o_ref[0, 0] = jnp.dot(h1, w2_ref[0], precision=HI, preferred_element_type=F32) + b2_ref[0]


def _compress(kraw, w1l, w1, posflat, b1, w2, b2):
    _, B, G, S, _ = kraw.shape
    nr = S // CMP_STRIDE
    fan = CMP_BLOCK * HEAD_DIM
    wspec = lambda shape: pl.BlockSpec((1,) + shape, lambda s, i: (s,) + (0,) * len(shape))
    return pl.pallas_call(
        _compress_kernel,
        grid=(2, B * G),
        in_specs=[pl.BlockSpec((1, 1, 1, S, HEAD_DIM), lambda s, i: (s, i // G, i % G, 0, 0)),
                  wspec((CMP_STRIDE, HEAD_DIM, 2 * HEAD_DIM)), wspec((fan, HEAD_DIM)), wspec((SUBLANES, fan)),
                  wspec((1, HEAD_DIM)), wspec((HEAD_DIM, HEAD_DIM)), wspec((1, HEAD_DIM))],
        out_specs=pl.BlockSpec((1, 1, nr, HEAD_DIM), lambda s, i: (s, i, 0, 0)),
        out_shape=jax.ShapeDtypeStruct((2, B * G, nr, HEAD_DIM), F32),
        compiler_params=_cparams(("arbitrary", "arbitrary")),
        name="compress",
    )(kraw, w1l, w1, posflat, b1, w2, b2)


def _selection_matrix(n_rows, n_cmp, n_sel):
    cs = np.arange(n_rows)[:, None] * CMP_STRIDE
    ss = np.arange(n_sel)[None, :] * SEL_BLOCK
    ov = np.clip(np.minimum(cs + CMP_BLOCK, ss + SEL_BLOCK) - np.maximum(cs, ss), 0, None)
    ov = (ov / CMP_STRIDE).astype(np.float32)
    ov[n_cmp:] = 0.0
    return ov


SEL_TILE = 4 * Q_BLOCK
WIN_SPAN = WINDOW + Q_BLOCK


def _nsa_kernel(q_ref, gt_ref, kc_ref, vct_ref, ks_ref, vst_ref, kw_ref, vwt_ref, slope_ref, bias_ref,
                selmat_ref, emat_ref, gn_ref, o_ref, s_scr, *, n_cmp, n_top, win_span):
    qt = pl.program_id(1)
    q0 = qt * Q_BLOCK
    R = Q_PER_KV
    cols = R * Q_BLOCK
    n_rows_c = kc_ref.shape[2]
    n_sel = selmat_ref.shape[0]
    q_loc = lax.broadcasted_iota(jnp.int32, (1, cols), 1) % Q_BLOCK
    t_col = q0 + q_loc
    gates_t = jax.nn.sigmoid(gt_ref[...]).T

    def step(carry, qs_bf, slope, bias, k_bf, v_tiles, offset, ok):
        m, acc = carry
        s = jnp.where(ok, _dot_nt(k_bf, qs_bf) + bias, NEG)
        c = slope * (-offset).astype(F32)
        m_new = jnp.maximum(m, jnp.max(s, axis=0, keepdims=True) + c)
        p = jnp.exp(s + (c - m_new)).astype(BF16)
        acc = jnp.exp(m - m_new) * acc
        for i, v in enumerate(v_tiles):
            acc = acc + jnp.dot(v, p[i * Q_BLOCK:(i + 1) * Q_BLOCK, :], preferred_element_type=F32)
        return m_new, acc

    init = (jnp.full((1, cols), NEG, F32), jnp.zeros((V_AUG_ROWS, cols), F32))
    finish = lambda acc: acc[0:HEAD_DIM, :] / acc[HEAD_DIM:HEAD_DIM + 1, :]
    y_rows = []
    for g in range(N_KV):
        qg = q_ref[:, g * R * HEAD_DIM:(g + 1) * R * HEAD_DIM] * (HEAD_DIM ** -0.5)
        qs = jnp.concatenate([qg[:, r * HEAD_DIM:(r + 1) * HEAD_DIM] for r in range(R)], axis=0)
        qs_bf = qs.astype(BF16)
        slope = slope_ref[g]

        kc = kc_ref[0, g]
        c_idx = lax.broadcasted_iota(jnp.int32, (n_rows_c, 1), 0)
        cmp_end = c_idx * CMP_STRIDE + (CMP_BLOCK - 1)
        valid_c = (cmp_end <= t_col) & (c_idx < n_cmp)
        q_hi, q_lo = _split2(qs)
        k_hi, k_lo = _split2(kc)
        s = _dot_nt(k_hi, q_hi) + (_dot_nt(k_hi, q_lo) + _dot_nt(k_lo, q_hi))
        s = s - slope * jnp.abs(t_col - cmp_end).astype(F32)
        s = jnp.where(valid_c, s, NEG)
        e = jnp.exp(s - jnp.max(s, axis=0, keepdims=True))
        p_c = e / jnp.sum(e, axis=0, keepdims=True)
        p_c = p_c * (t_col >= CMP_BLOCK - 1).astype(F32)
        o_c = jnp.dot(vct_ref[0, g].astype(BF16), p_c.astype(BF16), preferred_element_type=F32)

        p_sum = p_c[:, 0:Q_BLOCK]
        for r in range(1, R):
            p_sum = p_sum + p_c[:, r * Q_BLOCK:(r + 1) * Q_BLOCK]
        p_hi, p_lo = _split2(p_sum)
        dot = lambda a, b: jnp.dot(a, b, preferred_element_type=F32)
        imp_t = dot(selmat_ref[...], p_hi) + dot(selmat_ref[...], p_lo)
        j_idx = lax.broadcasted_iota(jnp.int32, (n_sel, Q_BLOCK), 0)
        j_f = j_idx.astype(F32)
        cur = t_col[:, 0:Q_BLOCK] // SEL_BLOCK
        forced = (j_idx == 0) | (j_idx == cur) | (j_idx == cur - 1)
        work = jnp.where(j_idx <= cur, imp_t + jnp.where(forced, FORCE_BONUS, 0.0), NEG)
        msel_t = jnp.zeros((n_sel, Q_BLOCK), F32)
        for _ in range(n_top):
            mx = jnp.max(work, axis=0, keepdims=True)
            first = jnp.min(jnp.where(work == mx, j_f, float(LANES)), axis=0, keepdims=True)
            pick = j_f == first
            msel_t = jnp.where(pick, 1.0, msel_t)
            work = jnp.where(pick, REMOVED, work)
        if n_sel < LANES:
            msel_t = jnp.concatenate([msel_t, jnp.zeros((LANES - n_sel, Q_BLOCK), F32)], axis=0)
        msel_bf = msel_t.astype(BF16)

        sel_key = lax.broadcasted_iota(jnp.int32, (SEL_TILE, cols), 0)
        bias_sel = bias_ref[g, 0:SEL_TILE, :]

        def sel_scores(kt, causal):
            k0 = pl.multiple_of(kt * SEL_TILE, SEL_TILE)
            ch = jnp.dot(emat_ref[kt], msel_bf, preferred_element_type=F32)
            ok = jnp.concatenate([ch] * R, axis=1) > 0.5
            if causal:
                ok = ok & (sel_key <= q_loc + (q0 - k0))
            s = jnp.where(ok, _dot_nt(ks_ref[0, g, pl.ds(k0, SEL_TILE), :], qs_bf) + bias_sel, NEG)
            return s, jnp.max(s, axis=0, keepdims=True)

        def sel_consume(m, acc, s_max, kt):
            c = slope * (kt * SEL_TILE - q0).astype(F32)
            m_new = jnp.maximum(m, s_max + c)
            p = jnp.exp(s_scr[...] + (c - m_new)).astype(BF16)
            acc = jnp.exp(m - m_new) * acc
            for i in range(SEL_TILE // Q_BLOCK):
                acc = acc + jnp.dot(vst_ref[0, g, kt * (SEL_TILE // Q_BLOCK) + i],
                                    p[i * Q_BLOCK:(i + 1) * Q_BLOCK, :], preferred_element_type=F32)
            return m_new, acc

        kt_diag = q0 // SEL_TILE
        s_first, max_first = sel_scores(kt_diag, True)
        s_scr[...] = s_first

        def sel_body(kt, lc):
            m, acc, s_max, kt_parked = lc
            s_next, max_next = sel_scores(kt, False)
            m, acc = sel_consume(m, acc, s_max, kt_parked)
            s_scr[...] = s_next
            return m, acc, max_next, kt

        m_s, acc_s, max_last, kt_last = lax.fori_loop(0, kt_diag, sel_body, init + (max_first, kt_diag))
        _, acc_s = sel_consume(m_s, acc_s, max_last, kt_last)
        o_s = finish(acc_s)

        w0 = pl.multiple_of(jnp.maximum(q0 + Q_BLOCK - win_span, 0), Q_BLOCK)
        win_key = lax.broadcasted_iota(jnp.int32, (win_span, cols), 0)
        reach = q_loc + (q0 - w0)
        ok_w = (win_key <= reach) & (win_key > reach - WINDOW)
        v_tiles = [vwt_ref[0, g, w0 // Q_BLOCK + i] for i in range(win_span // Q_BLOCK)]
        carry = step(init, qs_bf, slope, bias_ref[g, 0:win_span, :], kw_ref[0, g, pl.ds(w0, win_span), :],
                     v_tiles, q0 - w0, ok_w)
        o_w = finish(carry[1])

        for r in range(R):
            col = (g * R + r) * N_BRANCHES
            sl = slice(r * Q_BLOCK, (r + 1) * Q_BLOCK)
            y_rows.append(gates_t[col:col + 1, :] * o_c[:, sl] + gates_t[col + 1:col + 2, :] * o_s[:, sl]
                          + gates_t[col + 2:col + 3, :] * o_w[:, sl])
    y_t = jnp.concatenate(y_rows, axis=0)
    y_t = y_t * lax.rsqrt(jnp.mean(y_t * y_t, axis=0, keepdims=True) + LN_EPS) * gn_ref[...]
    o_ref[...] = y_t.T


def _nsa(h_all, B, S, kc, vc, ks_bf, vs_tiles, kw_bf, vw_tiles, gn):
    nq = S // Q_BLOCK
    n_cmp = (S - CMP_BLOCK) // CMP_STRIDE + 1
    n_rows_c = kc.shape[2]
    n_sel = S // SEL_BLOCK
    n_top = min(SEL_TOPN, n_sel)
    n_span = S // SEL_TILE
    win_span = min(WIN_SPAN, S)
    n_bias = max(win_span, SEL_TILE)
    cols = Q_PER_KV * Q_BLOCK
    slopes = np.array([2.0 ** (-8.0 * (h + 1) / N_HEADS) for h in range(N_HEADS)], np.float32)
    slope_cols = np.repeat(slopes.reshape(N_KV, Q_PER_KV), Q_BLOCK, axis=1)[:, None, :]
    rel = (np.arange(cols)[None, :] % Q_BLOCK - np.arange(n_bias)[:, None]).astype(np.float32)
    bias = -slope_cols * rel[None]
    selmat_t = jnp.asarray(_selection_matrix(n_rows_c, n_cmp, n_sel).T, dtype=BF16)
    key_blk = (np.arange(S) // SEL_BLOCK).reshape(n_span, SEL_TILE, 1)
    emat_t = jnp.asarray((key_blk == np.arange(LANES)[None, None, :]).astype(np.float32), dtype=BF16)
    qw = Q_PER_KV * HEAD_DIM * N_KV
    per_batch = lambda *dims: pl.BlockSpec((1,) + dims, lambda b, i: (b,) + (0,) * len(dims))
    const = lambda shape: pl.BlockSpec(shape, lambda b, i: (0,) * len(shape))
    return pl.pallas_call(
        functools.partial(_nsa_kernel, n_cmp=n_cmp, n_top=n_top, win_span=win_span),
        grid=(B, nq),
        in_specs=[pl.BlockSpec((Q_BLOCK, qw), lambda b, i: (b * nq + i, COL_Q // qw)),
                  pl.BlockSpec((Q_BLOCK, LANES), lambda b, i: (b * nq + i, COL_GATES // LANES)),
                  per_batch(N_KV, n_rows_c, HEAD_DIM), per_batch(N_KV, HEAD_DIM, n_rows_c),
                  per_batch(N_KV, S, HEAD_DIM), per_batch(N_KV, nq, V_AUG_ROWS, Q_BLOCK),
                  per_batch(N_KV, S, HEAD_DIM), per_batch(N_KV, nq, V_AUG_ROWS, Q_BLOCK),
                  const((N_KV, 1, cols)), const((N_KV, n_bias, cols)), const((n_sel, n_rows_c)),
                  const((n_span, SEL_TILE, LANES)), const((NSA_WIDTH, 1))],
        out_specs=pl.BlockSpec((Q_BLOCK, NSA_WIDTH), lambda b, i: (b * nq + i, 0)),
        out_shape=jax.ShapeDtypeStruct((B * S, NSA_WIDTH), F32),
        scratch_shapes=[pltpu.VMEM((SEL_TILE, cols), F32)],
        compiler_params=_cparams(("arbitrary", "arbitrary")),
        name="nsa",
    )(h_all, h_all, kc, vc.transpose(0, 1, 3, 2), ks_bf, vs_tiles, kw_bf, vw_tiles,
      jnp.asarray(slope_cols), jnp.asarray(bias), selmat_t, emat_t, gn.reshape(NSA_WIDTH, 1))


def _out_ln_kernel(ylru_ref, ynsa_ref, x_ref, w_ref, g_ref, b_ref, o_ref, *, alpha):
    mix = jnp.dot(ylru_ref[...].astype(BF16), w_ref[0:LRU_WIDTH, :], preferred_element_type=F32)
    mix = mix + jnp.dot(ynsa_ref[...].astype(BF16), w_ref[LRU_WIDTH:, :], preferred_element_type=F32)
    o_ref[...] = _layer_norm_rows(alpha * x_ref[...] + mix, g_ref[...], b_ref[...])


def _out_ln(ylru, ynsa, xf, w_bf, g, b, alpha, tm=512):
    T = xf.shape[0]
    const = lambda shape: pl.BlockSpec(shape, lambda i: (0,) * len(shape))
    return pl.pallas_call(
        functools.partial(_out_ln_kernel, alpha=alpha),
        grid=(T // tm,),
        in_specs=[pl.BlockSpec((tm, LRU_WIDTH), lambda i: (i, 0)),
                  pl.BlockSpec((tm, NSA_WIDTH), lambda i: (i, 0)),
                  pl.BlockSpec((tm, D_MODEL), lambda i: (i, 0)),
                  const((D_MODEL, D_MODEL)), const((1, D_MODEL)), const((1, D_MODEL))],
        out_specs=pl.BlockSpec((tm, D_MODEL), lambda i: (i, 0)),
        out_shape=jax.ShapeDtypeStruct((T, D_MODEL), F32),
        compiler_params=_cparams(("arbitrary",)),
        name="out_ln",
    )(ylru, ynsa, xf, w_bf, g.reshape(1, D_MODEL), b.reshape(1, D_MODEL))


def _topk_rows(vals, order, payload, n_keep):
    n = vals.shape[1]
    kidx = lax.broadcasted_iota(jnp.int32, (n_keep, n), 0)
    never = float(1 << 20)
    kept_v = jnp.zeros((n_keep, n), F32)
    kept_p = jnp.zeros((n_keep, n), F32)
    work = vals
    for r in range(n_keep):
        mx = jnp.max(work, axis=0, keepdims=True)
        first = jnp.min(jnp.where(work == mx, order, never), axis=0, keepdims=True)
        pick = order == first
        kept_v = jnp.where(kidx == r, mx, kept_v)
        if payload is None:
            kept_p = jnp.where(kidx == r, first, kept_p)
        else:
            kept_p = jnp.where(kidx == r, jnp.max(jnp.where(pick, payload, -1.0), axis=0, keepdims=True), kept_p)
        work = jnp.where(pick, REMOVED, work)
    return kept_v, kept_p


def _candidate_rows():
    ij = [(i, 0) for i in range(PEER_TOPK)]
    for j in range(1, SUBLANES):
        ij += [(i, j) for i in range(SUBLANES)]
    ij += [(0, j) for j in range(SUBLANES, PEER_TOPK)]
    flat = np.array([i * PEER_TOPK + j for i, j in ij], np.float32)
    valid = np.array([(i + 1) * (j + 1) <= PEER_TOPK for i, j in ij])
    return flat[:, None], np.where(valid, 0.0, REMOVED).astype(np.float32)[:, None]


def _candidates(a0, a1):
    return ([(a0, a1[0:1, :])]
            + [(a0[0:SUBLANES, :], a1[j:j + 1, :]) for j in range(1, SUBLANES)]
            + [(a0[0:1, :], a1[SUBLANES:PEER_TOPK, :])])


def _peer_route_kernel(x_ref, wq_ref, sk_ref, flat_ref, pad_ref, e_ref, g_ref):
    tm = x_ref.shape[0]
    q = jnp.dot(x_ref[...].astype(BF16), wq_ref[...], preferred_element_type=F32)
    key_idx = lax.broadcasted_iota(jnp.int32, (PEER_KEYS, tm), 0).astype(F32)
    flat = jnp.broadcast_to(flat_ref[...], (flat_ref.shape[0], tm))
    experts, gates = [], []
    for h in range(PEER_HEADS):
        sv, si = [], []
        for c in range(2):
            col = (h * 2 + c) * PEER_KEY_DIM
            s_t = _dot_nt(sk_ref[c], q[:, col:col + PEER_KEY_DIM], HI)
            v, i = _topk_rows(s_t, key_idx, None, PEER_TOPK)
            sv.append(v)
            si.append(i)
        cand = jnp.concatenate([a + b for a, b in _candidates(sv[0], sv[1])], axis=0) + pad_ref[...]
        ecand = jnp.concatenate([a * float(PEER_KEYS) + b for a, b in _candidates(si[0], si[1])], axis=0)
        cv, ce = _topk_rows(cand, flat, ecand, PEER_TOPK)
        ex = jnp.exp(cv - cv[0:1, :])
        gates.append(ex / jnp.sum(ex, axis=0, keepdims=True))
        experts.append(ce)
    e_ref[...] = jnp.concatenate(experts, axis=0).astype(jnp.int32)
    g_ref[...] = jnp.concatenate(gates, axis=0).T


def _peer_route(x1, wq_bf, subkeys, tm=256):
    T = x1.shape[0]
    tm = min(tm, T)
    nq = wq_bf.shape[1]
    flat, pad = _candidate_rows()
    n_cand = flat.shape[0]
    out = lambda: pl.BlockSpec((tm, N_PAIRS), lambda i: (i, 0))
    return pl.pallas_call(
        _peer_route_kernel,
        grid=(T // tm,),
        in_specs=[pl.BlockSpec((tm, D_MODEL), lambda i: (i, 0)),
                  pl.BlockSpec((D_MODEL, nq), lambda i: (0, 0)),
                  pl.BlockSpec((2, PEER_KEYS, PEER_KEY_DIM), lambda i: (0, 0, 0)),
                  pl.BlockSpec((n_cand, 1), lambda i: (0, 0)),
                  pl.BlockSpec((n_cand, 1), lambda i: (0, 0))],
        out_specs=[pl.BlockSpec((N_PAIRS, tm), lambda i: (0, i)), out()],
        out_shape=[jax.ShapeDtypeStruct((N_PAIRS, T), jnp.int32),
                   jax.ShapeDtypeStruct((T, N_PAIRS), F32)],
        compiler_params=_cparams(("arbitrary",)),
        name="peer_route",
    )(x1, wq_bf, subkeys, jnp.asarray(flat), jnp.asarray(pad))


TOKENS_PER_MATMUL = 16
PEER_TOKENS = 2 * LANES


class _ExpertIds:
    def __init__(self, e_hbm, e_smem_halves, sems, tb):
        self.e_hbm, self.e_smem, self.sems, self.tb, self.half = e_hbm, e_smem_halves, sems, tb, tb // 2
        self.step = pl.program_id(0)
        self.n_steps = pl.num_programs(0)

    def _copy(self, block, h):
        return pltpu.make_async_copy(self.e_hbm.at[:, pl.ds(block * self.tb + h * self.half, self.half)],
                                     self.e_smem[h], self.sems.at[h])

    def start_first_block(self):
        @pl.when(self.step == 0)
        def _():
            self._copy(0, 0).start()
            self._copy(0, 1).start()

    def wait_half(self, h):
        self._copy(self.step, h).wait()

    def prefetch_half(self, h):
        @pl.when(self.step + 1 < self.n_steps)
        def _():
            self._copy(self.step + 1, h).start()

    def __call__(self, h, k, t):
        return self.e_smem[h].at[k][t - h * self.half]


def _pack_table(tab):
    return tab.astype(BF16).reshape(N_EXPERTS, D_ROWS, LANES)


def _sublane_sums(ps, roll, where, sub):
    lvl = ps
    for r in (1, 2, 4):
        m = (sub % (2 * r)) < r
        half = len(lvl) // 2
        lvl = [where(m, lvl[i], lvl[i + half]) + roll(where(m, lvl[i + half], lvl[i]), r) for i in range(half)]
    return lvl[0]


def _sum_src():
    sub = np.arange(SUBLANES)[:, None] * np.ones((1, LANES), np.int64)
    ps = [np.full((SUBLANES, LANES), 10.0 ** i) for i in range(SUBLANES)]
    out = _sublane_sums(ps, lambda v, r: np.roll(v, r, axis=0), np.where, sub)
    return [int(round(np.log10(out[s, 0] / SUBLANES))) for s in range(SUBLANES)]


_SUM_SRC = _sum_src()


def _peer_u_kernel(e_hbm, x_ref, tab_ref, gate_ref, w_ref, part_ref, zt_ref, e_smem_a, e_smem_b, sems, *, tb):
    sub = lax.broadcasted_iota(jnp.int32, (SUBLANES, LANES), 0)
    lane = lax.broadcasted_iota(jnp.int32, (N_PAIRS, tb), 1)
    ids = _ExpertIds(e_hbm, (e_smem_a, e_smem_b), sems, tb)
    ids.start_first_block()

    def gather_dots(h, t, slot):
        xt = x_ref[t]
        for grp in range(N_PAIRS // SUBLANES):
            prods = [None] * SUBLANES
            for s in range(SUBLANES):
                prods[_SUM_SRC[s]] = tab_ref[ids(h, grp * SUBLANES + s, t)].astype(F32) * xt
            part_ref[slot, grp * SUBLANES:(grp + 1) * SUBLANES, :] = _sublane_sums(
                prods, lambda v, r: pltpu.roll(v, r, axis=0), jnp.where, sub)

    n_slots = part_ref.shape[0]

    def lane_sums(t0):
        zt = zt_ref[...]
        for j in range(n_slots):
            zt = jnp.where(lane == t0 + j, jnp.sum(part_ref[j], axis=1, keepdims=True), zt)
        zt_ref[...] = zt

    part_ref[...] = jnp.zeros_like(part_ref)
    zt_ref[...] = jnp.zeros_like(zt_ref)

    def some_tokens(h, i, carry):
        lane_sums(n_slots * (i - 1))
        for j in range(n_slots):
            gather_dots(h, n_slots * i + j, j)
        return carry

    for h in range(2):
        ids.wait_half(h)
        lax.fori_loop(h * tb // (2 * n_slots), (h + 1) * tb // (2 * n_slots), functools.partial(some_tokens, h), 0)
        ids.prefetch_half(h)
    lane_sums(tb - n_slots)
    w_ref[...] = jax.nn.gelu(zt_ref[...].T) * gate_ref[...]


def _peer_u(experts_t, x3, tab, gates, tb=PEER_TOKENS):
    T = x3.shape[0]
    vm = lambda: pl.BlockSpec((tb, N_PAIRS), lambda i: (i, 0))
    return pl.pallas_call(
        functools.partial(_peer_u_kernel, tb=tb),
        grid=(T // tb,),
        in_specs=[pl.BlockSpec(memory_space=pl.ANY),
                  pl.BlockSpec((tb, D_ROWS, LANES), lambda i: (i, 0, 0)),
                  pl.BlockSpec((N_EXPERTS, D_ROWS, LANES), lambda i: (0, 0, 0), pipeline_mode=pl.Buffered(1)),
                  vm()],
        out_specs=vm(),
        out_shape=jax.ShapeDtypeStruct((T, N_PAIRS), F32),
        scratch_shapes=[pltpu.VMEM((4, N_PAIRS, LANES), F32), pltpu.VMEM((N_PAIRS, tb), F32),
                        pltpu.SMEM((N_PAIRS, tb // 2), jnp.int32), pltpu.SMEM((N_PAIRS, tb // 2), jnp.int32),
                        pltpu.SemaphoreType.DMA((2,))],
        compiler_params=_cparams(("arbitrary",)),
        name="peer_u",
    )(experts_t, x3, tab, gates)


def _expand_matrix():
    k = np.arange(N_PAIRS)[:, None]
    c = np.arange(N_PAIRS * D_ROWS)[None, :]
    return (c // D_ROWS == k).astype(np.float32)


def _peer_v_kernel(e_hbm, w_ref, x_ref, tab_ref, expand_ref, g_ref, b_ref, o_ref, y_ref, wide_ref,
                   e_smem_a, e_smem_b, sems, *, tb, alpha):
    n_col = N_PAIRS * D_ROWS
    ids = _ExpertIds(e_hbm, (e_smem_a, e_smem_b), sems, tb)
    ids.start_first_block()
    diag = (lax.broadcasted_iota(jnp.int32, (D_ROWS, n_col), 1) % D_ROWS
            == lax.broadcasted_iota(jnp.int32, (D_ROWS, n_col), 0))
    part_row = lax.broadcasted_iota(jnp.int32, (SUBLANES, LANES), 0)

    def expand_tokens(i, carry):
        t0 = pl.multiple_of(i * TOKENS_PER_MATMUL, TOKENS_PER_MATMUL)
        parts = []
        for j in range(TOKENS_PER_MATMUL):
            wrow = w_ref[pl.ds(t0 + j, 1), :]
            hi = wrow.astype(BF16).astype(F32)
            mid = (wrow - hi).astype(BF16).astype(F32)
            lo = ((wrow - hi) - mid).astype(BF16).astype(F32)
            bc = lambda v: jnp.broadcast_to(v, (SUBLANES, LANES))
            parts.append(jnp.where(part_row == 0, bc(hi), jnp.where(part_row == 1, bc(mid),
                                                                    jnp.where(part_row == 2, bc(lo), 0.0))))
        lhs = jnp.concatenate(parts, axis=0).astype(BF16)
        wide = jnp.dot(lhs, expand_ref[...], preferred_element_type=F32)
        wide_ref[pl.ds(t0, TOKENS_PER_MATMUL)] = wide.reshape(TOKENS_PER_MATMUL, SUBLANES, n_col)
        return carry

    lax.fori_loop(0, tb // TOKENS_PER_MATMUL, expand_tokens, 0)

    def left_operand(t):
        rows = [jnp.where(diag, jnp.broadcast_to(wide_ref[t, p:p + 1, :], (D_ROWS, n_col)), 0.0) for p in range(3)]
        return jnp.concatenate(rows, axis=0).astype(BF16)

    def weighted_sum(h, t, a):
        rows = [tab_ref[ids(h, k, t)].astype(F32) for k in range(N_PAIRS)]
        rhs = jnp.concatenate(rows, axis=0).astype(BF16)
        y3 = jnp.dot(a, rhs, preferred_element_type=F32)
        return (y3[0:D_ROWS] + y3[D_ROWS:2 * D_ROWS]) + y3[2 * D_ROWS:]

    tokens_per_step = 32

    def some_tokens(h, i, carry):
        ts = [tokens_per_step * i + j for j in range(tokens_per_step)]
        ys = [weighted_sum(h, t, left_operand(t)) for t in ts]
        for t, y in zip(ts, ys):
            y_ref[t] = y
        return carry

    for h in range(2):
        ids.wait_half(h)
        lax.fori_loop(h * tb // (2 * tokens_per_step), (h + 1) * tb // (2 * tokens_per_step),
                      functools.partial(some_tokens, h), 0)
        ids.prefetch_half(h)
    z = alpha * x_ref[...] + y_ref[...]
    inv_d = 1.0 / D_MODEL
    mu = jnp.sum(jnp.sum(z, axis=2, keepdims=True), axis=1, keepdims=True) * inv_d
    zc = z - mu
    var = jnp.sum(jnp.sum(zc * zc, axis=2, keepdims=True), axis=1, keepdims=True) * inv_d
    o_ref[...] = zc * lax.rsqrt(var + LN_EPS) * g_ref[...] + b_ref[...]


def _peer_v(experts_t, wgt, x3, tab, g, b, alpha, tb=PEER_TOKENS):
    T = x3.shape[0]
    return pl.pallas_call(
        functools.partial(_peer_v_kernel, tb=tb, alpha=alpha),
        grid=(T // tb,),
        in_specs=[pl.BlockSpec(memory_space=pl.ANY),
                  pl.BlockSpec((tb, N_PAIRS), lambda i: (i, 0)),
                  pl.BlockSpec((tb, D_ROWS, LANES), lambda i: (i, 0, 0)),
                  pl.BlockSpec((N_EXPERTS, D_ROWS, LANES), lambda i: (0, 0, 0), pipeline_mode=pl.Buffered(1)),
                  pl.BlockSpec((N_PAIRS, N_PAIRS * D_ROWS), lambda i: (0, 0)),
                  pl.BlockSpec((1, D_ROWS, LANES), lambda i: (0, 0, 0)),
                  pl.BlockSpec((1, D_ROWS, LANES), lambda i: (0, 0, 0))],
        out_specs=pl.BlockSpec((tb, D_ROWS, LANES), lambda i: (i, 0, 0)),
        out_shape=jax.ShapeDtypeStruct((T, D_ROWS, LANES), F32),
        scratch_shapes=[pltpu.VMEM((tb, D_ROWS, LANES), F32), pltpu.VMEM((tb, SUBLANES, N_PAIRS * D_ROWS), F32),
                        pltpu.SMEM((N_PAIRS, tb // 2), jnp.int32), pltpu.SMEM((N_PAIRS, tb // 2), jnp.int32),
                        pltpu.SemaphoreType.DMA((2,))],
        compiler_params=_cparams(("arbitrary",)),
        name="peer_v",
    )(experts_t, wgt, x3, tab, jnp.asarray(_expand_matrix(), dtype=BF16),
      g.reshape(1, D_ROWS, LANES), b.reshape(1, D_ROWS, LANES))


def kernel(x, w_in, b_in, conv_w, conv_b, lru_wa, lru_ba, lru_wx, lru_bx, lru_lambda, cmp_pos_k, cmpk_w1, cmpk_b1, cmpk_w2, cmpk_b2, cmp_pos_v, cmpv_w1, cmpv_b1, cmpv_w2, cmpv_b2, gn_lru_g, gn_nsa_g, w_out, ln1_g, ln1_b, peer_wq, peer_subkeys, peer_u, peer_v, ln2_g, ln2_b):
    B, S, D = x.shape
    T = B * S
    depth = w_in.shape[0]
    alpha = (2 * depth) ** 0.25
    xf = x.reshape(T, D)
    for l in range(depth):
        w_in_bf = jnp.pad(w_in[l], ((0, 0), (0, IN_COLS_PAD - IN_COLS))).astype(BF16)
        b_in_row = jnp.pad(b_in[l], (0, IN_COLS_PAD - IN_COLS)).reshape(1, IN_COLS_PAD)
        h_all, kraw, kbf, vtiles = _in_proj(xf, w_in_bf, b_in_row, B, S)

        y_lru = _rglru(h_all, B, S, conv_w[l], conv_b[l], _block_diag(lru_wa[l]), lru_ba[l],
                       _block_diag(lru_wx[l]), lru_bx[l], lru_lambda[l], gn_lru_g[l])

        w1 = jnp.stack([cmpk_w1[l], cmpv_w1[l]])
        w1_pos = w1.reshape(2, 2, CMP_STRIDE, HEAD_DIM, HEAD_DIM)
        w1l = jnp.concatenate([w1_pos[:, 0], w1_pos[:, 1]], axis=-1)
        pos = jnp.stack([cmp_pos_k[l], cmp_pos_v[l]]).reshape(2, 1, CMP_BLOCK * HEAD_DIM)
        posflat = jnp.broadcast_to(pos, (2, SUBLANES, CMP_BLOCK * HEAD_DIM))
        kcvc = _compress(kraw, w1l, w1, posflat,
                         jnp.stack([cmpk_b1[l], cmpv_b1[l]]).reshape(2, 1, HEAD_DIM),
                         jnp.stack([cmpk_w2[l], cmpv_w2[l]]),
                         jnp.stack([cmpk_b2[l], cmpv_b2[l]]).reshape(2, 1, HEAD_DIM))
        kcvc = kcvc.reshape(2, B, N_KV, S // CMP_STRIDE, HEAD_DIM)
        y_nsa = _nsa(h_all, B, S, kcvc[0], kcvc[1], kbf[0], vtiles[0], kbf[1], vtiles[1], gn_nsa_g[l])

        x1 = _out_ln(y_lru, y_nsa, xf, w_out[l].astype(BF16), ln1_g[l], ln1_b[l], alpha)

        experts_t, gates = _peer_route(x1, peer_wq[l].astype(BF16), peer_subkeys[l])
        x3 = x1.reshape(T, D_ROWS, LANES)
        wgt = _peer_u(experts_t, x3, _pack_table(peer_u[l]), gates)
        xf = _peer_v(experts_t, wgt, x3, _pack_table(peer_v[l]), ln2_g[l], ln2_b[l], alpha).reshape(T, D)
    return xf.reshape(B, S, D)
```

```python
import functools

import numpy as np
import jax
import jax.numpy as jnp
from jax import lax
from jax.experimental import pallas as pl
from jax.experimental.pallas import tpu as pltpu

D_MODEL = 1024
LRU_WIDTH = 512
LRU_BLOCKS = 8
LRU_BLOCK_DIM = LRU_WIDTH // LRU_BLOCKS
CONV_WIDTH = 4
LRU_C = 8.0
N_HEADS = 8
HEAD_DIM = 64
N_KV = 2
Q_PER_KV = N_HEADS // N_KV
NSA_WIDTH = N_HEADS * HEAD_DIM
KV_WIDTH = N_KV * HEAD_DIM
N_BRANCHES = 3
CMP_BLOCK = 32
CMP_STRIDE = 16
SEL_BLOCK = 64
SEL_TOPN = 16
WINDOW = 512
Q_BLOCK = 128
FORCE_BONUS = 1e4
NEG = -1e30
REMOVED = -3.0e38
PEER_HEADS = 8
PEER_KEYS = 128
PEER_TOPK = 16
PEER_KEY_DIM = 128
N_EXPERTS = PEER_KEYS * PEER_KEYS
N_PAIRS = PEER_HEADS * PEER_TOPK
LN_EPS = 1e-5
IN_SPLITS = (LRU_WIDTH, LRU_WIDTH, NSA_WIDTH) + (KV_WIDTH,) * 6 + (N_HEADS * N_BRANCHES,)
IN_COLS = sum(IN_SPLITS)

LANES = 128
SUBLANES = 8
VMEM_LIMIT_BYTES = 56 * 1024 * 1024

IN_COLS_PAD = -(-IN_COLS // LANES) * LANES
COL_LRU_X = 0
COL_LRU_GATE = LRU_WIDTH
COL_Q = 2 * LRU_WIDTH
COL_KV = COL_Q + NSA_WIDTH
COL_GATES = COL_KV + 6 * KV_WIDTH

HI = lax.Precision.HIGHEST
F32 = jnp.float32
BF16 = jnp.bfloat16

D_ROWS = D_MODEL // LANES


def _cparams(sem):
    return pltpu.CompilerParams(dimension_semantics=sem, vmem_limit_bytes=VMEM_LIMIT_BYTES)


def _dot_nt(a, b, precision=None):
    return lax.dot_general(a, b, (((1,), (1,)), ((), ())), precision=precision,
                           preferred_element_type=F32)


def _split2(a):
    hi = a.astype(BF16)
    return hi, (a - hi.astype(F32)).astype(BF16)


def _dot_split(a, b):
    a_hi, a_lo = _split2(a)
    b_hi, b_lo = _split2(b)
    dot = lambda u, v: jnp.dot(u, v, preferred_element_type=F32)
    return dot(a_hi, b_hi) + (dot(a_hi, b_lo) + dot(a_lo, b_hi))


def _layer_norm_rows(z, g, b):
    mu = jnp.mean(z, axis=-1, keepdims=True)
    zc = z - mu
    var = jnp.mean(zc * zc, axis=-1, keepdims=True)
    return zc * lax.rsqrt(var + LN_EPS) * g + b


V_AUG_ROWS = HEAD_DIM + 16


def _in_proj_kernel(x_ref, w_ref, b_ref, o_ref, kraw_ref, kbf_ref, vt_ref):
    h = jnp.dot(x_ref[...].astype(BF16), w_ref[...], preferred_element_type=F32) + b_ref[...]
    o_ref[...] = h
    group_cols = lambda col, g: slice(col + g * HEAD_DIM, col + (g + 1) * HEAD_DIM)
    for j in range(2):
        for g in range(N_KV):
            kraw_ref[j, 0, g] = h[:, group_cols(COL_KV + j * KV_WIDTH, g)]
            kbf_ref[j, 0, g] = h[:, group_cols(COL_KV + (2 + 2 * j) * KV_WIDTH, g)].astype(BF16)
    ones_row = (lax.broadcasted_iota(jnp.int32, (V_AUG_ROWS - HEAD_DIM, Q_BLOCK), 0) == 0).astype(F32)
    for j in range(2):
        col = COL_KV + (3 + 2 * j) * KV_WIDTH
        for qb in range(h.shape[0] // Q_BLOCK):
            v_t = h[qb * Q_BLOCK:(qb + 1) * Q_BLOCK, col:col + KV_WIDTH].T
            for g in range(N_KV):
                tile = jnp.concatenate([v_t[g * HEAD_DIM:(g + 1) * HEAD_DIM, :], ones_row], axis=0)
                vt_ref[j, 0, g, qb] = tile.astype(BF16)


def _in_proj(xf, w_bf, b_row, B, S, tm=512):
    T = xf.shape[0]
    tm = min(tm, S)
    nt = S // tm
    per_tile = lambda *dims: pl.BlockSpec((2, 1, N_KV) + dims, lambda i: (0, i // nt, 0, i % nt) + (0,) * (len(dims) - 1))
    return pl.pallas_call(
        _in_proj_kernel,
        grid=(T // tm,),
        in_specs=[pl.BlockSpec((tm, D_MODEL), lambda i: (i, 0)),
                  pl.BlockSpec((D_MODEL, IN_COLS_PAD), lambda i: (0, 0)),
                  pl.BlockSpec((1, IN_COLS_PAD), lambda i: (0, 0))],
        out_specs=[pl.BlockSpec((tm, IN_COLS_PAD), lambda i: (i, 0)),
                   per_tile(tm, HEAD_DIM), per_tile(tm, HEAD_DIM),
                   per_tile(tm // Q_BLOCK, V_AUG_ROWS, Q_BLOCK)],
        out_shape=[jax.ShapeDtypeStruct((T, IN_COLS_PAD), F32),
                   jax.ShapeDtypeStruct((2, B, N_KV, S, HEAD_DIM), F32),
                   jax.ShapeDtypeStruct((2, B, N_KV, S, HEAD_DIM), BF16),
                   jax.ShapeDtypeStruct((2, B, N_KV, S // Q_BLOCK, V_AUG_ROWS, Q_BLOCK), BF16)],
        compiler_params=_cparams(("arbitrary",)),
        name="in_proj",
    )(xf, w_bf, b_row)


def _rglru_kernel(x_ref, gate_ref, cw_ref, cb_ref, wa_ref, ba_ref, wx_ref, bx_ref, lam_ref, gn_ref,
                  o_ref, xprev_ref, h_ref, *, ts):
    i = pl.program_id(1)

    @pl.when(i == 0)
    def _():
        xprev_ref[...] = jnp.zeros_like(xprev_ref)
        h_ref[...] = jnp.zeros_like(h_ref)

    xb = x_ref[...]
    xp = xprev_ref[...]
    row = lax.broadcasted_iota(jnp.int32, xb.shape, 0)
    xc = cw_ref[CONV_WIDTH - 1:CONV_WIDTH, :] * xb + cb_ref[...]
    for j in range(1, CONV_WIDTH):
        shifted = jnp.where(row >= j, pltpu.roll(xb, j, axis=0), pltpu.roll(xp, j, axis=0))
        xc = xc + cw_ref[CONV_WIDTH - 1 - j:CONV_WIDTH - j, :] * shifted
    xprev_ref[...] = xb

    r = jax.nn.sigmoid(_dot_split(xc, wa_ref[...]) + ba_ref[...])
    ig = jax.nn.sigmoid(_dot_split(xc, wx_ref[...]) + bx_ref[...])
    lam = lam_ref[...]
    softplus_neg_lam = jnp.maximum(-lam, 0.0) + jnp.log1p(jnp.exp(-jnp.abs(lam)))
    log_a = -LRU_C * r * softplus_neg_lam
    a = jnp.exp(log_a)
    th = jnp.tanh(log_a)
    u = jnp.sqrt(-2.0 * th / (1.0 - th)) * (ig * xc)

    d = 1
    while d < ts:
        keep = row >= d
        a_sh = pltpu.roll(a, d, axis=0)
        u_sh = pltpu.roll(u, d, axis=0)
        u = jnp.where(keep, a * u_sh + u, u)
        a = jnp.where(keep, a * a_sh, a)
        d *= 2
    h = u + a * h_ref[0:1, :]
    h_ref[0:1, :] = h[ts - 1:ts, :]

    y = h * jax.nn.gelu(gate_ref[...])
    y = y * lax.rsqrt(jnp.mean(y * y, axis=-1, keepdims=True) + LN_EPS) * gn_ref[...]
    o_ref[...] = y


def _rglru(h_all, B, S, cw, cb, wa_bd, ba, wx_bd, bx, lam, gn, ts=512):
    ts = min(ts, S)
    nt = S // ts
    C = LRU_WIDTH
    row = lambda v: v.reshape(1, C)
    const = lambda shape: pl.BlockSpec(shape, lambda b, i: (0,) * len(shape))
    return pl.pallas_call(
        functools.partial(_rglru_kernel, ts=ts),
        grid=(B, nt),
        in_specs=[pl.BlockSpec((ts, C), lambda b, i: (b * nt + i, COL_LRU_X // C)),
                  pl.BlockSpec((ts, C), lambda b, i: (b * nt + i, COL_LRU_GATE // C)),
                  const((CONV_WIDTH, C)), const((1, C)), const((C, C)), const((1, C)),
                  const((C, C)), const((1, C)), const((1, C)), const((1, C))],
        out_specs=pl.BlockSpec((ts, C), lambda b, i: (b * nt + i, 0)),
        out_shape=jax.ShapeDtypeStruct((B * S, C), F32),
        scratch_shapes=[pltpu.VMEM((ts, C), F32), pltpu.VMEM((SUBLANES, C), F32)],
        compiler_params=_cparams(("arbitrary", "arbitrary")),
        name="rglru",
    )(h_all, h_all, cw, row(cb), wa_bd, row(ba), wx_bd, row(bx), row(lam), row(gn))


def _block_diag(w):
    n, d, e = w.shape
    eye = jnp.eye(n, dtype=w.dtype)
    return (w[:, :, None, :] * eye[:, None, :, None]).reshape(n * d, n * e)


def _compress_kernel(k_ref, w1l_ref, w1_ref, pos_ref, b1_ref, w2_ref, b2_ref, o_ref):
    nr = o_ref.shape[2]
    ab = jnp.zeros((nr, 2 * HEAD_DIM), F32)
    for l in range(CMP_STRIDE):
        k_l = k_ref[0, 0, 0, pl.ds(l, nr, stride=CMP_STRIDE), :]
        ab = ab + jnp.dot(k_l, w1l_ref[0, l], precision=HI, preferred_element_type=F32)
    ab_next = pltpu.roll(ab, nr - 1, axis=0)
    pre = ab[:, :HEAD_DIM] + ab_next[:, HEAD_DIM:]
    posc = jnp.dot(pos_ref[0], w1_ref[0], precision=HI, preferred_element_type=F32)[0:1, :]
    h1 = jax.nn.gelu(pre + posc + b1_ref[0])
    o_ref[0, 0] = jnp.dot(h1, w2_ref[0], precision=HI, preferred_element_type=F32) + b2_ref[0]


def _compress(kraw, w1l, w1, posflat, b1, w2, b2):
    _, B, G, S, _ = kraw.shape
    nr = S // CMP_STRIDE
    fan = CMP_BLOCK * HEAD_DIM
    wspec = lambda shape: pl.BlockSpec((1,) + shape, lambda s, i: (s,) + (0,) * len(shape))
    return pl.pallas_call(
        _compress_kernel,
        grid=(2, B * G),
        in_specs=[pl.BlockSpec((1, 1, 1, S, HEAD_DIM), lambda s, i: (s, i // G, i % G, 0, 0)),
                  wspec((CMP_STRIDE, HEAD_DIM, 2 * HEAD_DIM)), wspec((fan, HEAD_DIM)), wspec((SUBLANES, fan)),
                  wspec((1, HEAD_DIM)), wspec((HEAD_DIM, HEAD_DIM)), wspec((1, HEAD_DIM))],
        out_specs=pl.BlockSpec((1, 1, nr, HEAD_DIM), lambda s, i: (s, i, 0, 0)),
        out_shape=jax.ShapeDtypeStruct((2, B * G, nr, HEAD_DIM), F32),
        compiler_params=_cparams(("arbitrary", "arbitrary")),
        name="compress",
    )(kraw, w1l, w1, posflat, b1, w2, b2)


def _selection_matrix(n_rows, n_cmp, n_sel):
    cs = np.arange(n_rows)[:, None] * CMP_STRIDE
    ss = np.arange(n_sel)[None, :] * SEL_BLOCK
    ov = np.clip(np.minimum(cs + CMP_BLOCK, ss + SEL_BLOCK) - np.maximum(cs, ss), 0, None)
    ov = (ov / CMP_STRIDE).astype(np.float32)
    ov[n_cmp:] = 0.0
    return ov


SEL_TILE = 4 * Q_BLOCK
WIN_SPAN = WINDOW + Q_BLOCK


def _nsa_kernel(q_ref, gt_ref, kc_ref, vct_ref, ks_ref, vst_ref, kw_ref, vwt_ref, slope_ref, bias_ref,
                selmat_ref, emat_ref, gn_ref, o_ref, s_scr, *, n_cmp, n_top, win_span):
    qt = pl.program_id(1)
    q0 = qt * Q_BLOCK
    R = Q_PER_KV
    cols = R * Q_BLOCK
    n_rows_c = kc_ref.shape[2]
    n_sel = selmat_ref.shape[0]
    q_loc = lax.broadcasted_iota(jnp.int32, (1, cols), 1) % Q_BLOCK
    t_col = q0 + q_loc
    gates_t = jax.nn.sigmoid(gt_ref[...]).T

    def step(carry, qs_bf, slope, bias, k_bf, v_tiles, offset, ok):
        m, acc = carry
        s = jnp.where(ok, _dot_nt(k_bf, qs_bf) + bias, NEG)
        c = slope * (-offset).astype(F32)
        m_new = jnp.maximum(m, jnp.max(s, axis=0, keepdims=True) + c)
        p = jnp.exp(s + (c - m_new)).astype(BF16)
        acc = jnp.exp(m - m_new) * acc
        for i, v in enumerate(v_tiles):
            acc = acc + jnp.dot(v, p[i * Q_BLOCK:(i + 1) * Q_BLOCK, :], preferred_element_type=F32)
        return m_new, acc

    init = (jnp.full((1, cols), NEG, F32), jnp.zeros((V_AUG_ROWS, cols), F32))
    finish = lambda acc: acc[0:HEAD_DIM, :] / acc[HEAD_DIM:HEAD_DIM + 1, :]
    y_rows = []
    for g in range(N_KV):
        qg = q_ref[:, g * R * HEAD_DIM:(g + 1) * R * HEAD_DIM] * (HEAD_DIM ** -0.5)
        qs = jnp.concatenate([qg[:, r * HEAD_DIM:(r + 1) * HEAD_DIM] for r in range(R)], axis=0)
        qs_bf = qs.astype(BF16)
        slope = slope_ref[g]

        kc = kc_ref[0, g]
        c_idx = lax.broadcasted_iota(jnp.int32, (n_rows_c, 1), 0)
        cmp_end = c_idx * CMP_STRIDE + (CMP_BLOCK - 1)
        valid_c = (cmp_end <= t_col) & (c_idx < n_cmp)
        q_hi, q_lo = _split2(qs)
        k_hi, k_lo = _split2(kc)
        s = _dot_nt(k_hi, q_hi) + (_dot_nt(k_hi, q_lo) + _dot_nt(k_lo, q_hi))
        s = s - slope * jnp.abs(t_col - cmp_end).astype(F32)
        s = jnp.where(valid_c, s, NEG)
        e = jnp.exp(s - jnp.max(s, axis=0, keepdims=True))
        p_c = e / jnp.sum(e, axis=0, keepdims=True)
        p_c = p_c * (t_col >= CMP_BLOCK - 1).astype(F32)
        o_c = jnp.dot(vct_ref[0, g].astype(BF16), p_c.astype(BF16), preferred_element_type=F32)

        p_sum = p_c[:, 0:Q_BLOCK]
        for r in range(1, R):
            p_sum = p_sum + p_c[:, r * Q_BLOCK:(r + 1) * Q_BLOCK]
        p_hi, p_lo = _split2(p_sum)
        dot = lambda a, b: jnp.dot(a, b, preferred_element_type=F32)
        imp_t = dot(selmat_ref[...], p_hi) + dot(selmat_ref[...], p_lo)
        j_idx = lax.broadcasted_iota(jnp.int32, (n_sel, Q_BLOCK), 0)
        j_f = j_idx.astype(F32)
        cur = t_col[:, 0:Q_BLOCK] // SEL_BLOCK
        forced = (j_idx == 0) | (j_idx == cur) | (j_idx == cur - 1)
        work = jnp.where(j_idx <= cur, imp_t + jnp.where(forced, FORCE_BONUS, 0.0), NEG)
        msel_t = jnp.zeros((n_sel, Q_BLOCK), F32)
        for _ in range(n_top):
            mx = jnp.max(work, axis=0, keepdims=True)
            first = jnp.min(jnp.where(work == mx, j_f, float(LANES)), axis=0, keepdims=True)
            pick = j_f == first
            msel_t = jnp.where(pick, 1.0, msel_t)
            work = jnp.where(pick, REMOVED, work)
        if n_sel < LANES:
            msel_t = jnp.concatenate([msel_t, jnp.zeros((LANES - n_sel, Q_BLOCK), F32)], axis=0)
        msel_bf = msel_t.astype(BF16)

        sel_key = lax.broadcasted_iota(jnp.int32, (SEL_TILE, cols), 0)
        bias_sel = bias_ref[g, 0:SEL_TILE, :]

        def sel_scores(kt, causal):
            k0 = pl.multiple_of(kt * SEL_TILE, SEL_TILE)
            ch = jnp.dot(emat_ref[kt], msel_bf, preferred_element_type=F32)
            ok = jnp.concatenate([ch] * R, axis=1) > 0.5
            if causal:
                ok = ok & (sel_key <= q_loc + (q0 - k0))
            s = jnp.where(ok, _dot_nt(ks_ref[0, g, pl.ds(k0, SEL_TILE), :], qs_bf) + bias_sel, NEG)
            return s, jnp.max(s, axis=0, keepdims=True)

        def sel_consume(m, acc, s_max, kt):
            c = slope * (kt * SEL_TILE - q0).astype(F32)
            m_new = jnp.maximum(m, s_max + c)
            p = jnp.exp(s_scr[...] + (c - m_new)).astype(BF16)
            acc = jnp.exp(m - m_new) * acc
            for i in range(SEL_TILE // Q_BLOCK):
                acc = acc + jnp.dot(vst_ref[0, g, kt * (SEL_TILE // Q_BLOCK) + i],
                                    p[i * Q_BLOCK:(i + 1) * Q_BLOCK, :], preferred_element_type=F32)
            return m_new, acc

        kt_diag = q0 // SEL_TILE
        s_first, max_first = sel_scores(kt_diag, True)
        s_scr[...] = s_first

        def sel_body(kt, lc):
            m, acc, s_max, kt_parked = lc
            s_next, max_next = sel_scores(kt, False)
            m, acc = sel_consume(m, acc, s_max, kt_parked)
            s_scr[...] = s_next
            return m, acc, max_next, kt

        m_s, acc_s, max_last, kt_last = lax.fori_loop(0, kt_diag, sel_body, init + (max_first, kt_diag))
        _, acc_s = sel_consume(m_s, acc_s, max_last, kt_last)
        o_s = finish(acc_s)

        w0 = pl.multiple_of(jnp.maximum(q0 + Q_BLOCK - win_span, 0), Q_BLOCK)
        win_key = lax.broadcasted_iota(jnp.int32, (win_span, cols), 0)
        reach = q_loc + (q0 - w0)
        ok_w = (win_key <= reach) & (win_key > reach - WINDOW)
        v_tiles = [vwt_ref[0, g, w0 // Q_BLOCK + i] for i in range(win_span // Q_BLOCK)]
        carry = step(init, qs_bf, slope, bias_ref[g, 0:win_span, :], kw_ref[0, g, pl.ds(w0, win_span), :],
                     v_tiles, q0 - w0, ok_w)
        o_w = finish(carry[1])

        for r in range(R):
            col = (g * R + r) * N_BRANCHES
            sl = slice(r * Q_BLOCK, (r + 1) * Q_BLOCK)
            y_rows.append(gates_t[col:col + 1, :] * o_c[:, sl] + gates_t[col + 1:col + 2, :] * o_s[:, sl]
                          + gates_t[col + 2:col + 3, :] * o_w[:, sl])
    y_t = jnp.concatenate(y_rows, axis=0)
    y_t = y_t * lax.rsqrt(jnp.mean(y_t * y_t, axis=0, keepdims=True) + LN_EPS) * gn_ref[...]
    o_ref[...] = y_t.T


def _nsa(h_all, B, S, kc, vc, ks_bf, vs_tiles, kw_bf, vw_tiles, gn):
    nq = S // Q_BLOCK
    n_cmp = (S - CMP_BLOCK) // CMP_STRIDE + 1
    n_rows_c = kc.shape[2]
    n_sel = S // SEL_BLOCK
    n_top = min(SEL_TOPN, n_sel)
    n_span = S // SEL_TILE
    win_span = min(WIN_SPAN, S)
    n_bias = max(win_span, SEL_TILE)
    cols = Q_PER_KV * Q_BLOCK
    slopes = np.array([2.0 ** (-8.0 * (h + 1) / N_HEADS) for h in range(N_HEADS)], np.float32)
    slope_cols = np.repeat(slopes.reshape(N_KV, Q_PER_KV), Q_BLOCK, axis=1)[:, None, :]
    rel = (np.arange(cols)[None, :] % Q_BLOCK - np.arange(n_bias)[:, None]).astype(np.float32)
    bias = -slope_cols * rel[None]
    selmat_t = jnp.asarray(_selection_matrix(n_rows_c, n_cmp, n_sel).T, dtype=BF16)
    key_blk = (np.arange(S) // SEL_BLOCK).reshape(n_span, SEL_TILE, 1)
    emat_t = jnp.asarray((key_blk == np.arange(LANES)[None, None, :]).astype(np.float32), dtype=BF16)
    qw = Q_PER_KV * HEAD_DIM * N_KV
    per_batch = lambda *dims: pl.BlockSpec((1,) + dims, lambda b, i: (b,) + (0,) * len(dims))
    const = lambda shape: pl.BlockSpec(shape, lambda b, i: (0,) * len(shape))
    return pl.pallas_call(
        functools.partial(_nsa_kernel, n_cmp=n_cmp, n_top=n_top, win_span=win_span),
        grid=(B, nq),
        in_specs=[pl.BlockSpec((Q_BLOCK, qw), lambda b, i: (b * nq + i, COL_Q // qw)),
                  pl.BlockSpec((Q_BLOCK, LANES), lambda b, i: (b * nq + i, COL_GATES // LANES)),
                  per_batch(N_KV, n_rows_c, HEAD_DIM), per_batch(N_KV, HEAD_DIM, n_rows_c),
                  per_batch(N_KV, S, HEAD_DIM), per_batch(N_KV, nq, V_AUG_ROWS, Q_BLOCK),
                  per_batch(N_KV, S, HEAD_DIM), per_batch(N_KV, nq, V_AUG_ROWS, Q_BLOCK),
                  const((N_KV, 1, cols)), const((N_KV, n_bias, cols)), const((n_sel, n_rows_c)),
                  const((n_span, SEL_TILE, LANES)), const((NSA_WIDTH, 1))],
        out_specs=pl.BlockSpec((Q_BLOCK, NSA_WIDTH), lambda b, i: (b * nq + i, 0)),
        out_shape=jax.ShapeDtypeStruct((B * S, NSA_WIDTH), F32),
        scratch_shapes=[pltpu.VMEM((SEL_TILE, cols), F32)],
        compiler_params=_cparams(("arbitrary", "arbitrary")),
        name="nsa",
    )(h_all, h_all, kc, vc.transpose(0, 1, 3, 2), ks_bf, vs_tiles, kw_bf, vw_tiles,
      jnp.asarray(slope_cols), jnp.asarray(bias), selmat_t, emat_t, gn.reshape(NSA_WIDTH, 1))


def _out_ln_kernel(ylru_ref, ynsa_ref, x_ref, w_ref, g_ref, b_ref, o_ref, *, alpha):
    mix = jnp.dot(ylru_ref[...].astype(BF16), w_ref[0:LRU_WIDTH, :], preferred_element_type=F32)
    mix = mix + jnp.dot(ynsa_ref[...].astype(BF16), w_ref[LRU_WIDTH:, :], preferred_element_type=F32)
    o_ref[...] = _layer_norm_rows(alpha * x_ref[...] + mix, g_ref[...], b_ref[...])


def _out_ln(ylru, ynsa, xf, w_bf, g, b, alpha, tm=512):
    T = xf.shape[0]
    const = lambda shape: pl.BlockSpec(shape, lambda i: (0,) * len(shape))
    return pl.pallas_call(
        functools.partial(_out_ln_kernel, alpha=alpha),
        grid=(T // tm,),
        in_specs=[pl.BlockSpec((tm, LRU_WIDTH), lambda i: (i, 0)),
                  pl.BlockSpec((tm, NSA_WIDTH), lambda i: (i, 0)),
                  pl.BlockSpec((tm, D_MODEL), lambda i: (i, 0)),
                  const((D_MODEL, D_MODEL)), const((1, D_MODEL)), const((1, D_MODEL))],
        out_specs=pl.BlockSpec((tm, D_MODEL), lambda i: (i, 0)),
        out_shape=jax.ShapeDtypeStruct((T, D_MODEL), F32),
        compiler_params=_cparams(("arbitrary",)),
        name="out_ln",
    )(ylru, ynsa, xf, w_bf, g.reshape(1, D_MODEL), b.reshape(1, D_MODEL))


def _topk_rows(vals, order, payload, n_keep):
    n = vals.shape[1]
    kidx = lax.broadcasted_iota(jnp.int32, (n_keep, n), 0)
    never = float(1 << 20)
    kept_v = jnp.zeros((n_keep, n), F32)
    kept_p = jnp.zeros((n_keep, n), F32)
    work = vals
    for r in range(n_keep):
        mx = jnp.max(work, axis=0, keepdims=True)
        first = jnp.min(jnp.where(work == mx, order, never), axis=0, keepdims=True)
        pick = order == first
        kept_v = jnp.where(kidx == r, mx, kept_v)
        if payload is None:
            kept_p = jnp.where(kidx == r, first, kept_p)
        else:
            kept_p = jnp.where(kidx == r, jnp.max(jnp.where(pick, payload, -1.0), axis=0, keepdims=True), kept_p)
        work = jnp.where(pick, REMOVED, work)
    return kept_v, kept_p


def _candidate_rows():
    ij = [(i, 0) for i in range(PEER_TOPK)]
    for j in range(1, SUBLANES):
        ij += [(i, j) for i in range(SUBLANES)]
    ij += [(0, j) for j in range(SUBLANES, PEER_TOPK)]
    flat = np.array([i * PEER_TOPK + j for i, j in ij], np.float32)
    valid = np.array([(i + 1) * (j + 1) <= PEER_TOPK for i, j in ij])
    return flat[:, None], np.where(valid, 0.0, REMOVED).astype(np.float32)[:, None]


def _candidates(a0, a1):
    return ([(a0, a1[0:1, :])]
            + [(a0[0:SUBLANES, :], a1[j:j + 1, :]) for j in range(1, SUBLANES)]
            + [(a0[0:1, :], a1[SUBLANES:PEER_TOPK, :])])


def _peer_route_kernel(x_ref, wq_ref, sk_ref, flat_ref, pad_ref, e_ref, g_ref):
    tm = x_ref.shape[0]
    q = jnp.dot(x_ref[...].astype(BF16), wq_ref[...], preferred_element_type=F32)
    key_idx = lax.broadcasted_iota(jnp.int32, (PEER_KEYS, tm), 0).astype(F32)
    flat = jnp.broadcast_to(flat_ref[...], (flat_ref.shape[0], tm))
    experts, gates = [], []
    for h in range(PEER_HEADS):
        sv, si = [], []
        for c in range(2):
            col = (h * 2 + c) * PEER_KEY_DIM
            s_t = _dot_nt(sk_ref[c], q[:, col:col + PEER_KEY_DIM], HI)
            v, i = _topk_rows(s_t, key_idx, None, PEER_TOPK)
            sv.append(v)
            si.append(i)
        cand = jnp.concatenate([a + b for a, b in _candidates(sv[0], sv[1])], axis=0) + pad_ref[...]
        ecand = jnp.concatenate([a * float(PEER_KEYS) + b for a, b in _candidates(si[0], si[1])], axis=0)
        cv, ce = _topk_rows(cand, flat, ecand, PEER_TOPK)
        ex = jnp.exp(cv - cv[0:1, :])
        gates.append(ex / jnp.sum(ex, axis=0, keepdims=True))
        experts.append(ce)
    e_ref[...] = (jnp.concatenate(experts, axis=0) * float(ROWS_PER_EXPERT)).astype(jnp.int32)
    g_ref[...] = jnp.concatenate(gates, axis=0).T


def _peer_route(x1, wq_bf, subkeys, tm=256):
    T = x1.shape[0]
    tm = min(tm, T)
    nq = wq_bf.shape[1]
    flat, pad = _candidate_rows()
    n_cand = flat.shape[0]
    out = lambda: pl.BlockSpec((tm, N_PAIRS), lambda i: (i, 0))
    return pl.pallas_call(
        _peer_route_kernel,
        grid=(T // tm,),
        in_specs=[pl.BlockSpec((tm, D_MODEL), lambda i: (i, 0)),
                  pl.BlockSpec((D_MODEL, nq), lambda i: (0, 0)),
                  pl.BlockSpec((2, PEER_KEYS, PEER_KEY_DIM), lambda i: (0, 0, 0)),
                  pl.BlockSpec((n_cand, 1), lambda i: (0, 0)),
                  pl.BlockSpec((n_cand, 1), lambda i: (0, 0))],
        out_specs=[pl.BlockSpec((N_PAIRS, tm), lambda i: (0, i)), out()],
        out_shape=[jax.ShapeDtypeStruct((N_PAIRS, T), jnp.int32),
                   jax.ShapeDtypeStruct((T, N_PAIRS), F32)],
        compiler_params=_cparams(("arbitrary",)),
        name="peer_route",
    )(x1, wq_bf, subkeys, jnp.asarray(flat), jnp.asarray(pad))


ROWS_PER_EXPERT = D_ROWS // 2
TOKENS_PER_MATMUL = 16
PEER_TOKENS = 2 * LANES


class _ExpertIds:
    def __init__(self, e_hbm, e_smem_halves, sems, tb):
        self.e_hbm, self.e_smem, self.sems, self.tb, self.half = e_hbm, e_smem_halves, sems, tb, tb // 2
        self.step = pl.program_id(0)
        self.n_steps = pl.num_programs(0)

    def _copy(self, block, h):
        return pltpu.make_async_copy(self.e_hbm.at[:, pl.ds(block * self.tb + h * self.half, self.half)],
                                     self.e_smem[h], self.sems.at[h])

    def start_first_block(self):
        @pl.when(self.step == 0)
        def _():
            self._copy(0, 0).start()
            self._copy(0, 1).start()

    def wait_half(self, h):
        self._copy(self.step, h).wait()

    def prefetch_half(self, h):
        @pl.when(self.step + 1 < self.n_steps)
        def _():
            self._copy(self.step + 1, h).start()

    def __call__(self, h, k, t):
        return self.e_smem[h].at[k][t - h * self.half]


def _pack_table(tab):
    t = tab.astype(BF16).reshape(N_EXPERTS, ROWS_PER_EXPERT, 2, LANES).transpose(0, 1, 3, 2)
    return lax.bitcast_convert_type(t, jnp.int32).reshape(N_EXPERTS * ROWS_PER_EXPERT, LANES)


def _expert_row(tab_ref, offset):
    words = tab_ref[pl.ds(pl.multiple_of(offset, ROWS_PER_EXPERT), ROWS_PER_EXPERT), :]
    return pltpu.bitcast(words, BF16)


def _sublane_sums(ps, roll, where, sub):
    lvl = ps
    for r in (1, 2, 4):
        m = (sub % (2 * r)) < r
        half = len(lvl) // 2
        lvl = [where(m, lvl[i], lvl[i + half]) + roll(where(m, lvl[i + half], lvl[i]), r) for i in range(half)]
    return lvl[0]


def _sum_src():
    sub = np.arange(SUBLANES)[:, None] * np.ones((1, LANES), np.int64)
    ps = [np.full((SUBLANES, LANES), 10.0 ** i) for i in range(SUBLANES)]
    out = _sublane_sums(ps, lambda v, r: np.roll(v, r, axis=0), np.where, sub)
    return [int(round(np.log10(out[s, 0] / SUBLANES))) for s in range(SUBLANES)]


_SUM_SRC = _sum_src()


def _peer_u_kernel(e_hbm, x_ref, tab_ref, gate_ref, w_ref, part_ref, zt_ref, e_smem_a, e_smem_b, sems, *, tb):
    sub = lax.broadcasted_iota(jnp.int32, (SUBLANES, LANES), 0)
    lane = lax.broadcasted_iota(jnp.int32, (N_PAIRS, tb), 1)
    ids = _ExpertIds(e_hbm, (e_smem_a, e_smem_b), sems, tb)
    ids.start_first_block()

    def gather_dots(h, t, slot):
        xt = x_ref[t]
        for grp in range(N_PAIRS // SUBLANES):
            prods = [None] * SUBLANES
            for s in range(SUBLANES):
                prods[_SUM_SRC[s]] = _expert_row(tab_ref, ids(h, grp * SUBLANES + s, t)).astype(F32) * xt
            part_ref[slot, grp * SUBLANES:(grp + 1) * SUBLANES, :] = _sublane_sums(
                prods, lambda v, r: pltpu.roll(v, r, axis=0), jnp.where, sub)

    n_slots = part_ref.shape[0]

    def lane_sums(t0):
        zt = zt_ref[...]
        for j in range(n_slots):
            zt = jnp.where(lane == t0 + j, jnp.sum(part_ref[j], axis=1, keepdims=True), zt)
        zt_ref[...] = zt

    part_ref[...] = jnp.zeros_like(part_ref)
    zt_ref[...] = jnp.zeros_like(zt_ref)

    def some_tokens(h, i, carry):
        lane_sums(n_slots * (i - 1))
        for j in range(n_slots):
            gather_dots(h, n_slots * i + j, j)
        return carry

    for h in range(2):
        ids.wait_half(h)
        lax.fori_loop(h * tb // (2 * n_slots), (h + 1) * tb // (2 * n_slots), functools.partial(some_tokens, h), 0)
        ids.prefetch_half(h)
    lane_sums(tb - n_slots)
    w_ref[...] = jax.nn.gelu(zt_ref[...].T) * gate_ref[...]


def _peer_u(experts_t, x3, tab, gates, tb=PEER_TOKENS):
    T = x3.shape[0]
    vm = lambda: pl.BlockSpec((tb, N_PAIRS), lambda i: (i, 0))
    return pl.pallas_call(
        functools.partial(_peer_u_kernel, tb=tb),
        grid=(T // tb,),
        in_specs=[pl.BlockSpec(memory_space=pl.ANY),
                  pl.BlockSpec((tb, D_ROWS, LANES), lambda i: (i, 0, 0)),
                  pl.BlockSpec((N_EXPERTS * ROWS_PER_EXPERT, LANES), lambda i: (0, 0), pipeline_mode=pl.Buffered(1)),
                  vm()],
        out_specs=vm(),
        out_shape=jax.ShapeDtypeStruct((T, N_PAIRS), F32),
        scratch_shapes=[pltpu.VMEM((4, N_PAIRS, LANES), F32), pltpu.VMEM((N_PAIRS, tb), F32),
                        pltpu.SMEM((N_PAIRS, tb // 2), jnp.int32), pltpu.SMEM((N_PAIRS, tb // 2), jnp.int32),
                        pltpu.SemaphoreType.DMA((2,))],
        compiler_params=_cparams(("arbitrary",)),
        name="peer_u",
    )(experts_t, x3, tab, gates)


def _expand_matrix():
    k = np.arange(N_PAIRS)[:, None]
    c = np.arange(N_PAIRS * D_ROWS)[None, :]
    return (c // D_ROWS == k).astype(np.float32)


def _peer_v_kernel(e_hbm, w_ref, x_ref, tab_ref, expand_ref, g_ref, b_ref, o_ref, y_ref, wide_ref,
                   e_smem_a, e_smem_b, sems, *, tb, alpha):
    n_col = N_PAIRS * D_ROWS
    ids = _ExpertIds(e_hbm, (e_smem_a, e_smem_b), sems, tb)
    ids.start_first_block()
    diag = (lax.broadcasted_iota(jnp.int32, (D_ROWS, n_col), 1) % D_ROWS
            == lax.broadcasted_iota(jnp.int32, (D_ROWS, n_col), 0))
    part_row = lax.broadcasted_iota(jnp.int32, (SUBLANES, LANES), 0)

    def expand_tokens(i, carry):
        t0 = pl.multiple_of(i * TOKENS_PER_MATMUL, TOKENS_PER_MATMUL)
        parts = []
        for j in range(TOKENS_PER_MATMUL):
            wrow = w_ref[pl.ds(t0 + j, 1), :]
            hi = wrow.astype(BF16).astype(F32)
            mid = (wrow - hi).astype(BF16).astype(F32)
            lo = ((wrow - hi) - mid).astype(BF16).astype(F32)
            bc = lambda v: jnp.broadcast_to(v, (SUBLANES, LANES))
            parts.append(jnp.where(part_row == 0, bc(hi), jnp.where(part_row == 1, bc(mid),
                                                                    jnp.where(part_row == 2, bc(lo), 0.0))))
        lhs = jnp.concatenate(parts, axis=0).astype(BF16)
        wide = jnp.dot(lhs, expand_ref[...], preferred_element_type=F32)
        wide_ref[pl.ds(t0, TOKENS_PER_MATMUL)] = wide.reshape(TOKENS_PER_MATMUL, SUBLANES, n_col)
        return carry

    lax.fori_loop(0, tb // TOKENS_PER_MATMUL, expand_tokens, 0)

    def left_operand(t):
        rows = [jnp.where(diag, jnp.broadcast_to(wide_ref[t, p:p + 1, :], (D_ROWS, n_col)), 0.0) for p in range(3)]
        return jnp.concatenate(rows, axis=0).astype(BF16)

    def weighted_sum(h, t, a):
        rhs = jnp.concatenate([_expert_row(tab_ref, ids(h, k, t)) for k in range(N_PAIRS)], axis=0)
        y3 = jnp.dot(a, rhs, preferred_element_type=F32)
        return (y3[0:D_ROWS] + y3[D_ROWS:2 * D_ROWS]) + y3[2 * D_ROWS:]

    tokens_per_step = 32

    def some_tokens(h, i, carry):
        ts = [tokens_per_step * i + j for j in range(tokens_per_step)]
        ys = [weighted_sum(h, t, left_operand(t)) for t in ts]
        for t, y in zip(ts, ys):
            y_ref[t] = y
        return carry

    for h in range(2):
        ids.wait_half(h)
        lax.fori_loop(h * tb // (2 * tokens_per_step), (h + 1) * tb // (2 * tokens_per_step),
                      functools.partial(some_tokens, h), 0)
        ids.prefetch_half(h)
    z = alpha * x_ref[...] + y_ref[...]
    inv_d = 1.0 / D_MODEL
    mu = jnp.sum(jnp.sum(z, axis=2, keepdims=True), axis=1, keepdims=True) * inv_d
    zc = z - mu
    var = jnp.sum(jnp.sum(zc * zc, axis=2, keepdims=True), axis=1, keepdims=True) * inv_d
    o_ref[...] = zc * lax.rsqrt(var + LN_EPS) * g_ref[...] + b_ref[...]


def _peer_v(experts_t, wgt, x3, tab, g, b, alpha, tb=PEER_TOKENS):
    T = x3.shape[0]
    return pl.pallas_call(
        functools.partial(_peer_v_kernel, tb=tb, alpha=alpha),
        grid=(T // tb,),
        in_specs=[pl.BlockSpec(memory_space=pl.ANY),
                  pl.BlockSpec((tb, N_PAIRS), lambda i: (i, 0)),
                  pl.BlockSpec((tb, D_ROWS, LANES), lambda i: (i, 0, 0)),
                  pl.BlockSpec((N_EXPERTS * ROWS_PER_EXPERT, LANES), lambda i: (0, 0), pipeline_mode=pl.Buffered(1)),
                  pl.BlockSpec((N_PAIRS, N_PAIRS * D_ROWS), lambda i: (0, 0)),
                  pl.BlockSpec((1, D_ROWS, LANES), lambda i: (0, 0, 0)),
                  pl.BlockSpec((1, D_ROWS, LANES), lambda i: (0, 0, 0))],
        out_specs=pl.BlockSpec((tb, D_ROWS, LANES), lambda i: (i, 0, 0)),
        out_shape=jax.ShapeDtypeStruct((T, D_ROWS, LANES), F32),
        scratch_shapes=[pltpu.VMEM((tb, D_ROWS, LANES), F32), pltpu.VMEM((tb, SUBLANES, N_PAIRS * D_ROWS), F32),
                        pltpu.SMEM((N_PAIRS, tb // 2), jnp.int32), pltpu.SMEM((N_PAIRS, tb // 2), jnp.int32),
                        pltpu.SemaphoreType.DMA((2,))],
        compiler_params=_cparams(("arbitrary",)),
        name="peer_v",
    )(experts_t, wgt, x3, tab, jnp.asarray(_expand_matrix(), dtype=BF16),
      g.reshape(1, D_ROWS, LANES), b.reshape(1, D_ROWS, LANES))


def kernel(x, w_in, b_in, conv_w, conv_b, lru_wa, lru_ba, lru_wx, lru_bx, lru_lambda, cmp_pos_k, cmpk_w1, cmpk_b1, cmpk_w2, cmpk_b2, cmp_pos_v, cmpv_w1, cmpv_b1, cmpv_w2, cmpv_b2, gn_lru_g, gn_nsa_g, w_out, ln1_g, ln1_b, peer_wq, peer_subkeys, peer_u, peer_v, ln2_g, ln2_b):
    B, S, D = x.shape
    T = B * S
    depth = w_in.shape[0]
    alpha = (2 * depth) ** 0.25
    xf = x.reshape(T, D)
    for l in range(depth):
        w_in_bf = jnp.pad(w_in[l], ((0, 0), (0, IN_COLS_PAD - IN_COLS))).astype(BF16)
        b_in_row = jnp.pad(b_in[l], (0, IN_COLS_PAD - IN_COLS)).reshape(1, IN_COLS_PAD)
        h_all, kraw, kbf, vtiles = _in_proj(xf, w_in_bf, b_in_row, B, S)

        y_lru = _rglru(h_all, B, S, conv_w[l], conv_b[l], _block_diag(lru_wa[l]), lru_ba[l],
                       _block_diag(lru_wx[l]), lru_bx[l], lru_lambda[l], gn_lru_g[l])

        w1 = jnp.stack([cmpk_w1[l], cmpv_w1[l]])
        w1_pos = w1.reshape(2, 2, CMP_STRIDE, HEAD_DIM, HEAD_DIM)
        w1l = jnp.concatenate([w1_pos[:, 0], w1_pos[:, 1]], axis=-1)
        pos = jnp.stack([cmp_pos_k[l], cmp_pos_v[l]]).reshape(2, 1, CMP_BLOCK * HEAD_DIM)
        posflat = jnp.broadcast_to(pos, (2, SUBLANES, CMP_BLOCK * HEAD_DIM))
        kcvc = _compress(kraw, w1l, w1, posflat,
                         jnp.stack([cmpk_b1[l], cmpv_b1[l]]).reshape(2, 1, HEAD_DIM),
                         jnp.stack([cmpk_w2[l], cmpv_w2[l]]),
                         jnp.stack([cmpk_b2[l], cmpv_b2[l]]).reshape(2, 1, HEAD_DIM))
        kcvc = kcvc.reshape(2, B, N_KV, S // CMP_STRIDE, HEAD_DIM)
        y_nsa = _nsa(h_all, B, S, kcvc[0], kcvc[1], kbf[0], vtiles[0], kbf[1], vtiles[1], gn_nsa_g[l])

        x1 = _out_ln(y_lru, y_nsa, xf, w_out[l].astype(BF16), ln1_g[l], ln1_b[l], alpha)

        experts_t, gates = _peer_route(x1, peer_wq[l].astype(BF16), peer_subkeys[l])
        x3 = x1.reshape(T, D_ROWS, LANES)
        wgt = _peer_u(experts_t, x3, _pack_table(peer_u[l]), gates)
        xf = _peer_v(experts_t, wgt, x3, _pack_table(peer_v[l]), ln2_g[l], ln2_b[l], alpha).reshape(T, D)
    return xf.reshape(B, S, D)
```

```python
import functools

import numpy as np
import jax
import jax.numpy as jnp
from jax import lax
from jax.experimental import pallas as pl
from jax.experimental.pallas import tpu as pltpu

D_MODEL = 1024
LRU_WIDTH = 512
LRU_BLOCKS = 8
LRU_BLOCK_DIM = LRU_WIDTH // LRU_BLOCKS
CONV_WIDTH = 4
LRU_C = 8.0
N_HEADS = 8
HEAD_DIM = 64
N_KV = 2
Q_PER_KV = N_HEADS // N_KV
NSA_WIDTH = N_HEADS * HEAD_DIM
KV_WIDTH = N_KV * HEAD_DIM
N_BRANCHES = 3
CMP_BLOCK = 32
CMP_STRIDE = 16
SEL_BLOCK = 64
SEL_TOPN = 16
WINDOW = 512
Q_BLOCK = 128
FORCE_BONUS = 1e4
NEG = -1e30
REMOVED = -3.0e38
PEER_HEADS = 8
PEER_KEYS = 128
PEER_TOPK = 16
PEER_KEY_DIM = 128
N_EXPERTS = PEER_KEYS * PEER_KEYS
N_PAIRS = PEER_HEADS * PEER_TOPK
LN_EPS = 1e-5
IN_SPLITS = (LRU_WIDTH, LRU_WIDTH, NSA_WIDTH) + (KV_WIDTH,) * 6 + (N_HEADS * N_BRANCHES,)
IN_COLS = sum(IN_SPLITS)

LANES = 128
SUBLANES = 8
VMEM_LIMIT_BYTES = 56 * 1024 * 1024

IN_COLS_PAD = -(-IN_COLS // LANES) * LANES
COL_LRU_X = 0
COL_LRU_GATE = LRU_WIDTH
COL_Q = 2 * LRU_WIDTH
COL_KV = COL_Q + NSA_WIDTH
COL_GATES = COL_KV + 6 * KV_WIDTH

HI = lax.Precision.HIGHEST
F32 = jnp.float32
BF16 = jnp.bfloat16

D_ROWS = D_MODEL // LANES


def _cparams(sem):
    return pltpu.CompilerParams(dimension_semantics=sem, vmem_limit_bytes=VMEM_LIMIT_BYTES)


def _dot_nt(a, b, precision=None):
    return lax.dot_general(a, b, (((1,), (1,)), ((), ())), precision=precision,
                           preferred_element_type=F32)


def _split2(a):
    hi = a.astype(BF16)
    return hi, (a - hi.astype(F32)).astype(BF16)


def _dot_split(a, b):
    a_hi, a_lo = _split2(a)
    b_hi, b_lo = _split2(b)
    dot = lambda u, v: jnp.dot(u, v, preferred_element_type=F32)
    return dot(a_hi, b_hi) + (dot(a_hi, b_lo) + dot(a_lo, b_hi))


def _layer_norm_rows(z, g, b):
    mu = jnp.mean(z, axis=-1, keepdims=True)
    zc = z - mu
    var = jnp.mean(zc * zc, axis=-1, keepdims=True)
    return zc * lax.rsqrt(var + LN_EPS) * g + b


V_AUG_ROWS = HEAD_DIM + 16


def _in_proj_kernel(x_ref, w_ref, b_ref, o_ref, kraw_ref, kbf_ref, vt_ref):
    h = jnp.dot(x_ref[...].astype(BF16), w_ref[...], preferred_element_type=F32) + b_ref[...]
    o_ref[...] = h
    group_cols = lambda col, g: slice(col + g * HEAD_DIM, col + (g + 1) * HEAD_DIM)
    for j in range(2):
        for g in range(N_KV):
            kraw_ref[j, 0, g] = h[:, group_cols(COL_KV + j * KV_WIDTH, g)]
            kbf_ref[j, 0, g] = h[:, group_cols(COL_KV + (2 + 2 * j) * KV_WIDTH, g)].astype(BF16)
    ones_row = (lax.broadcasted_iota(jnp.int32, (V_AUG_ROWS - HEAD_DIM, Q_BLOCK), 0) == 0).astype(F32)
    for j in range(2):
        col = COL_KV + (3 + 2 * j) * KV_WIDTH
        for qb in range(h.shape[0] // Q_BLOCK):
            v_t = h[qb * Q_BLOCK:(qb + 1) * Q_BLOCK, col:col + KV_WIDTH].T
            for g in range(N_KV):
                tile = jnp.concatenate([v_t[g * HEAD_DIM:(g + 1) * HEAD_DIM, :], ones_row], axis=0)
                vt_ref[j, 0, g, qb] = tile.astype(BF16)


def _in_proj(xf, w_bf, b_row, B, S, tm=512):
    T = xf.shape[0]
    tm = min(tm, S)
    nt = S // tm
    per_tile = lambda *dims: pl.BlockSpec((2, 1, N_KV) + dims, lambda i: (0, i // nt, 0, i % nt) + (0,) * (len(dims) - 1))
    return pl.pallas_call(
        _in_proj_kernel,
        grid=(T // tm,),
        in_specs=[pl.BlockSpec((tm, D_MODEL), lambda i: (i, 0)),
                  pl.BlockSpec((D_MODEL, IN_COLS_PAD), lambda i: (0, 0)),
                  pl.BlockSpec((1, IN_COLS_PAD), lambda i: (0, 0))],
        out_specs=[pl.BlockSpec((tm, IN_COLS_PAD), lambda i: (i, 0)),
                   per_tile(tm, HEAD_DIM), per_tile(tm, HEAD_DIM),
                   per_tile(tm // Q_BLOCK, V_AUG_ROWS, Q_BLOCK)],
        out_shape=[jax.ShapeDtypeStruct((T, IN_COLS_PAD), F32),
                   jax.ShapeDtypeStruct((2, B, N_KV, S, HEAD_DIM), F32),
                   jax.ShapeDtypeStruct((2, B, N_KV, S, HEAD_DIM), BF16),
                   jax.ShapeDtypeStruct((2, B, N_KV, S // Q_BLOCK, V_AUG_ROWS, Q_BLOCK), BF16)],
        compiler_params=_cparams(("arbitrary",)),
        name="in_proj",
    )(xf, w_bf, b_row)


def _rglru_kernel(x_ref, gate_ref, cw_ref, cb_ref, wa_ref, ba_ref, wx_ref, bx_ref, lam_ref, gn_ref,
                  o_ref, xprev_ref, h_ref, *, ts):
    i = pl.program_id(1)

    @pl.when(i == 0)
    def _():
        xprev_ref[...] = jnp.zeros_like(xprev_ref)
        h_ref[...] = jnp.zeros_like(h_ref)

    xb = x_ref[...]
    xp = xprev_ref[...]
    row = lax.broadcasted_iota(jnp.int32, xb.shape, 0)
    xc = cw_ref[CONV_WIDTH - 1:CONV_WIDTH, :] * xb + cb_ref[...]
    for j in range(1, CONV_WIDTH):
        shifted = jnp.where(row >= j, pltpu.roll(xb, j, axis=0), pltpu.roll(xp, j, axis=0))
        xc = xc + cw_ref[CONV_WIDTH - 1 - j:CONV_WIDTH - j, :] * shifted
    xprev_ref[...] = xb

    r = jax.nn.sigmoid(_dot_split(xc, wa_ref[...]) + ba_ref[...])
    ig = jax.nn.sigmoid(_dot_split(xc, wx_ref[...]) + bx_ref[...])
    lam = lam_ref[...]
    softplus_neg_lam = jnp.maximum(-lam, 0.0) + jnp.log1p(jnp.exp(-jnp.abs(lam)))
    log_a = -LRU_C * r * softplus_neg_lam
    a = jnp.exp(log_a)
    th = jnp.tanh(log_a)
    u = jnp.sqrt(-2.0 * th / (1.0 - th)) * (ig * xc)

    d = 1
    while d < ts:
        keep = row >= d
        a_sh = pltpu.roll(a, d, axis=0)
        u_sh = pltpu.roll(u, d, axis=0)
        u = jnp.where(keep, a * u_sh + u, u)
        a = jnp.where(keep, a * a_sh, a)
        d *= 2
    h = u + a * h_ref[0:1, :]
    h_ref[0:1, :] = h[ts - 1:ts, :]

    y = h * jax.nn.gelu(gate_ref[...])
    y = y * lax.rsqrt(jnp.mean(y * y, axis=-1, keepdims=True) + LN_EPS) * gn_ref[...]
    o_ref[...] = y


def _rglru(h_all, B, S, cw, cb, wa_bd, ba, wx_bd, bx, lam, gn, ts=512):
    ts = min(ts, S)
    nt = S // ts
    C = LRU_WIDTH
    row = lambda v: v.reshape(1, C)
    const = lambda shape: pl.BlockSpec(shape, lambda b, i: (0,) * len(shape))
    return pl.pallas_call(
        functools.partial(_rglru_kernel, ts=ts),
        grid=(B, nt),
        in_specs=[pl.BlockSpec((ts, C), lambda b, i: (b * nt + i, COL_LRU_X // C)),
                  pl.BlockSpec((ts, C), lambda b, i: (b * nt + i, COL_LRU_GATE // C)),
                  const((CONV_WIDTH, C)), const((1, C)), const((C, C)), const((1, C)),
                  const((C, C)), const((1, C)), const((1, C)), const((1, C))],
        out_specs=pl.BlockSpec((ts, C), lambda b, i: (b * nt + i, 0)),
        out_shape=jax.ShapeDtypeStruct((B * S, C), F32),
        scratch_shapes=[pltpu.VMEM((ts, C), F32), pltpu.VMEM((SUBLANES, C), F32)],
        compiler_params=_cparams(("arbitrary", "arbitrary")),
        name="rglru",
    )(h_all, h_all, cw, row(cb), wa_bd, row(ba), wx_bd, row(bx), row(lam), row(gn))


def _block_diag(w):
    n, d, e = w.shape
    eye = jnp.eye(n, dtype=w.dtype)
    return (w[:, :, None, :] * eye[:, None, :, None]).reshape(n * d, n * e)


def _compress_kernel(k_ref, w1l_ref, w1_ref, pos_ref, b1_ref, w2_ref, b2_ref, o_ref):
    nr = o_ref.shape[2]
    ab = jnp.zeros((nr, 2 * HEAD_DIM), F32)
    for l in range(CMP_STRIDE):
        k_l = k_ref[0, 0, 0, pl.ds(l, nr, stride=CMP_STRIDE), :]
        ab = ab + jnp.dot(k_l, w1l_ref[0, l], precision=HI, preferred_element_type=F32)
    ab_next = pltpu.roll(ab, nr - 1, axis=0)
    pre = ab[:, :HEAD_DIM] + ab_next[:, HEAD_DIM:]
    posc = jnp.dot(pos_ref[0], w1_ref[0], precision=HI, preferred_element_type=F32)[0:1, :]
    h1 = jax.nn.gelu(pre + posc + b1_ref[0])
    o_ref[0, 0] = jnp.dot(h1, w2_ref[0], precision=HI, preferred_element_type=F32) + b2_ref[0]


def _compress(kraw, w1l, w1, posflat, b1, w2, b2):
    _, B, G, S, _ = kraw.shape
    nr = S // CMP_STRIDE
    fan = CMP_BLOCK * HEAD_DIM
    wspec = lambda shape: pl.BlockSpec((1,) + shape, lambda s, i: (s,) + (0,) * len(shape))
    return pl.pallas_call(
        _compress_kernel,
        grid=(2, B * G),
        in_specs=[pl.BlockSpec((1, 1, 1, S, HEAD_DIM), lambda s, i: (s, i // G, i % G, 0, 0)),
                  wspec((CMP_STRIDE, HEAD_DIM, 2 * HEAD_DIM)), wspec((fan, HEAD_DIM)), wspec((SUBLANES, fan)),
                  wspec((1, HEAD_DIM)), wspec((HEAD_DIM, HEAD_DIM)), wspec((1, HEAD_DIM))],
        out_specs=pl.BlockSpec((1, 1, nr, HEAD_DIM), lambda s, i: (s, i, 0, 0)),
        out_shape=jax.ShapeDtypeStruct((2, B * G, nr, HEAD_DIM), F32),
        compiler_params=_cparams(("arbitrary", "arbitrary")),
        name="compress",
    )(kraw, w1l, w1, posflat, b1, w2, b2)


def _selection_matrix(n_rows, n_cmp, n_sel):
    cs = np.arange(n_rows)[:, None] * CMP_STRIDE
    ss = np.arange(n_sel)[None, :] * SEL_BLOCK
    ov = np.clip(np.minimum(cs + CMP_BLOCK, ss + SEL_BLOCK) - np.maximum(cs, ss), 0, None)
    ov = (ov / CMP_STRIDE).astype(np.float32)
    ov[n_cmp:] = 0.0
    return ov


SEL_TILE = 4 * Q_BLOCK
WIN_SPAN = WINDOW + Q_BLOCK


def _nsa_kernel(q_ref, gt_ref, kc_ref, vct_ref, ks_ref, vst_ref, kw_ref, vwt_ref, slope_ref, bias_ref,
                selmat_ref, emat_ref, gn_ref, o_ref, s_scr, *, n_cmp, n_top, win_span):
    qt = pl.program_id(1)
    q0 = qt * Q_BLOCK
    R = Q_PER_KV
    cols = R * Q_BLOCK
    n_rows_c = kc_ref.shape[2]
    n_sel = selmat_ref.shape[0]
    q_loc = lax.broadcasted_iota(jnp.int32, (1, cols), 1) % Q_BLOCK
    t_col = q0 + q_loc
    gates_t = jax.nn.sigmoid(gt_ref[...]).T

    def step(carry, qs_bf, slope, bias, k_bf, v_tiles, offset, ok):
        m, acc = carry
        s = jnp.where(ok, _dot_nt(k_bf, qs_bf) + bias, NEG)
        c = slope * (-offset).astype(F32)
        m_new = jnp.maximum(m, jnp.max(s, axis=0, keepdims=True) + c)
        p = jnp.exp(s + (c - m_new)).astype(BF16)
        acc = jnp.exp(m - m_new) * acc
        for i, v in enumerate(v_tiles):
            acc = acc + jnp.dot(v, p[i * Q_BLOCK:(i + 1) * Q_BLOCK, :], preferred_element_type=F32)
        return m_new, acc

    init = (jnp.full((1, cols), NEG, F32), jnp.zeros((V_AUG_ROWS, cols), F32))
    finish = lambda acc: acc[0:HEAD_DIM, :] / acc[HEAD_DIM:HEAD_DIM + 1, :]
    y_rows = []
    for g in range(N_KV):
        qg = q_ref[:, g * R * HEAD_DIM:(g + 1) * R * HEAD_DIM] * (HEAD_DIM ** -0.5)
        qs = jnp.concatenate([qg[:, r * HEAD_DIM:(r + 1) * HEAD_DIM] for r in range(R)], axis=0)
        qs_bf = qs.astype(BF16)
        slope = slope_ref[g]

        kc = kc_ref[0, g]
        c_idx = lax.broadcasted_iota(jnp.int32, (n_rows_c, 1), 0)
        cmp_end = c_idx * CMP_STRIDE + (CMP_BLOCK - 1)
        valid_c = (cmp_end <= t_col) & (c_idx < n_cmp)
        q_hi, q_lo = _split2(qs)
        k_hi, k_lo = _split2(kc)
        s = _dot_nt(k_hi, q_hi) + (_dot_nt(k_hi, q_lo) + _dot_nt(k_lo, q_hi))
        s = s - slope * jnp.abs(t_col - cmp_end).astype(F32)
        s = jnp.where(valid_c, s, NEG)
        e = jnp.exp(s - jnp.max(s, axis=0, keepdims=True))
        p_c = e / jnp.sum(e, axis=0, keepdims=True)
        p_c = p_c * (t_col >= CMP_BLOCK - 1).astype(F32)
        o_c = jnp.dot(vct_ref[0, g].astype(BF16), p_c.astype(BF16), preferred_element_type=F32)

        p_sum = p_c[:, 0:Q_BLOCK]
        for r in range(1, R):
            p_sum = p_sum + p_c[:, r * Q_BLOCK:(r + 1) * Q_BLOCK]
        p_hi, p_lo = _split2(p_sum)
        dot = lambda a, b: jnp.dot(a, b, preferred_element_type=F32)
        imp_t = dot(selmat_ref[...], p_hi) + dot(selmat_ref[...], p_lo)
        j_idx = lax.broadcasted_iota(jnp.int32, (n_sel, Q_BLOCK), 0)
        j_f = j_idx.astype(F32)
        cur = t_col[:, 0:Q_BLOCK] // SEL_BLOCK
        forced = (j_idx == 0) | (j_idx == cur) | (j_idx == cur - 1)
        work = jnp.where(j_idx <= cur, imp_t + jnp.where(forced, FORCE_BONUS, 0.0), NEG)
        msel_t = jnp.zeros((n_sel, Q_BLOCK), F32)
        for _ in range(n_top):
            mx = jnp.max(work, axis=0, keepdims=True)
            first = jnp.min(jnp.where(work == mx, j_f, float(LANES)), axis=0, keepdims=True)
            pick = j_f == first
            msel_t = jnp.where(pick, 1.0, msel_t)
            work = jnp.where(pick, REMOVED, work)
        if n_sel < LANES:
            msel_t = jnp.concatenate([msel_t, jnp.zeros((LANES - n_sel, Q_BLOCK), F32)], axis=0)
        msel_bf = msel_t.astype(BF16)

        sel_key = lax.broadcasted_iota(jnp.int32, (SEL_TILE, cols), 0)
        bias_sel = bias_ref[g, 0:SEL_TILE, :]

        def sel_scores(kt, causal):
            k0 = pl.multiple_of(kt * SEL_TILE, SEL_TILE)
            ch = jnp.dot(emat_ref[kt], msel_bf, preferred_element_type=F32)
            ok = jnp.concatenate([ch] * R, axis=1) > 0.5
            if causal:
                ok = ok & (sel_key <= q_loc + (q0 - k0))
            s = jnp.where(ok, _dot_nt(ks_ref[0, g, pl.ds(k0, SEL_TILE), :], qs_bf) + bias_sel, NEG)
            return s, jnp.max(s, axis=0, keepdims=True)

        def sel_consume(m, acc, s_max, kt):
            c = slope * (kt * SEL_TILE - q0).astype(F32)
            m_new = jnp.maximum(m, s_max + c)
            p = jnp.exp(s_scr[...] + (c - m_new)).astype(BF16)
            acc = jnp.exp(m - m_new) * acc
            for i in range(SEL_TILE // Q_BLOCK):
                acc = acc + jnp.dot(vst_ref[0, g, kt * (SEL_TILE // Q_BLOCK) + i],
                                    p[i * Q_BLOCK:(i + 1) * Q_BLOCK, :], preferred_element_type=F32)
            return m_new, acc

        kt_diag = q0 // SEL_TILE
        s_first, max_first = sel_scores(kt_diag, True)
        s_scr[...] = s_first

        def sel_body(kt, lc):
            m, acc, s_max, kt_parked = lc
            s_next, max_next = sel_scores(kt, False)
            m, acc = sel_consume(m, acc, s_max, kt_parked)
            s_scr[...] = s_next
            return m, acc, max_next, kt

        m_s, acc_s, max_last, kt_last = lax.fori_loop(0, kt_diag, sel_body, init + (max_first, kt_diag))
        _, acc_s = sel_consume(m_s, acc_s, max_last, kt_last)
        o_s = finish(acc_s)

        w0 = pl.multiple_of(jnp.maximum(q0 + Q_BLOCK - win_span, 0), Q_BLOCK)
        win_key = lax.broadcasted_iota(jnp.int32, (win_span, cols), 0)
        reach = q_loc + (q0 - w0)
        ok_w = (win_key <= reach) & (win_key > reach - WINDOW)
        v_tiles = [vwt_ref[0, g, w0 // Q_BLOCK + i] for i in range(win_span // Q_BLOCK)]
        carry = step(init, qs_bf, slope, bias_ref[g, 0:win_span, :], kw_ref[0, g, pl.ds(w0, win_span), :],
                     v_tiles, q0 - w0, ok_w)
        o_w = finish(carry[1])

        for r in range(R):
            col = (g * R + r) * N_BRANCHES
            sl = slice(r * Q_BLOCK, (r + 1) * Q_BLOCK)
            y_rows.append(gates_t[col:col + 1, :] * o_c[:, sl] + gates_t[col + 1:col + 2, :] * o_s[:, sl]
                          + gates_t[col + 2:col + 3, :] * o_w[:, sl])
    y_t = jnp.concatenate(y_rows, axis=0)
    y_t = y_t * lax.rsqrt(jnp.mean(y_t * y_t, axis=0, keepdims=True) + LN_EPS) * gn_ref[...]
    o_ref[...] = y_t.T


def _nsa(h_all, B, S, kc, vc, ks_bf, vs_tiles, kw_bf, vw_tiles, gn):
    nq = S // Q_BLOCK
    n_cmp = (S - CMP_BLOCK) // CMP_STRIDE + 1
    n_rows_c = kc.shape[2]
    n_sel = S // SEL_BLOCK
    n_top = min(SEL_TOPN, n_sel)
    n_span = S // SEL_TILE
    win_span = min(WIN_SPAN, S)
    n_bias = max(win_span, SEL_TILE)
    cols = Q_PER_KV * Q_BLOCK
    slopes = np.array([2.0 ** (-8.0 * (h + 1) / N_HEADS) for h in range(N_HEADS)], np.float32)
    slope_cols = np.repeat(slopes.reshape(N_KV, Q_PER_KV), Q_BLOCK, axis=1)[:, None, :]
    rel = (np.arange(cols)[None, :] % Q_BLOCK - np.arange(n_bias)[:, None]).astype(np.float32)
    bias = -slope_cols * rel[None]
    selmat_t = jnp.asarray(_selection_matrix(n_rows_c, n_cmp, n_sel).T, dtype=BF16)
    key_blk = (np.arange(S) // SEL_BLOCK).reshape(n_span, SEL_TILE, 1)
    emat_t = jnp.asarray((key_blk == np.arange(LANES)[None, None, :]).astype(np.float32), dtype=BF16)
    qw = Q_PER_KV * HEAD_DIM * N_KV
    per_batch = lambda *dims: pl.BlockSpec((1,) + dims, lambda b, i: (b,) + (0,) * len(dims))
    const = lambda shape: pl.BlockSpec(shape, lambda b, i: (0,) * len(shape))
    return pl.pallas_call(
        functools.partial(_nsa_kernel, n_cmp=n_cmp, n_top=n_top, win_span=win_span),
        grid=(B, nq),
        in_specs=[pl.BlockSpec((Q_BLOCK, qw), lambda b, i: (b * nq + i, COL_Q // qw)),
                  pl.BlockSpec((Q_BLOCK, LANES), lambda b, i: (b * nq + i, COL_GATES // LANES)),
                  per_batch(N_KV, n_rows_c, HEAD_DIM), per_batch(N_KV, HEAD_DIM, n_rows_c),
                  per_batch(N_KV, S, HEAD_DIM), per_batch(N_KV, nq, V_AUG_ROWS, Q_BLOCK),
                  per_batch(N_KV, S, HEAD_DIM), per_batch(N_KV, nq, V_AUG_ROWS, Q_BLOCK),
                  const((N_KV, 1, cols)), const((N_KV, n_bias, cols)), const((n_sel, n_rows_c)),
                  const((n_span, SEL_TILE, LANES)), const((NSA_WIDTH, 1))],
        out_specs=pl.BlockSpec((Q_BLOCK, NSA_WIDTH), lambda b, i: (b * nq + i, 0)),
        out_shape=jax.ShapeDtypeStruct((B * S, NSA_WIDTH), F32),
        scratch_shapes=[pltpu.VMEM((SEL_TILE, cols), F32)],
        compiler_params=_cparams(("arbitrary", "arbitrary")),
        name="nsa",
    )(h_all, h_all, kc, vc.transpose(0, 1, 3, 2), ks_bf, vs_tiles, kw_bf, vw_tiles,
      jnp.asarray(slope_cols), jnp.asarray(bias), selmat_t, emat_t, gn.reshape(NSA_WIDTH, 1))


def _out_ln_kernel(ylru_ref, ynsa_ref, x_ref, w_ref, g_ref, b_ref, o_ref, *, alpha):
    mix = jnp.dot(ylru_ref[...].astype(BF16), w_ref[0:LRU_WIDTH, :], preferred_element_type=F32)
    mix = mix + jnp.dot(ynsa_ref[...].astype(BF16), w_ref[LRU_WIDTH:, :], preferred_element_type=F32)
    o_ref[...] = _layer_norm_rows(alpha * x_ref[...] + mix, g_ref[...], b_ref[...])


def _out_ln(ylru, ynsa, xf, w_bf, g, b, alpha, tm=512):
    T = xf.shape[0]
    const = lambda shape: pl.BlockSpec(shape, lambda i: (0,) * len(shape))
    return pl.pallas_call(
        functools.partial(_out_ln_kernel, alpha=alpha),
        grid=(T // tm,),
        in_specs=[pl.BlockSpec((tm, LRU_WIDTH), lambda i: (i, 0)),
                  pl.BlockSpec((tm, NSA_WIDTH), lambda i: (i, 0)),
                  pl.BlockSpec((tm, D_MODEL), lambda i: (i, 0)),
                  const((D_MODEL, D_MODEL)), const((1, D_MODEL)), const((1, D_MODEL))],
        out_specs=pl.BlockSpec((tm, D_MODEL), lambda i: (i, 0)),
        out_shape=jax.ShapeDtypeStruct((T, D_MODEL), F32),
        compiler_params=_cparams(("arbitrary",)),
        name="out_ln",
    )(ylru, ynsa, xf, w_bf, g.reshape(1, D_MODEL), b.reshape(1, D_MODEL))


def _topk_rows(vals, order, payload, n_keep):
    n = vals.shape[1]
    kidx = lax.broadcasted_iota(jnp.int32, (n_keep, n), 0)
    never = float(1 << 20)
    kept_v = jnp.zeros((n_keep, n), F32)
    kept_p = jnp.zeros((n_keep, n), F32)
    work = vals
    for r in range(n_keep):
        mx = jnp.max(work, axis=0, keepdims=True)
        first = jnp.min(jnp.where(work == mx, order, never), axis=0, keepdims=True)
        pick = order == first
        kept_v = jnp.where(kidx == r, mx, kept_v)
        if payload is None:
            kept_p = jnp.where(kidx == r, first, kept_p)
        else:
            kept_p = jnp.where(kidx == r, jnp.max(jnp.where(pick, payload, -1.0), axis=0, keepdims=True), kept_p)
        work = jnp.where(pick, REMOVED, work)
    return kept_v, kept_p


def _candidate_rows():
    ij = [(i, 0) for i in range(PEER_TOPK)]
    for j in range(1, SUBLANES):
        ij += [(i, j) for i in range(SUBLANES)]
    ij += [(0, j) for j in range(SUBLANES, PEER_TOPK)]
    flat = np.array([i * PEER_TOPK + j for i, j in ij], np.float32)
    valid = np.array([(i + 1) * (j + 1) <= PEER_TOPK for i, j in ij])
    return flat[:, None], np.where(valid, 0.0, REMOVED).astype(np.float32)[:, None]


def _candidates(a0, a1):
    return ([(a0, a1[0:1, :])]
            + [(a0[0:SUBLANES, :], a1[j:j + 1, :]) for j in range(1, SUBLANES)]
            + [(a0[0:1, :], a1[SUBLANES:PEER_TOPK, :])])


def _peer_route_kernel(x_ref, wq_ref, sk_ref, flat_ref, pad_ref, e_ref, g_ref):
    tm = x_ref.shape[0]
    q = jnp.dot(x_ref[...].astype(BF16), wq_ref[...], preferred_element_type=F32)
    key_idx = lax.broadcasted_iota(jnp.int32, (PEER_KEYS, tm), 0).astype(F32)
    flat = jnp.broadcast_to(flat_ref[...], (flat_ref.shape[0], tm))
    experts, gates = [], []
    for h in range(PEER_HEADS):
        sv, si = [], []
        for c in range(2):
            col = (h * 2 + c) * PEER_KEY_DIM
            s_t = _dot_nt(sk_ref[c], q[:, col:col + PEER_KEY_DIM], HI)
            v, i = _topk_rows(s_t, key_idx, None, PEER_TOPK)
            sv.append(v)
            si.append(i)
        cand = jnp.concatenate([a + b for a, b in _candidates(sv[0], sv[1])], axis=0) + pad_ref[...]
        ecand = jnp.concatenate([a * float(PEER_KEYS) + b for a, b in _candidates(si[0], si[1])], axis=0)
        cv, ce = _topk_rows(cand, flat, ecand, PEER_TOPK)
        ex = jnp.exp(cv - cv[0:1, :])
        gates.append(ex / jnp.sum(ex, axis=0, keepdims=True))
        experts.append(ce)
    e_ref[...] = (jnp.concatenate(experts, axis=0) * float(ROWS_PER_EXPERT)).astype(jnp.int32)
    g_ref[...] = jnp.concatenate(gates, axis=0).T


def _peer_route(x1, wq_bf, subkeys, tm=256):
    T = x1.shape[0]
    tm = min(tm, T)
    nq = wq_bf.shape[1]
    flat, pad = _candidate_rows()
    n_cand = flat.shape[0]
    out = lambda: pl.BlockSpec((tm, N_PAIRS), lambda i: (i, 0))
    return pl.pallas_call(
        _peer_route_kernel,
        grid=(T // tm,),
        in_specs=[pl.BlockSpec((tm, D_MODEL), lambda i: (i, 0)),
                  pl.BlockSpec((D_MODEL, nq), lambda i: (0, 0)),
                  pl.BlockSpec((2, PEER_KEYS, PEER_KEY_DIM), lambda i: (0, 0, 0)),
                  pl.BlockSpec((n_cand, 1), lambda i: (0, 0)),
                  pl.BlockSpec((n_cand, 1), lambda i: (0, 0))],
        out_specs=[pl.BlockSpec((N_PAIRS, tm), lambda i: (0, i)), out()],
        out_shape=[jax.ShapeDtypeStruct((N_PAIRS, T), jnp.int32),
                   jax.ShapeDtypeStruct((T, N_PAIRS), F32)],
        compiler_params=_cparams(("arbitrary",)),
        name="peer_route",
    )(x1, wq_bf, subkeys, jnp.asarray(flat), jnp.asarray(pad))


ROWS_PER_EXPERT = D_ROWS // 2
TOKENS_PER_MATMUL = 32
PEER_TOKENS = 2 * LANES


class _ExpertIds:
    def __init__(self, e_hbm, e_smem_halves, sems, tb):
        self.e_hbm, self.e_smem, self.sems, self.tb, self.half = e_hbm, e_smem_halves, sems, tb, tb // 2
        self.step = pl.program_id(0)
        self.n_steps = pl.num_programs(0)

    def _copy(self, block, h):
        return pltpu.make_async_copy(self.e_hbm.at[:, pl.ds(block * self.tb + h * self.half, self.half)],
                                     self.e_smem[h], self.sems.at[h])

    def start_first_block(self):
        @pl.when(self.step == 0)
        def _():
            self._copy(0, 0).start()
            self._copy(0, 1).start()

    def wait_half(self, h):
        self._copy(self.step, h).wait()

    def prefetch_half(self, h):
        @pl.when(self.step + 1 < self.n_steps)
        def _():
            self._copy(self.step + 1, h).start()

    def __call__(self, h, k, t):
        return self.e_smem[h].at[k][t - h * self.half]


def _pack_table(tab):
    t = tab.astype(BF16).reshape(N_EXPERTS, ROWS_PER_EXPERT, 2, LANES).transpose(0, 1, 3, 2)
    return lax.bitcast_convert_type(t, jnp.int32).reshape(N_EXPERTS * ROWS_PER_EXPERT, LANES)


def _expert_row(tab_ref, offset):
    words = tab_ref[pl.ds(pl.multiple_of(offset, ROWS_PER_EXPERT), ROWS_PER_EXPERT), :]
    return pltpu.bitcast(words, BF16)


def _sublane_sums(ps, roll, where, sub):
    lvl = ps
    for r in (1, 2, 4):
        m = (sub % (2 * r)) < r
        half = len(lvl) // 2
        lvl = [where(m, lvl[i], lvl[i + half]) + roll(where(m, lvl[i + half], lvl[i]), r) for i in range(half)]
    return lvl[0]


def _sum_src():
    sub = np.arange(SUBLANES)[:, None] * np.ones((1, LANES), np.int64)
    ps = [np.full((SUBLANES, LANES), 10.0 ** i) for i in range(SUBLANES)]
    out = _sublane_sums(ps, lambda v, r: np.roll(v, r, axis=0), np.where, sub)
    return [int(round(np.log10(out[s, 0] / SUBLANES))) for s in range(SUBLANES)]


_SUM_SRC = _sum_src()


def _peer_u_kernel(e_hbm, x_ref, tab_ref, gate_ref, w_ref, part_ref, zt_ref, e_smem_a, e_smem_b, sems, *, tb):
    sub = lax.broadcasted_iota(jnp.int32, (SUBLANES, LANES), 0)
    lane = lax.broadcasted_iota(jnp.int32, (N_PAIRS, tb), 1)
    ids = _ExpertIds(e_hbm, (e_smem_a, e_smem_b), sems, tb)
    ids.start_first_block()

    def gather_dots(h, t, slot):
        xt = x_ref[t]
        for grp in range(N_PAIRS // SUBLANES):
            prods = [None] * SUBLANES
            for s in range(SUBLANES):
                prods[_SUM_SRC[s]] = _expert_row(tab_ref, ids(h, grp * SUBLANES + s, t)).astype(F32) * xt
            part_ref[slot, grp * SUBLANES:(grp + 1) * SUBLANES, :] = _sublane_sums(
                prods, lambda v, r: pltpu.roll(v, r, axis=0), jnp.where, sub)

    n_slots = part_ref.shape[0]

    def lane_sums(t0):
        zt = zt_ref[...]
        for j in range(n_slots):
            zt = jnp.where(lane == t0 + j, jnp.sum(part_ref[j], axis=1, keepdims=True), zt)
        zt_ref[...] = zt

    part_ref[...] = jnp.zeros_like(part_ref)
    zt_ref[...] = jnp.zeros_like(zt_ref)

    def some_tokens(h, i, carry):
        lane_sums(n_slots * (i - 1))
        for j in range(n_slots):
            gather_dots(h, n_slots * i + j, j)
        return carry

    for h in range(2):
        ids.wait_half(h)
        lax.fori_loop(h * tb // (2 * n_slots), (h + 1) * tb // (2 * n_slots), functools.partial(some_tokens, h), 0)
        ids.prefetch_half(h)
    lane_sums(tb - n_slots)
    w_ref[...] = jax.nn.gelu(zt_ref[...].T) * gate_ref[...]


def _peer_u(experts_t, x3, tab, gates, tb=PEER_TOKENS):
    T = x3.shape[0]
    vm = lambda: pl.BlockSpec((tb, N_PAIRS), lambda i: (i, 0))
    return pl.pallas_call(
        functools.partial(_peer_u_kernel, tb=tb),
        grid=(T // tb,),
        in_specs=[pl.BlockSpec(memory_space=pl.ANY),
                  pl.BlockSpec((tb, D_ROWS, LANES), lambda i: (i, 0, 0)),
                  pl.BlockSpec((N_EXPERTS * ROWS_PER_EXPERT, LANES), lambda i: (0, 0), pipeline_mode=pl.Buffered(1)),
                  vm()],
        out_specs=vm(),
        out_shape=jax.ShapeDtypeStruct((T, N_PAIRS), F32),
        scratch_shapes=[pltpu.VMEM((4, N_PAIRS, LANES), F32), pltpu.VMEM((N_PAIRS, tb), F32),
                        pltpu.SMEM((N_PAIRS, tb // 2), jnp.int32), pltpu.SMEM((N_PAIRS, tb // 2), jnp.int32),
                        pltpu.SemaphoreType.DMA((2,))],
        compiler_params=_cparams(("arbitrary",)),
        name="peer_u",
    )(experts_t, x3, tab, gates)


def _expand_matrix():
    k = np.arange(N_PAIRS)[:, None]
    c = np.arange(N_PAIRS * D_ROWS)[None, :]
    return (c // D_ROWS == k).astype(np.float32)


def _peer_v_kernel(e_hbm, w_ref, x_ref, tab_ref, expand_ref, g_ref, b_ref, o_ref, y_ref, wide_ref,
                   e_smem_a, e_smem_b, sems, *, tb, alpha):
    n_col = N_PAIRS * D_ROWS
    ids = _ExpertIds(e_hbm, (e_smem_a, e_smem_b), sems, tb)
    ids.start_first_block()
    diag = (lax.broadcasted_iota(jnp.int32, (D_ROWS, n_col), 1) % D_ROWS
            == lax.broadcasted_iota(jnp.int32, (D_ROWS, n_col), 0))
    part_row = lax.broadcasted_iota(jnp.int32, (SUBLANES, LANES), 0)

    def expand_tokens(i, carry):
        t0 = pl.multiple_of(i * TOKENS_PER_MATMUL, TOKENS_PER_MATMUL)
        parts = []
        for j in range(TOKENS_PER_MATMUL):
            wrow = w_ref[pl.ds(t0 + j, 1), :]
            hi = wrow.astype(BF16).astype(F32)
            mid = (wrow - hi).astype(BF16).astype(F32)
            lo = ((wrow - hi) - mid).astype(BF16).astype(F32)
            bc = lambda v: jnp.broadcast_to(v, (SUBLANES, LANES))
            parts.append(jnp.where(part_row == 0, bc(hi), jnp.where(part_row == 1, bc(mid),
                                                                    jnp.where(part_row == 2, bc(lo), 0.0))))
        lhs = jnp.concatenate(parts, axis=0).astype(BF16)
        wide = jnp.dot(lhs, expand_ref[...], preferred_element_type=F32)
        wide_ref[pl.ds(t0, TOKENS_PER_MATMUL)] = wide.reshape(TOKENS_PER_MATMUL, SUBLANES, n_col)
        return carry

    lax.fori_loop(0, tb // TOKENS_PER_MATMUL, expand_tokens, 0)

    def left_operand(t):
        rows = [jnp.where(diag, jnp.broadcast_to(wide_ref[t, p:p + 1, :], (D_ROWS, n_col)), 0.0) for p in range(3)]
        return jnp.concatenate(rows, axis=0).astype(BF16)

    def weighted_sum(h, t, a):
        rhs = jnp.concatenate([_expert_row(tab_ref, ids(h, k, t)) for k in range(N_PAIRS)], axis=0)
        y3 = jnp.dot(a, rhs, preferred_element_type=F32)
        return (y3[0:D_ROWS] + y3[D_ROWS:2 * D_ROWS]) + y3[2 * D_ROWS:]

    tokens_per_step = 32

    def some_tokens(h, i, carry):
        ts = [tokens_per_step * i + j for j in range(tokens_per_step)]
        ys = [weighted_sum(h, t, left_operand(t)) for t in ts]
        for t, y in zip(ts, ys):
            y_ref[t] = y
        return carry

    for h in range(2):
        ids.wait_half(h)
        lax.fori_loop(h * tb // (2 * tokens_per_step), (h + 1) * tb // (2 * tokens_per_step),
                      functools.partial(some_tokens, h), 0)
        ids.prefetch_half(h)
    z = alpha * x_ref[...] + y_ref[...]
    inv_d = 1.0 / D_MODEL
    mu = jnp.sum(jnp.sum(z, axis=2, keepdims=True), axis=1, keepdims=True) * inv_d
    zc = z - mu
    var = jnp.sum(jnp.sum(zc * zc, axis=2, keepdims=True), axis=1, keepdims=True) * inv_d
    o_ref[...] = zc * lax.rsqrt(var + LN_EPS) * g_ref[...] + b_ref[...]


def _peer_v(experts_t, wgt, x3, tab, g, b, alpha, tb=PEER_TOKENS):
    T = x3.shape[0]
    return pl.pallas_call(
        functools.partial(_peer_v_kernel, tb=tb, alpha=alpha),
        grid=(T // tb,),
        in_specs=[pl.BlockSpec(memory_space=pl.ANY),
                  pl.BlockSpec((tb, N_PAIRS), lambda i: (i, 0)),
                  pl.BlockSpec((tb, D_ROWS, LANES), lambda i: (i, 0, 0)),
                  pl.BlockSpec((N_EXPERTS * ROWS_PER_EXPERT, LANES), lambda i: (0, 0), pipeline_mode=pl.Buffered(1)),
                  pl.BlockSpec((N_PAIRS, N_PAIRS * D_ROWS), lambda i: (0, 0)),
                  pl.BlockSpec((1, D_ROWS, LANES), lambda i: (0, 0, 0)),
                  pl.BlockSpec((1, D_ROWS, LANES), lambda i: (0, 0, 0))],
        out_specs=pl.BlockSpec((tb, D_ROWS, LANES), lambda i: (i, 0, 0)),
        out_shape=jax.ShapeDtypeStruct((T, D_ROWS, LANES), F32),
        scratch_shapes=[pltpu.VMEM((tb, D_ROWS, LANES), F32), pltpu.VMEM((tb, SUBLANES, N_PAIRS * D_ROWS), F32),
                        pltpu.SMEM((N_PAIRS, tb // 2), jnp.int32), pltpu.SMEM((N_PAIRS, tb // 2), jnp.int32),
                        pltpu.SemaphoreType.DMA((2,))],
        compiler_params=_cparams(("arbitrary",)),
        name="peer_v",
    )(experts_t, wgt, x3, tab, jnp.asarray(_expand_matrix(), dtype=BF16),
      g.reshape(1, D_ROWS, LANES), b.reshape(1, D_ROWS, LANES))


def kernel(x, w_in, b_in, conv_w, conv_b, lru_wa, lru_ba, lru_wx, lru_bx, lru_lambda, cmp_pos_k, cmpk_w1, cmpk_b1, cmpk_w2, cmpk_b2, cmp_pos_v, cmpv_w1, cmpv_b1, cmpv_w2, cmpv_b2, gn_lru_g, gn_nsa_g, w_out, ln1_g, ln1_b, peer_wq, peer_subkeys, peer_u, peer_v, ln2_g, ln2_b):
    B, S, D = x.shape
    T = B * S
    depth = w_in.shape[0]
    alpha = (2 * depth) ** 0.25
    xf = x.reshape(T, D)
    for l in range(depth):
        w_in_bf = jnp.pad(w_in[l], ((0, 0), (0, IN_COLS_PAD - IN_COLS))).astype(BF16)
        b_in_row = jnp.pad(b_in[l], (0, IN_COLS_PAD - IN_COLS)).reshape(1, IN_COLS_PAD)
        h_all, kraw, kbf, vtiles = _in_proj(xf, w_in_bf, b_in_row, B, S)

        y_lru = _rglru(h_all, B, S, conv_w[l], conv_b[l], _block_diag(lru_wa[l]), lru_ba[l],
                       _block_diag(lru_wx[l]), lru_bx[l], lru_lambda[l], gn_lru_g[l])

        w1 = jnp.stack([cmpk_w1[l], cmpv_w1[l]])
        w1_pos = w1.reshape(2, 2, CMP_STRIDE, HEAD_DIM, HEAD_DIM)
        w1l = jnp.concatenate([w1_pos[:, 0], w1_pos[:, 1]], axis=-1)
        pos = jnp.stack([cmp_pos_k[l], cmp_pos_v[l]]).reshape(2, 1, CMP_BLOCK * HEAD_DIM)
        posflat = jnp.broadcast_to(pos, (2, SUBLANES, CMP_BLOCK * HEAD_DIM))
        kcvc = _compress(kraw, w1l, w1, posflat,
                         jnp.stack([cmpk_b1[l], cmpv_b1[l]]).reshape(2, 1, HEAD_DIM),
                         jnp.stack([cmpk_w2[l], cmpv_w2[l]]),
                         jnp.stack([cmpk_b2[l], cmpv_b2[l]]).reshape(2, 1, HEAD_DIM))
        kcvc = kcvc.reshape(2, B, N_KV, S // CMP_STRIDE, HEAD_DIM)
        y_nsa = _nsa(h_all, B, S, kcvc[0], kcvc[1], kbf[0], vtiles[0], kbf[1], vtiles[1], gn_nsa_g[l])

        x1 = _out_ln(y_lru, y_nsa, xf, w_out[l].astype(BF16), ln1_g[l], ln1_b[l], alpha)

        experts_t, gates = _peer_route(x1, peer_wq[l].astype(BF16), peer_subkeys[l])
        x3 = x1.reshape(T, D_ROWS, LANES)
        wgt = _peer_u(experts_t, x3, _pack_table(peer_u[l]), gates)
        xf = _peer_v(experts_t, wgt, x3, _pack_table(peer_v[l]), ln2_g[l], ln2_b[l], alpha).reshape(T, D)
    return xf.reshape(B, S, D)
```

```python
import functools

import numpy as np
import jax
import jax.numpy as jnp
from jax import lax
from jax.experimental import pallas as pl
from jax.experimental.pallas import tpu as pltpu

D_MODEL = 1024
LRU_WIDTH = 512
LRU_BLOCKS = 8
LRU_BLOCK_DIM = LRU_WIDTH // LRU_BLOCKS
CONV_WIDTH = 4
LRU_C = 8.0
N_HEADS = 8
HEAD_DIM = 64
N_KV = 2
Q_PER_KV = N_HEADS // N_KV
NSA_WIDTH = N_HEADS * HEAD_DIM
KV_WIDTH = N_KV * HEAD_DIM
N_BRANCHES = 3
CMP_BLOCK = 32
CMP_STRIDE = 16
SEL_BLOCK = 64
SEL_TOPN = 16
WINDOW = 512
Q_BLOCK = 128
FORCE_BONUS = 1e4
NEG = -1e30
REMOVED = -3.0e38
PEER_HEADS = 8
PEER_KEYS = 128
PEER_TOPK = 16
PEER_KEY_DIM = 128
N_EXPERTS = PEER_KEYS * PEER_KEYS
N_PAIRS = PEER_HEADS * PEER_TOPK
LN_EPS = 1e-5
IN_SPLITS = (LRU_WIDTH, LRU_WIDTH, NSA_WIDTH) + (KV_WIDTH,) * 6 + (N_HEADS * N_BRANCHES,)
IN_COLS = sum(IN_SPLITS)

LANES = 128
SUBLANES = 8
VMEM_LIMIT_BYTES = 56 * 1024 * 1024

IN_COLS_PAD = -(-IN_COLS // LANES) * LANES
COL_LRU_X = 0
COL_LRU_GATE = LRU_WIDTH
COL_Q = 2 * LRU_WIDTH
COL_KV = COL_Q + NSA_WIDTH
COL_GATES = COL_KV + 6 * KV_WIDTH

HI = lax.Precision.HIGHEST
F32 = jnp.float32
BF16 = jnp.bfloat16

D_ROWS = D_MODEL // LANES


def _cparams(sem):
    return pltpu.CompilerParams(dimension_semantics=sem, vmem_limit_bytes=VMEM_LIMIT_BYTES)


def _dot_nt(a, b, precision=None):
    return lax.dot_general(a, b, (((1,), (1,)), ((), ())), precision=precision,
                           preferred_element_type=F32)


def _split2(a):
    hi = a.astype(BF16)
    return hi, (a - hi.astype(F32)).astype(BF16)


def _dot_split(a, b):
    a_hi, a_lo = _split2(a)
    b_hi, b_lo = _split2(b)
    dot = lambda u, v: jnp.dot(u, v, preferred_element_type=F32)
    return dot(a_hi, b_hi) + (dot(a_hi, b_lo) + dot(a_lo, b_hi))


def _layer_norm_rows(z, g, b):
    mu = jnp.mean(z, axis=-1, keepdims=True)
    zc = z - mu
    var = jnp.mean(zc * zc, axis=-1, keepdims=True)
    return zc * lax.rsqrt(var + LN_EPS) * g + b


V_AUG_ROWS = HEAD_DIM + 16


def _in_proj_kernel(x_ref, w_ref, b_ref, o_ref, kraw_ref, kbf_ref, vt_ref):
    h = jnp.dot(x_ref[...].astype(BF16), w_ref[...], preferred_element_type=F32) + b_ref[...]
    o_ref[...] = h
    group_cols = lambda col, g: slice(col + g * HEAD_DIM, col + (g + 1) * HEAD_DIM)
    for j in range(2):
        for g in range(N_KV):
            kraw_ref[j, 0, g] = h[:, group_cols(COL_KV + j * KV_WIDTH, g)]
            kbf_ref[j, 0, g] = h[:, group_cols(COL_KV + (2 + 2 * j) * KV_WIDTH, g)].astype(BF16)
    ones_row = (lax.broadcasted_iota(jnp.int32, (V_AUG_ROWS - HEAD_DIM, Q_BLOCK), 0) == 0).astype(F32)
    for j in range(2):
        col = COL_KV + (3 + 2 * j) * KV_WIDTH
        for qb in range(h.shape[0] // Q_BLOCK):
            v_t = h[qb * Q_BLOCK:(qb + 1) * Q_BLOCK, col:col + KV_WIDTH].T
            for g in range(N_KV):
                tile = jnp.concatenate([v_t[g * HEAD_DIM:(g + 1) * HEAD_DIM, :], ones_row], axis=0)
                vt_ref[j, 0, g, qb] = tile.astype(BF16)


def _in_proj(xf, w_bf, b_row, B, S, tm=512):
    T = xf.shape[0]
    tm = min(tm, S)
    nt = S // tm
    per_tile = lambda *dims: pl.BlockSpec((2, 1, N_KV) + dims, lambda i: (0, i // nt, 0, i % nt) + (0,) * (len(dims) - 1))
    return pl.pallas_call(
        _in_proj_kernel,
        grid=(T // tm,),
        in_specs=[pl.BlockSpec((tm, D_MODEL), lambda i: (i, 0)),
                  pl.BlockSpec((D_MODEL, IN_COLS_PAD), lambda i: (0, 0)),
                  pl.BlockSpec((1, IN_COLS_PAD), lambda i: (0, 0))],
        out_specs=[pl.BlockSpec((tm, IN_COLS_PAD), lambda i: (i, 0)),
                   per_tile(tm, HEAD_DIM), per_tile(tm, HEAD_DIM),
                   per_tile(tm // Q_BLOCK, V_AUG_ROWS, Q_BLOCK)],
        out_shape=[jax.ShapeDtypeStruct((T, IN_COLS_PAD), F32),
                   jax.ShapeDtypeStruct((2, B, N_KV, S, HEAD_DIM), F32),
                   jax.ShapeDtypeStruct((2, B, N_KV, S, HEAD_DIM), BF16),
                   jax.ShapeDtypeStruct((2, B, N_KV, S // Q_BLOCK, V_AUG_ROWS, Q_BLOCK), BF16)],
        compiler_params=_cparams(("arbitrary",)),
        name="in_proj",
    )(xf, w_bf, b_row)


def _rglru_kernel(x_ref, gate_ref, cw_ref, cb_ref, wa_ref, ba_ref, wx_ref, bx_ref, lam_ref, gn_ref,
                  o_ref, xprev_ref, h_ref, *, ts):
    i = pl.program_id(1)

    @pl.when(i == 0)
    def _():
        xprev_ref[...] = jnp.zeros_like(xprev_ref)
        h_ref[...] = jnp.zeros_like(h_ref)

    xb = x_ref[...]
    xp = xprev_ref[...]
    row = lax.broadcasted_iota(jnp.int32, xb.shape, 0)
    xc = cw_ref[CONV_WIDTH - 1:CONV_WIDTH, :] * xb + cb_ref[...]
    for j in range(1, CONV_WIDTH):
        shifted = jnp.where(row >= j, pltpu.roll(xb, j, axis=0), pltpu.roll(xp, j, axis=0))
        xc = xc + cw_ref[CONV_WIDTH - 1 - j:CONV_WIDTH - j, :] * shifted
    xprev_ref[...] = xb

    r = jax.nn.sigmoid(_dot_split(xc, wa_ref[...]) + ba_ref[...])
    ig = jax.nn.sigmoid(_dot_split(xc, wx_ref[...]) + bx_ref[...])
    lam = lam_ref[...]
    softplus_neg_lam = jnp.maximum(-lam, 0.0) + jnp.log1p(jnp.exp(-jnp.abs(lam)))
    log_a = -LRU_C * r * softplus_neg_lam
    a = jnp.exp(log_a)
    th = jnp.tanh(log_a)
    u = jnp.sqrt(-2.0 * th / (1.0 - th)) * (ig * xc)

    d = 1
    while d < ts:
        keep = row >= d
        a_sh = pltpu.roll(a, d, axis=0)
        u_sh = pltpu.roll(u, d, axis=0)
        u = jnp.where(keep, a * u_sh + u, u)
        a = jnp.where(keep, a * a_sh, a)
        d *= 2
    h = u + a * h_ref[0:1, :]
    h_ref[0:1, :] = h[ts - 1:ts, :]

    y = h * jax.nn.gelu(gate_ref[...])
    y = y * lax.rsqrt(jnp.mean(y * y, axis=-1, keepdims=True) + LN_EPS) * gn_ref[...]
    o_ref[...] = y


def _rglru(h_all, B, S, cw, cb, wa_bd, ba, wx_bd, bx, lam, gn, ts=512):
    ts = min(ts, S)
    nt = S // ts
    C = LRU_WIDTH
    row = lambda v: v.reshape(1, C)
    const = lambda shape: pl.BlockSpec(shape, lambda b, i: (0,) * len(shape))
    return pl.pallas_call(
        functools.partial(_rglru_kernel, ts=ts),
        grid=(B, nt),
        in_specs=[pl.BlockSpec((ts, C), lambda b, i: (b * nt + i, COL_LRU_X // C)),
                  pl.BlockSpec((ts, C), lambda b, i: (b * nt + i, COL_LRU_GATE // C)),
                  const((CONV_WIDTH, C)), const((1, C)), const((C, C)), const((1, C)),
                  const((C, C)), const((1, C)), const((1, C)), const((1, C))],
        out_specs=pl.BlockSpec((ts, C), lambda b, i: (b * nt + i, 0)),
        out_shape=jax.ShapeDtypeStruct((B * S, C), F32),
        scratch_shapes=[pltpu.VMEM((ts, C), F32), pltpu.VMEM((SUBLANES, C), F32)],
        compiler_params=_cparams(("arbitrary", "arbitrary")),
        name="rglru",
    )(h_all, h_all, cw, row(cb), wa_bd, row(ba), wx_bd, row(bx), row(lam), row(gn))


def _block_diag(w):
    n, d, e = w.shape
    eye = jnp.eye(n, dtype=w.dtype)
    return (w[:, :, None, :] * eye[:, None, :, None]).reshape(n * d, n * e)


def _compress_kernel(k_ref, w1l_ref, w1_ref, pos_ref, b1_ref, w2_ref, b2_ref, o_ref):
    nr = o_ref.shape[2]
    ab = jnp.zeros((nr, 2 * HEAD_DIM), F32)
    for l in range(CMP_STRIDE):
        k_l = k_ref[0, 0, 0, pl.ds(l, nr, stride=CMP_STRIDE), :]
        ab = ab + jnp.dot(k_l, w1l_ref[0, l], precision=HI, preferred_element_type=F32)
    ab_next = pltpu.roll(ab, nr - 1, axis=0)
    pre = ab[:, :HEAD_DIM] + ab_next[:, HEAD_DIM:]
    posc = jnp.dot(pos_ref[0], w1_ref[0], precision=HI, preferred_element_type=F32)[0:1, :]
    h1 = jax.nn.gelu(pre + posc + b1_ref[0])
    o_ref[0, 0] = jnp.dot(h1, w2_ref[0], precision=HI, preferred_element_type=F32) + b2_ref[0]


def _compress(kraw, w1l, w1, posflat, b1, w2, b2):
    _, B, G, S, _ = kraw.shape
    nr = S // CMP_STRIDE
    fan = CMP_BLOCK * HEAD_DIM
    wspec = lambda shape: pl.BlockSpec((1,) + shape, lambda s, i: (s,) + (0,) * len(shape))
    return pl.pallas_call(
        _compress_kernel,
        grid=(2, B * G),
        in_specs=[pl.BlockSpec((1, 1, 1, S, HEAD_DIM), lambda s, i: (s, i // G, i % G, 0, 0)),
                  wspec((CMP_STRIDE, HEAD_DIM, 2 * HEAD_DIM)), wspec((fan, HEAD_DIM)), wspec((SUBLANES, fan)),
                  wspec((1, HEAD_DIM)), wspec((HEAD_DIM, HEAD_DIM)), wspec((1, HEAD_DIM))],
        out_specs=pl.BlockSpec((1, 1, nr, HEAD_DIM), lambda s, i: (s, i, 0, 0)),
        out_shape=jax.ShapeDtypeStruct((2, B * G, nr, HEAD_DIM), F32),
        compiler_params=_cparams(("arbitrary", "arbitrary")),
        name="compress",
    )(kraw, w1l, w1, posflat, b1, w2, b2)


def _selection_matrix(n_rows, n_cmp, n_sel):
    cs = np.arange(n_rows)[:, None] * CMP_STRIDE
    ss = np.arange(n_sel)[None, :] * SEL_BLOCK
    ov = np.clip(np.minimum(cs + CMP_BLOCK, ss + SEL_BLOCK) - np.maximum(cs, ss), 0, None)
    ov = (ov / CMP_STRIDE).astype(np.float32)
    ov[n_cmp:] = 0.0
    return ov


SEL_TILE = 4 * Q_BLOCK
WIN_SPAN = WINDOW + Q_BLOCK


def _nsa_kernel(q_ref, gt_ref, kc_ref, vct_ref, ks_ref, vst_ref, kw_ref, vwt_ref, slope_ref, bias_ref,
                selmat_ref, emat_ref, gn_ref, o_ref, s_scr, *, n_cmp, n_top, win_span):
    qt = pl.program_id(1)
    q0 = qt * Q_BLOCK
    R = Q_PER_KV
    cols = R * Q_BLOCK
    n_rows_c = kc_ref.shape[2]
    n_sel = selmat_ref.shape[0]
    q_loc = lax.broadcasted_iota(jnp.int32, (1, cols), 1) % Q_BLOCK
    t_col = q0 + q_loc
    gates_t = jax.nn.sigmoid(gt_ref[...]).T

    def step(carry, qs_bf, slope, bias, k_bf, v_tiles, offset, ok):
        m, acc = carry
        s = jnp.where(ok, _dot_nt(k_bf, qs_bf) + bias, NEG)
        c = slope * (-offset).astype(F32)
        m_new = jnp.maximum(m, jnp.max(s, axis=0, keepdims=True) + c)
        p = jnp.exp(s + (c - m_new)).astype(BF16)
        acc = jnp.exp(m - m_new) * acc
        for i, v in enumerate(v_tiles):
            acc = acc + jnp.dot(v, p[i * Q_BLOCK:(i + 1) * Q_BLOCK, :], preferred_element_type=F32)
        return m_new, acc

    init = (jnp.full((1, cols), NEG, F32), jnp.zeros((V_AUG_ROWS, cols), F32))
    finish = lambda acc: acc[0:HEAD_DIM, :] / acc[HEAD_DIM:HEAD_DIM + 1, :]
    y_rows = []
    for g in range(N_KV):
        qg = q_ref[:, g * R * HEAD_DIM:(g + 1) * R * HEAD_DIM] * (HEAD_DIM ** -0.5)
        qs = jnp.concatenate([qg[:, r * HEAD_DIM:(r + 1) * HEAD_DIM] for r in range(R)], axis=0)
        qs_bf = qs.astype(BF16)
        slope = slope_ref[g]

        kc = kc_ref[0, g]
        c_idx = lax.broadcasted_iota(jnp.int32, (n_rows_c, 1), 0)
        cmp_end = c_idx * CMP_STRIDE + (CMP_BLOCK - 1)
        valid_c = (cmp_end <= t_col) & (c_idx < n_cmp)
        q_hi, q_lo = _split2(qs)
        k_hi, k_lo = _split2(kc)
        s = _dot_nt(k_hi, q_hi) + (_dot_nt(k_hi, q_lo) + _dot_nt(k_lo, q_hi))
        s = s - slope * jnp.abs(t_col - cmp_end).astype(F32)
        s = jnp.where(valid_c, s, NEG)
        e = jnp.exp(s - jnp.max(s, axis=0, keepdims=True))
        p_c = e / jnp.sum(e, axis=0, keepdims=True)
        p_c = p_c * (t_col >= CMP_BLOCK - 1).astype(F32)
        o_c = jnp.dot(vct_ref[0, g].astype(BF16), p_c.astype(BF16), preferred_element_type=F32)

        p_sum = p_c[:, 0:Q_BLOCK]
        for r in range(1, R):
            p_sum = p_sum + p_c[:, r * Q_BLOCK:(r + 1) * Q_BLOCK]
        p_hi, p_lo = _split2(p_sum)
        dot = lambda a, b: jnp.dot(a, b, preferred_element_type=F32)
        imp_t = dot(selmat_ref[...], p_hi) + dot(selmat_ref[...], p_lo)
        j_idx = lax.broadcasted_iota(jnp.int32, (n_sel, Q_BLOCK), 0)
        j_f = j_idx.astype(F32)
        cur = t_col[:, 0:Q_BLOCK] // SEL_BLOCK
        forced = (j_idx == 0) | (j_idx == cur) | (j_idx == cur - 1)
        work = jnp.where(j_idx <= cur, imp_t + jnp.where(forced, FORCE_BONUS, 0.0), NEG)
        msel_t = jnp.zeros((n_sel, Q_BLOCK), F32)
        for _ in range(n_top):
            mx = jnp.max(work, axis=0, keepdims=True)
            first = jnp.min(jnp.where(work == mx, j_f, float(LANES)), axis=0, keepdims=True)
            pick = j_f == first
            msel_t = jnp.where(pick, 1.0, msel_t)
            work = jnp.where(pick, REMOVED, work)
        if n_sel < LANES:
            msel_t = jnp.concatenate([msel_t, jnp.zeros((LANES - n_sel, Q_BLOCK), F32)], axis=0)
        msel_bf = msel_t.astype(BF16)

        sel_key = lax.broadcasted_iota(jnp.int32, (SEL_TILE, cols), 0)
        bias_sel = bias_ref[g, 0:SEL_TILE, :]

        def sel_scores(kt, causal):
            k0 = pl.multiple_of(kt * SEL_TILE, SEL_TILE)
            ch = jnp.dot(emat_ref[kt], msel_bf, preferred_element_type=F32)
            ok = jnp.concatenate([ch] * R, axis=1) > 0.5
            if causal:
                ok = ok & (sel_key <= q_loc + (q0 - k0))
            s = jnp.where(ok, _dot_nt(ks_ref[0, g, pl.ds(k0, SEL_TILE), :], qs_bf) + bias_sel, NEG)
            return s, jnp.max(s, axis=0, keepdims=True)

        def sel_consume(m, acc, s_max, kt):
            c = slope * (kt * SEL_TILE - q0).astype(F32)
            m_new = jnp.maximum(m, s_max + c)
            p = jnp.exp(s_scr[...] + (c - m_new)).astype(BF16)
            acc = jnp.exp(m - m_new) * acc
            for i in range(SEL_TILE // Q_BLOCK):
                acc = acc + jnp.dot(vst_ref[0, g, kt * (SEL_TILE // Q_BLOCK) + i],
                                    p[i * Q_BLOCK:(i + 1) * Q_BLOCK, :], preferred_element_type=F32)
            return m_new, acc

        kt_diag = q0 // SEL_TILE
        s_first, max_first = sel_scores(kt_diag, True)
        s_scr[...] = s_first

        def sel_body(kt, lc):
            m, acc, s_max, kt_parked = lc
            s_next, max_next = sel_scores(kt, False)
            m, acc = sel_consume(m, acc, s_max, kt_parked)
            s_scr[...] = s_next
            return m, acc, max_next, kt

        m_s, acc_s, max_last, kt_last = lax.fori_loop(0, kt_diag, sel_body, init + (max_first, kt_diag))
        _, acc_s = sel_consume(m_s, acc_s, max_last, kt_last)
        o_s = finish(acc_s)

        w0 = pl.multiple_of(jnp.maximum(q0 + Q_BLOCK - win_span, 0), Q_BLOCK)
        win_key = lax.broadcasted_iota(jnp.int32, (win_span, cols), 0)
        reach = q_loc + (q0 - w0)
        ok_w = (win_key <= reach) & (win_key > reach - WINDOW)
        v_tiles = [vwt_ref[0, g, w0 // Q_BLOCK + i] for i in range(win_span // Q_BLOCK)]
        carry = step(init, qs_bf, slope, bias_ref[g, 0:win_span, :], kw_ref[0, g, pl.ds(w0, win_span), :],
                     v_tiles, q0 - w0, ok_w)
        o_w = finish(carry[1])

        for r in range(R):
            col = (g * R + r) * N_BRANCHES
            sl = slice(r * Q_BLOCK, (r + 1) * Q_BLOCK)
            y_rows.append(gates_t[col:col + 1, :] * o_c[:, sl] + gates_t[col + 1:col + 2, :] * o_s[:, sl]
                          + gates_t[col + 2:col + 3, :] * o_w[:, sl])
    y_t = jnp.concatenate(y_rows, axis=0)
    y_t = y_t * lax.rsqrt(jnp.mean(y_t * y_t, axis=0, keepdims=True) + LN_EPS) * gn_ref[...]
    o_ref[...] = y_t.T


def _nsa(h_all, B, S, kc, vc, ks_bf, vs_tiles, kw_bf, vw_tiles, gn):
    nq = S // Q_BLOCK
    n_cmp = (S - CMP_BLOCK) // CMP_STRIDE + 1
    n_rows_c = kc.shape[2]
    n_sel = S // SEL_BLOCK
    n_top = min(SEL_TOPN, n_sel)
    n_span = S // SEL_TILE
    win_span = min(WIN_SPAN, S)
    n_bias = max(win_span, SEL_TILE)
    cols = Q_PER_KV * Q_BLOCK
    slopes = np.array([2.0 ** (-8.0 * (h + 1) / N_HEADS) for h in range(N_HEADS)], np.float32)
    slope_cols = np.repeat(slopes.reshape(N_KV, Q_PER_KV), Q_BLOCK, axis=1)[:, None, :]
    rel = (np.arange(cols)[None, :] % Q_BLOCK - np.arange(n_bias)[:, None]).astype(np.float32)
    bias = -slope_cols * rel[None]
    selmat_t = jnp.asarray(_selection_matrix(n_rows_c, n_cmp, n_sel).T, dtype=BF16)
    key_blk = (np.arange(S) // SEL_BLOCK).reshape(n_span, SEL_TILE, 1)
    emat_t = jnp.asarray((key_blk == np.arange(LANES)[None, None, :]).astype(np.float32), dtype=BF16)
    qw = Q_PER_KV * HEAD_DIM * N_KV
    per_batch = lambda *dims: pl.BlockSpec((1,) + dims, lambda b, i: (b,) + (0,) * len(dims))
    const = lambda shape: pl.BlockSpec(shape, lambda b, i: (0,) * len(shape))
    return pl.pallas_call(
        functools.partial(_nsa_kernel, n_cmp=n_cmp, n_top=n_top, win_span=win_span),
        grid=(B, nq),
        in_specs=[pl.BlockSpec((Q_BLOCK, qw), lambda b, i: (b * nq + i, COL_Q // qw)),
                  pl.BlockSpec((Q_BLOCK, LANES), lambda b, i: (b * nq + i, COL_GATES // LANES)),
                  per_batch(N_KV, n_rows_c, HEAD_DIM), per_batch(N_KV, HEAD_DIM, n_rows_c),
                  per_batch(N_KV, S, HEAD_DIM), per_batch(N_KV, nq, V_AUG_ROWS, Q_BLOCK),
                  per_batch(N_KV, S, HEAD_DIM), per_batch(N_KV, nq, V_AUG_ROWS, Q_BLOCK),
                  const((N_KV, 1, cols)), const((N_KV, n_bias, cols)), const((n_sel, n_rows_c)),
                  const((n_span, SEL_TILE, LANES)), const((NSA_WIDTH, 1))],
        out_specs=pl.BlockSpec((Q_BLOCK, NSA_WIDTH), lambda b, i: (b * nq + i, 0)),
        out_shape=jax.ShapeDtypeStruct((B * S, NSA_WIDTH), F32),
        scratch_shapes=[pltpu.VMEM((SEL_TILE, cols), F32)],
        compiler_params=_cparams(("arbitrary", "arbitrary")),
        name="nsa",
    )(h_all, h_all, kc, vc.transpose(0, 1, 3, 2), ks_bf, vs_tiles, kw_bf, vw_tiles,
      jnp.asarray(slope_cols), jnp.asarray(bias), selmat_t, emat_t, gn.reshape(NSA_WIDTH, 1))


def _out_ln_kernel(ylru_ref, ynsa_ref, x_ref, w_ref, g_ref, b_ref, o_ref, *, alpha):
    mix = jnp.dot(ylru_ref[...].astype(BF16), w_ref[0:LRU_WIDTH, :], preferred_element_type=F32)
    mix = mix + jnp.dot(ynsa_ref[...].astype(BF16), w_ref[LRU_WIDTH:, :], preferred_element_type=F32)
    o_ref[...] = _layer_norm_rows(alpha * x_ref[...] + mix, g_ref[...], b_ref[...])


def _out_ln(ylru, ynsa, xf, w_bf, g, b, alpha, tm=512):
    T = xf.shape[0]
    const = lambda shape: pl.BlockSpec(shape, lambda i: (0,) * len(shape))
    return pl.pallas_call(
        functools.partial(_out_ln_kernel, alpha=alpha),
        grid=(T // tm,),
        in_specs=[pl.BlockSpec((tm, LRU_WIDTH), lambda i: (i, 0)),
                  pl.BlockSpec((tm, NSA_WIDTH), lambda i: (i, 0)),
                  pl.BlockSpec((tm, D_MODEL), lambda i: (i, 0)),
                  const((D_MODEL, D_MODEL)), const((1, D_MODEL)), const((1, D_MODEL))],
        out_specs=pl.BlockSpec((tm, D_MODEL), lambda i: (i, 0)),
        out_shape=jax.ShapeDtypeStruct((T, D_MODEL), F32),
        compiler_params=_cparams(("arbitrary",)),
        name="out_ln",
    )(ylru, ynsa, xf, w_bf, g.reshape(1, D_MODEL), b.reshape(1, D_MODEL))


def _topk_rows(vals, order, payload, n_keep):
    n = vals.shape[1]
    kidx = lax.broadcasted_iota(jnp.int32, (n_keep, n), 0)
    never = float(1 << 20)
    kept_v = jnp.zeros((n_keep, n), F32)
    kept_p = jnp.zeros((n_keep, n), F32)
    work = vals
    for r in range(n_keep):
        mx = jnp.max(work, axis=0, keepdims=True)
        first = jnp.min(jnp.where(work == mx, order, never), axis=0, keepdims=True)
        pick = order == first
        kept_v = jnp.where(kidx == r, mx, kept_v)
        if payload is None:
            kept_p = jnp.where(kidx == r, first, kept_p)
        else:
            kept_p = jnp.where(kidx == r, jnp.max(jnp.where(pick, payload, -1.0), axis=0, keepdims=True), kept_p)
        work = jnp.where(pick, REMOVED, work)
    return kept_v, kept_p


def _candidate_rows():
    ij = [(i, 0) for i in range(PEER_TOPK)]
    for j in range(1, SUBLANES):
        ij += [(i, j) for i in range(SUBLANES)]
    ij += [(0, j) for j in range(SUBLANES, PEER_TOPK)]
    flat = np.array([i * PEER_TOPK + j for i, j in ij], np.float32)
    valid = np.array([(i + 1) * (j + 1) <= PEER_TOPK for i, j in ij])
    return flat[:, None], np.where(valid, 0.0, REMOVED).astype(np.float32)[:, None]


def _candidates(a0, a1):
    return ([(a0, a1[0:1, :])]
            + [(a0[0:SUBLANES, :], a1[j:j + 1, :]) for j in range(1, SUBLANES)]
            + [(a0[0:1, :], a1[SUBLANES:PEER_TOPK, :])])


def _peer_route_kernel(x_ref, wq_ref, sk_ref, flat_ref, pad_ref, e_ref, g_ref):
    tm = x_ref.shape[0]
    q = jnp.dot(x_ref[...].astype(BF16), wq_ref[...], preferred_element_type=F32)
    key_idx = lax.broadcasted_iota(jnp.int32, (PEER_KEYS, tm), 0).astype(F32)
    flat = jnp.broadcast_to(flat_ref[...], (flat_ref.shape[0], tm))
    experts, gates = [], []
    for h in range(PEER_HEADS):
        sv, si = [], []
        for c in range(2):
            col = (h * 2 + c) * PEER_KEY_DIM
            s_t = _dot_nt(sk_ref[c], q[:, col:col + PEER_KEY_DIM], HI)
            v, i = _topk_rows(s_t, key_idx, None, PEER_TOPK)
            sv.append(v)
            si.append(i)
        cand = jnp.concatenate([a + b for a, b in _candidates(sv[0], sv[1])], axis=0) + pad_ref[...]
        ecand = jnp.concatenate([a * float(PEER_KEYS) + b for a, b in _candidates(si[0], si[1])], axis=0)
        cv, ce = _topk_rows(cand, flat, ecand, PEER_TOPK)
        ex = jnp.exp(cv - cv[0:1, :])
        gates.append(ex / jnp.sum(ex, axis=0, keepdims=True))
        experts.append(ce)
    e_ref[...] = (jnp.concatenate(experts, axis=0) * float(ROWS_PER_EXPERT)).astype(jnp.int32)
    g_ref[...] = jnp.concatenate(gates, axis=0).T


def _peer_route(x1, wq_bf, subkeys, tm=256):
    T = x1.shape[0]
    tm = min(tm, T)
    nq = wq_bf.shape[1]
    flat, pad = _candidate_rows()
    n_cand = flat.shape[0]
    out = lambda: pl.BlockSpec((tm, N_PAIRS), lambda i: (i, 0))
    return pl.pallas_call(
        _peer_route_kernel,
        grid=(T // tm,),
        in_specs=[pl.BlockSpec((tm, D_MODEL), lambda i: (i, 0)),
                  pl.BlockSpec((D_MODEL, nq), lambda i: (0, 0)),
                  pl.BlockSpec((2, PEER_KEYS, PEER_KEY_DIM), lambda i: (0, 0, 0)),
                  pl.BlockSpec((n_cand, 1), lambda i: (0, 0)),
                  pl.BlockSpec((n_cand, 1), lambda i: (0, 0))],
        out_specs=[pl.BlockSpec((N_PAIRS, tm), lambda i: (0, i)), out()],
        out_shape=[jax.ShapeDtypeStruct((N_PAIRS, T), jnp.int32),
                   jax.ShapeDtypeStruct((T, N_PAIRS), F32)],
        compiler_params=_cparams(("arbitrary",)),
        name="peer_route",
    )(x1, wq_bf, subkeys, jnp.asarray(flat), jnp.asarray(pad))


ROWS_PER_EXPERT = D_ROWS // 2
TOKENS_PER_MATMUL = 32
PEER_TOKENS = 2 * LANES


class _ExpertIds:
    def __init__(self, e_hbm, e_smem_halves, sems, tb):
        self.e_hbm, self.e_smem, self.sems, self.tb, self.half = e_hbm, e_smem_halves, sems, tb, tb // 2
        self.step = pl.program_id(0)
        self.n_steps = pl.num_programs(0)

    def _copy(self, block, h):
        return pltpu.make_async_copy(self.e_hbm.at[:, pl.ds(block * self.tb + h * self.half, self.half)],
                                     self.e_smem[h], self.sems.at[h])

    def start_first_block(self):
        @pl.when(self.step == 0)
        def _():
            self._copy(0, 0).start()
            self._copy(0, 1).start()

    def wait_half(self, h):
        self._copy(self.step, h).wait()

    def prefetch_half(self, h):
        @pl.when(self.step + 1 < self.n_steps)
        def _():
            self._copy(self.step + 1, h).start()

    def __call__(self, h, k, t):
        return self.e_smem[h].at[k][t - h * self.half]


def _pack_table(tab):
    t = lax.bitcast_convert_type(tab.astype(BF16), jnp.uint16).astype(jnp.uint32)
    t = t.reshape(N_EXPERTS, ROWS_PER_EXPERT, 2, LANES)
    words = t[:, :, 0, :] | (t[:, :, 1, :] << 16)
    return lax.bitcast_convert_type(words, jnp.int32).reshape(N_EXPERTS * ROWS_PER_EXPERT, LANES)


def _expert_row(tab_ref, offset):
    words = tab_ref[pl.ds(pl.multiple_of(offset, ROWS_PER_EXPERT), ROWS_PER_EXPERT), :]
    return pltpu.bitcast(words, BF16)


def _sublane_sums(ps, roll, where, sub):
    lvl = ps
    for r in (1, 2, 4):
        m = (sub % (2 * r)) < r
        half = len(lvl) // 2
        lvl = [where(m, lvl[i], lvl[i + half]) + roll(where(m, lvl[i + half], lvl[i]), r) for i in range(half)]
    return lvl[0]


def _sum_src():
    sub = np.arange(SUBLANES)[:, None] * np.ones((1, LANES), np.int64)
    ps = [np.full((SUBLANES, LANES), 10.0 ** i) for i in range(SUBLANES)]
    out = _sublane_sums(ps, lambda v, r: np.roll(v, r, axis=0), np.where, sub)
    return [int(round(np.log10(out[s, 0] / SUBLANES))) for s in range(SUBLANES)]


_SUM_SRC = _sum_src()


def _peer_u_kernel(e_hbm, x_ref, tab_ref, gate_ref, w_ref, part_ref, zt_ref, e_smem_a, e_smem_b, sems, *, tb):
    sub = lax.broadcasted_iota(jnp.int32, (SUBLANES, LANES), 0)
    lane = lax.broadcasted_iota(jnp.int32, (N_PAIRS, tb), 1)
    ids = _ExpertIds(e_hbm, (e_smem_a, e_smem_b), sems, tb)
    ids.start_first_block()

    def gather_dots(h, t, slot):
        xt = x_ref[t]
        for grp in range(N_PAIRS // SUBLANES):
            prods = [None] * SUBLANES
            for s in range(SUBLANES):
                prods[_SUM_SRC[s]] = _expert_row(tab_ref, ids(h, grp * SUBLANES + s, t)).astype(F32) * xt
            part_ref[slot, grp * SUBLANES:(grp + 1) * SUBLANES, :] = _sublane_sums(
                prods, lambda v, r: pltpu.roll(v, r, axis=0), jnp.where, sub)

    n_slots = part_ref.shape[0]

    def lane_sums(t0):
        zt = zt_ref[...]
        for j in range(n_slots):
            zt = jnp.where(lane == t0 + j, jnp.sum(part_ref[j], axis=1, keepdims=True), zt)
        zt_ref[...] = zt

    part_ref[...] = jnp.zeros_like(part_ref)
    zt_ref[...] = jnp.zeros_like(zt_ref)

    def some_tokens(h, i, carry):
        lane_sums(n_slots * (i - 1))
        for j in range(n_slots):
            gather_dots(h, n_slots * i + j, j)
        return carry

    for h in range(2):
        ids.wait_half(h)
        lax.fori_loop(h * tb // (2 * n_slots), (h + 1) * tb // (2 * n_slots), functools.partial(some_tokens, h), 0)
        ids.prefetch_half(h)
    lane_sums(tb - n_slots)
    w_ref[...] = jax.nn.gelu(zt_ref[...].T) * gate_ref[...]


def _peer_u(experts_t, x3, tab, gates, tb=PEER_TOKENS):
    T = x3.shape[0]
    vm = lambda: pl.BlockSpec((tb, N_PAIRS), lambda i: (i, 0))
    return pl.pallas_call(
        functools.partial(_peer_u_kernel, tb=tb),
        grid=(T // tb,),
        in_specs=[pl.BlockSpec(memory_space=pl.ANY),
                  pl.BlockSpec((tb, D_ROWS, LANES), lambda i: (i, 0, 0)),
                  pl.BlockSpec((N_EXPERTS * ROWS_PER_EXPERT, LANES), lambda i: (0, 0), pipeline_mode=pl.Buffered(1)),
                  vm()],
        out_specs=vm(),
        out_shape=jax.ShapeDtypeStruct((T, N_PAIRS), F32),
        scratch_shapes=[pltpu.VMEM((4, N_PAIRS, LANES), F32), pltpu.VMEM((N_PAIRS, tb), F32),
                        pltpu.SMEM((N_PAIRS, tb // 2), jnp.int32), pltpu.SMEM((N_PAIRS, tb // 2), jnp.int32),
                        pltpu.SemaphoreType.DMA((2,))],
        compiler_params=_cparams(("arbitrary",)),
        name="peer_u",
    )(experts_t, x3, tab, gates)


def _expand_matrix():
    k = np.arange(N_PAIRS)[:, None]
    c = np.arange(N_PAIRS * D_ROWS)[None, :]
    return (c // D_ROWS == k).astype(np.float32)


def _peer_v_kernel(e_hbm, w_ref, x_ref, tab_ref, expand_ref, g_ref, b_ref, o_ref, y_ref, wide_ref,
                   e_smem_a, e_smem_b, sems, *, tb, alpha):
    n_col = N_PAIRS * D_ROWS
    ids = _ExpertIds(e_hbm, (e_smem_a, e_smem_b), sems, tb)
    ids.start_first_block()
    diag = (lax.broadcasted_iota(jnp.int32, (D_ROWS, n_col), 1) % D_ROWS
            == lax.broadcasted_iota(jnp.int32, (D_ROWS, n_col), 0))
    part_row = lax.broadcasted_iota(jnp.int32, (SUBLANES, LANES), 0)

    def expand_tokens(i, carry):
        t0 = pl.multiple_of(i * TOKENS_PER_MATMUL, TOKENS_PER_MATMUL)
        parts = []
        for j in range(TOKENS_PER_MATMUL):
            wrow = w_ref[pl.ds(t0 + j, 1), :]
            hi = wrow.astype(BF16).astype(F32)
            mid = (wrow - hi).astype(BF16).astype(F32)
            lo = ((wrow - hi) - mid).astype(BF16).astype(F32)
            bc = lambda v: jnp.broadcast_to(v, (SUBLANES, LANES))
            parts.append(jnp.where(part_row == 0, bc(hi), jnp.where(part_row == 1, bc(mid),
                                                                    jnp.where(part_row == 2, bc(lo), 0.0))))
        lhs = jnp.concatenate(parts, axis=0).astype(BF16)
        wide = jnp.dot(lhs, expand_ref[...], preferred_element_type=F32)
        wide_ref[pl.ds(t0, TOKENS_PER_MATMUL)] = wide.reshape(TOKENS_PER_MATMUL, SUBLANES, n_col)
        return carry

    lax.fori_loop(0, tb // TOKENS_PER_MATMUL, expand_tokens, 0)

    def left_operand(t):
        rows = [jnp.where(diag, jnp.broadcast_to(wide_ref[t, p:p + 1, :], (D_ROWS, n_col)), 0.0) for p in range(3)]
        return jnp.concatenate(rows, axis=0).astype(BF16)

    def weighted_sum(h, t, a):
        rhs = jnp.concatenate([_expert_row(tab_ref, ids(h, k, t)) for k in range(N_PAIRS)], axis=0)
        y3 = jnp.dot(a, rhs, preferred_element_type=F32)
        return (y3[0:D_ROWS] + y3[D_ROWS:2 * D_ROWS]) + y3[2 * D_ROWS:]

    tokens_per_step = 32

    def some_tokens(h, i, carry):
        ts = [tokens_per_step * i + j for j in range(tokens_per_step)]
        ys = [weighted_sum(h, t, left_operand(t)) for t in ts]
        for t, y in zip(ts, ys):
            y_ref[t] = y
        return carry

    for h in range(2):
        ids.wait_half(h)
        lax.fori_loop(h * tb // (2 * tokens_per_step), (h + 1) * tb // (2 * tokens_per_step),
                      functools.partial(some_tokens, h), 0)
        ids.prefetch_half(h)
    z = alpha * x_ref[...] + y_ref[...]
    inv_d = 1.0 / D_MODEL
    mu = jnp.sum(jnp.sum(z, axis=2, keepdims=True), axis=1, keepdims=True) * inv_d
    zc = z - mu
    var = jnp.sum(jnp.sum(zc * zc, axis=2, keepdims=True), axis=1, keepdims=True) * inv_d
    o_ref[...] = zc * lax.rsqrt(var + LN_EPS) * g_ref[...] + b_ref[...]


def _peer_v(experts_t, wgt, x3, tab, g, b, alpha, tb=PEER_TOKENS):
    T = x3.shape[0]
    return pl.pallas_call(
        functools.partial(_peer_v_kernel, tb=tb, alpha=alpha),
        grid=(T // tb,),
        in_specs=[pl.BlockSpec(memory_space=pl.ANY),
                  pl.BlockSpec((tb, N_PAIRS), lambda i: (i, 0)),
                  pl.BlockSpec((tb, D_ROWS, LANES), lambda i: (i, 0, 0)),
                  pl.BlockSpec((N_EXPERTS * ROWS_PER_EXPERT, LANES), lambda i: (0, 0), pipeline_mode=pl.Buffered(1)),
                  pl.BlockSpec((N_PAIRS, N_PAIRS * D_ROWS), lambda i: (0, 0)),
                  pl.BlockSpec((1, D_ROWS, LANES), lambda i: (0, 0, 0)),
                  pl.BlockSpec((1, D_ROWS, LANES), lambda i: (0, 0, 0))],
        out_specs=pl.BlockSpec((tb, D_ROWS, LANES), lambda i: (i, 0, 0)),
        out_shape=jax.ShapeDtypeStruct((T, D_ROWS, LANES), F32),
        scratch_shapes=[pltpu.VMEM((tb, D_ROWS, LANES), F32), pltpu.VMEM((tb, SUBLANES, N_PAIRS * D_ROWS), F32),
                        pltpu.SMEM((N_PAIRS, tb // 2), jnp.int32), pltpu.SMEM((N_PAIRS, tb // 2), jnp.int32),
                        pltpu.SemaphoreType.DMA((2,))],
        compiler_params=_cparams(("arbitrary",)),
        name="peer_v",
    )(experts_t, wgt, x3, tab, jnp.asarray(_expand_matrix(), dtype=BF16),
      g.reshape(1, D_ROWS, LANES), b.reshape(1, D_ROWS, LANES))


def kernel(x, w_in, b_in, conv_w, conv_b, lru_wa, lru_ba, lru_wx, lru_bx, lru_lambda, cmp_pos_k, cmpk_w1, cmpk_b1, cmpk_w2, cmpk_b2, cmp_pos_v, cmpv_w1, cmpv_b1, cmpv_w2, cmpv_b2, gn_lru_g, gn_nsa_g, w_out, ln1_g, ln1_b, peer_wq, peer_subkeys, peer_u, peer_v, ln2_g, ln2_b):
    B, S, D = x.shape
    T = B * S
    depth = w_in.shape[0]
    alpha = (2 * depth) ** 0.25
    xf = x.reshape(T, D)
    for l in range(depth):
        w_in_bf = jnp.pad(w_in[l], ((0, 0), (0, IN_COLS_PAD - IN_COLS))).astype(BF16)
        b_in_row = jnp.pad(b_in[l], (0, IN_COLS_PAD - IN_COLS)).reshape(1, IN_COLS_PAD)
        h_all, kraw, kbf, vtiles = _in_proj(xf, w_in_bf, b_in_row, B, S)

        y_lru = _rglru(h_all, B, S, conv_w[l], conv_b[l], _block_diag(lru_wa[l]), lru_ba[l],
                       _block_diag(lru_wx[l]), lru_bx[l], lru_lambda[l], gn_lru_g[l])

        w1 = jnp.stack([cmpk_w1[l], cmpv_w1[l]])
        w1_pos = w1.reshape(2, 2, CMP_STRIDE, HEAD_DIM, HEAD_DIM)
        w1l = jnp.concatenate([w1_pos[:, 0], w1_pos[:, 1]], axis=-1)
        pos = jnp.stack([cmp_pos_k[l], cmp_pos_v[l]]).reshape(2, 1, CMP_BLOCK * HEAD_DIM)
        posflat = jnp.broadcast_to(pos, (2, SUBLANES, CMP_BLOCK * HEAD_DIM))
        kcvc = _compress(kraw, w1l, w1, posflat,
                         jnp.stack([cmpk_b1[l], cmpv_b1[l]]).reshape(2, 1, HEAD_DIM),
                         jnp.stack([cmpk_w2[l], cmpv_w2[l]]),
                         jnp.stack([cmpk_b2[l], cmpv_b2[l]]).reshape(2, 1, HEAD_DIM))
        kcvc = kcvc.reshape(2, B, N_KV, S // CMP_STRIDE, HEAD_DIM)
        y_nsa = _nsa(h_all, B, S, kcvc[0], kcvc[1], kbf[0], vtiles[0], kbf[1], vtiles[1], gn_nsa_g[l])

        x1 = _out_ln(y_lru, y_nsa, xf, w_out[l].astype(BF16), ln1_g[l], ln1_b[l], alpha)

        experts_t, gates = _peer_route(x1, peer_wq[l].astype(BF16), peer_subkeys[l])
        x3 = x1.reshape(T, D_ROWS, LANES)
        wgt = _peer_u(experts_t, x3, _pack_table(peer_u[l]), gates)
        xf = _peer_v(experts_t, wgt, x3, _pack_table(peer_v[l]), ln2_g[l], ln2_b[l], alpha).reshape(T, D)
    return xf.reshape(B, S, D)
```

```python
import functools

import numpy as np
import jax
import jax.numpy as jnp
from jax import lax
from jax.experimental import pallas as pl
from jax.experimental.pallas import tpu as pltpu

D_MODEL = 1024
LRU_WIDTH = 512
LRU_BLOCKS = 8
LRU_BLOCK_DIM = LRU_WIDTH // LRU_BLOCKS
CONV_WIDTH = 4
LRU_C = 8.0
N_HEADS = 8
HEAD_DIM = 64
N_KV = 2
Q_PER_KV = N_HEADS // N_KV
NSA_WIDTH = N_HEADS * HEAD_DIM
KV_WIDTH = N_KV * HEAD_DIM
N_BRANCHES = 3
CMP_BLOCK = 32
CMP_STRIDE = 16
SEL_BLOCK = 64
SEL_TOPN = 16
WINDOW = 512
Q_BLOCK = 128
FORCE_BONUS = 1e4
NEG = -1e30
REMOVED = -3.0e38
PEER_HEADS = 8
PEER_KEYS = 128
PEER_TOPK = 16
PEER_KEY_DIM = 128
N_EXPERTS = PEER_KEYS * PEER_KEYS
N_PAIRS = PEER_HEADS * PEER_TOPK
LN_EPS = 1e-5
IN_SPLITS = (LRU_WIDTH, LRU_WIDTH, NSA_WIDTH) + (KV_WIDTH,) * 6 + (N_HEADS * N_BRANCHES,)
IN_COLS = sum(IN_SPLITS)

LANES = 128
SUBLANES = 8
VMEM_LIMIT_BYTES = 56 * 1024 * 1024

IN_COLS_PAD = -(-IN_COLS // LANES) * LANES
COL_LRU_X = 0
COL_LRU_GATE = LRU_WIDTH
COL_Q = 2 * LRU_WIDTH
COL_KV = COL_Q + NSA_WIDTH
COL_GATES = COL_KV + 6 * KV_WIDTH

HI = lax.Precision.HIGHEST
F32 = jnp.float32
BF16 = jnp.bfloat16

D_ROWS = D_MODEL // LANES


def _cparams(sem):
    return pltpu.CompilerParams(dimension_semantics=sem, vmem_limit_bytes=VMEM_LIMIT_BYTES)


def _dot_nt(a, b, precision=None):
    return lax.dot_general(a, b, (((1,), (1,)), ((), ())), precision=precision,
                           preferred_element_type=F32)


def _split2(a):
    hi = a.astype(BF16)
    return hi, (a - hi.astype(F32)).astype(BF16)


def _dot_split(a, b):
    a_hi, a_lo = _split2(a)
    b_hi, b_lo = _split2(b)
    dot = lambda u, v: jnp.dot(u, v, preferred_element_type=F32)
    return dot(a_hi, b_hi) + (dot(a_hi, b_lo) + dot(a_lo, b_hi))


def _layer_norm_rows(z, g, b):
    mu = jnp.mean(z, axis=-1, keepdims=True)
    zc = z - mu
    var = jnp.mean(zc * zc, axis=-1, keepdims=True)
    return zc * lax.rsqrt(var + LN_EPS) * g + b


V_AUG_ROWS = HEAD_DIM + 16


def _in_proj_kernel(x_ref, w_ref, b_ref, o_ref, kraw_ref, kbf_ref, vt_ref):
    h = jnp.dot(x_ref[...].astype(BF16), w_ref[...], preferred_element_type=F32) + b_ref[...]
    o_ref[...] = h
    group_cols = lambda col, g: slice(col + g * HEAD_DIM, col + (g + 1) * HEAD_DIM)
    for j in range(2):
        for g in range(N_KV):
            kraw_ref[j, 0, g] = h[:, group_cols(COL_KV + j * KV_WIDTH, g)]
            kbf_ref[j, 0, g] = h[:, group_cols(COL_KV + (2 + 2 * j) * KV_WIDTH, g)].astype(BF16)
    ones_row = (lax.broadcasted_iota(jnp.int32, (V_AUG_ROWS - HEAD_DIM, Q_BLOCK), 0) == 0).astype(F32)
    for j in range(2):
        col = COL_KV + (3 + 2 * j) * KV_WIDTH
        for qb in range(h.shape[0] // Q_BLOCK):
            v_t = h[qb * Q_BLOCK:(qb + 1) * Q_BLOCK, col:col + KV_WIDTH].T
            for g in range(N_KV):
                tile = jnp.concatenate([v_t[g * HEAD_DIM:(g + 1) * HEAD_DIM, :], ones_row], axis=0)
                vt_ref[j, 0, g, qb] = tile.astype(BF16)


def _in_proj(xf, w_bf, b_row, B, S, tm=512):
    T = xf.shape[0]
    tm = min(tm, S)
    nt = S // tm
    per_tile = lambda *dims: pl.BlockSpec((2, 1, N_KV) + dims, lambda i: (0, i // nt, 0, i % nt) + (0,) * (len(dims) - 1))
    return pl.pallas_call(
        _in_proj_kernel,
        grid=(T // tm,),
        in_specs=[pl.BlockSpec((tm, D_MODEL), lambda i: (i, 0)),
                  pl.BlockSpec((D_MODEL, IN_COLS_PAD), lambda i: (0, 0)),
                  pl.BlockSpec((1, IN_COLS_PAD), lambda i: (0, 0))],
        out_specs=[pl.BlockSpec((tm, IN_COLS_PAD), lambda i: (i, 0)),
                   per_tile(tm, HEAD_DIM), per_tile(tm, HEAD_DIM),
                   per_tile(tm // Q_BLOCK, V_AUG_ROWS, Q_BLOCK)],
        out_shape=[jax.ShapeDtypeStruct((T, IN_COLS_PAD), F32),
                   jax.ShapeDtypeStruct((2, B, N_KV, S, HEAD_DIM), F32),
                   jax.ShapeDtypeStruct((2, B, N_KV, S, HEAD_DIM), BF16),
                   jax.ShapeDtypeStruct((2, B, N_KV, S // Q_BLOCK, V_AUG_ROWS, Q_BLOCK), BF16)],
        compiler_params=_cparams(("arbitrary",)),
        name="in_proj",
    )(xf, w_bf, b_row)


def _rglru_kernel(x_ref, gate_ref, cw_ref, cb_ref, wa_ref, ba_ref, wx_ref, bx_ref, lam_ref, gn_ref,
                  o_ref, xprev_ref, h_ref, *, ts):
    i = pl.program_id(1)

    @pl.when(i == 0)
    def _():
        xprev_ref[...] = jnp.zeros_like(xprev_ref)
        h_ref[...] = jnp.zeros_like(h_ref)

    xb = x_ref[...]
    xp = xprev_ref[...]
    row = lax.broadcasted_iota(jnp.int32, xb.shape, 0)
    xc = cw_ref[CONV_WIDTH - 1:CONV_WIDTH, :] * xb + cb_ref[...]
    for j in range(1, CONV_WIDTH):
        shifted = jnp.where(row >= j, pltpu.roll(xb, j, axis=0), pltpu.roll(xp, j, axis=0))
        xc = xc + cw_ref[CONV_WIDTH - 1 - j:CONV_WIDTH - j, :] * shifted
    xprev_ref[...] = xb

    r = jax.nn.sigmoid(_dot_split(xc, wa_ref[...]) + ba_ref[...])
    ig = jax.nn.sigmoid(_dot_split(xc, wx_ref[...]) + bx_ref[...])
    lam = lam_ref[...]
    softplus_neg_lam = jnp.maximum(-lam, 0.0) + jnp.log1p(jnp.exp(-jnp.abs(lam)))
    log_a = -LRU_C * r * softplus_neg_lam
    a = jnp.exp(log_a)
    th = jnp.tanh(log_a)
    u = jnp.sqrt(-2.0 * th / (1.0 - th)) * (ig * xc)

    d = 1
    while d < ts:
        keep = row >= d
        a_sh = pltpu.roll(a, d, axis=0)
        u_sh = pltpu.roll(u, d, axis=0)
        u = jnp.where(keep, a * u_sh + u, u)
        a = jnp.where(keep, a * a_sh, a)
        d *= 2
    h = u + a * h_ref[0:1, :]
    h_ref[0:1, :] = h[ts - 1:ts, :]

    y = h * jax.nn.gelu(gate_ref[...])
    y = y * lax.rsqrt(jnp.mean(y * y, axis=-1, keepdims=True) + LN_EPS) * gn_ref[...]
    o_ref[...] = y


def _rglru(h_all, B, S, cw, cb, wa_bd, ba, wx_bd, bx, lam, gn, ts=512):
    ts = min(ts, S)
    nt = S // ts
    C = LRU_WIDTH
    row = lambda v: v.reshape(1, C)
    const = lambda shape: pl.BlockSpec(shape, lambda b, i: (0,) * len(shape))
    return pl.pallas_call(
        functools.partial(_rglru_kernel, ts=ts),
        grid=(B, nt),
        in_specs=[pl.BlockSpec((ts, C), lambda b, i: (b * nt + i, COL_LRU_X // C)),
                  pl.BlockSpec((ts, C), lambda b, i: (b * nt + i, COL_LRU_GATE // C)),
                  const((CONV_WIDTH, C)), const((1, C)), const((C, C)), const((1, C)),
                  const((C, C)), const((1, C)), const((1, C)), const((1, C))],
        out_specs=pl.BlockSpec((ts, C), lambda b, i: (b * nt + i, 0)),
        out_shape=jax.ShapeDtypeStruct((B * S, C), F32),
        scratch_shapes=[pltpu.VMEM((ts, C), F32), pltpu.VMEM((SUBLANES, C), F32)],
        compiler_params=_cparams(("arbitrary", "arbitrary")),
        name="rglru",
    )(h_all, h_all, cw, row(cb), wa_bd, row(ba), wx_bd, row(bx), row(lam), row(gn))


def _block_diag(w):
    n, d, e = w.shape
    eye = jnp.eye(n, dtype=w.dtype)
    return (w[:, :, None, :] * eye[:, None, :, None]).reshape(n * d, n * e)


def _compress_kernel(k_ref, w1l_ref, w1_ref, pos_ref, b1_ref, w2_ref, b2_ref, o_ref):
    nr = o_ref.shape[2]
    ab = jnp.zeros((nr, 2 * HEAD_DIM), F32)
    for l in range(CMP_STRIDE):
        k_l = k_ref[0, 0, 0, pl.ds(l, nr, stride=CMP_STRIDE), :]
        ab = ab + jnp.dot(k_l, w1l_ref[0, l], precision=HI, preferred_element_type=F32)
    ab_next = pltpu.roll(ab, nr - 1, axis=0)
    pre = ab[:, :HEAD_DIM] + ab_next[:, HEAD_DIM:]
    posc = jnp.dot(pos_ref[0], w1_ref[0], precision=HI, preferred_element_type=F32)[0:1, :]
    h1 = jax.nn.gelu(pre + posc + b1_ref[0])
    o_ref[0, 0] = jnp.dot(h1, w2_ref[0], precision=HI, preferred_element_type=F32) + b2_ref[0]


def _compress(kraw, w1l, w1, posflat, b1, w2, b2):
    _, B, G, S, _ = kraw.shape
    nr = S // CMP_STRIDE
    fan = CMP_BLOCK * HEAD_DIM
    wspec = lambda shape: pl.BlockSpec((1,) + shape, lambda s, i: (s,) + (0,) * len(shape))
    return pl.pallas_call(
        _compress_kernel,
        grid=(2, B * G),
        in_specs=[pl.BlockSpec((1, 1, 1, S, HEAD_DIM), lambda s, i: (s, i // G, i % G, 0, 0)),
                  wspec((CMP_STRIDE, HEAD_DIM, 2 * HEAD_DIM)), wspec((fan, HEAD_DIM)), wspec((SUBLANES, fan)),
                  wspec((1, HEAD_DIM)), wspec((HEAD_DIM, HEAD_DIM)), wspec((1, HEAD_DIM))],
        out_specs=pl.BlockSpec((1, 1, nr, HEAD_DIM), lambda s, i: (s, i, 0, 0)),
        out_shape=jax.ShapeDtypeStruct((2, B * G, nr, HEAD_DIM), F32),
        compiler_params=_cparams(("arbitrary", "arbitrary")),
        name="compress",
    )(kraw, w1l, w1, posflat, b1, w2, b2)


def _selection_matrix(n_rows, n_cmp, n_sel):
    cs = np.arange(n_rows)[:, None] * CMP_STRIDE
    ss = np.arange(n_sel)[None, :] * SEL_BLOCK
    ov = np.clip(np.minimum(cs + CMP_BLOCK, ss + SEL_BLOCK) - np.maximum(cs, ss), 0, None)
    ov = (ov / CMP_STRIDE).astype(np.float32)
    ov[n_cmp:] = 0.0
    return ov


SEL_TILE = 4 * Q_BLOCK
WIN_SPAN = WINDOW + Q_BLOCK


def _nsa_kernel(q_ref, gt_ref, kc_ref, vct_ref, ks_ref, vst_ref, kw_ref, vwt_ref, slope_ref, bias_ref,
                selmat_ref, emat_ref, gn_ref, o_ref, s_scr, *, n_cmp, n_top, win_span):
    qt = pl.program_id(1)
    q0 = qt * Q_BLOCK
    R = Q_PER_KV
    cols = R * Q_BLOCK
    n_rows_c = kc_ref.shape[2]
    n_sel = selmat_ref.shape[0]
    q_loc = lax.broadcasted_iota(jnp.int32, (1, cols), 1) % Q_BLOCK
    t_col = q0 + q_loc
    gates_t = jax.nn.sigmoid(gt_ref[...]).T

    def step(carry, qs_bf, slope, bias, k_bf, v_tiles, offset, ok):
        m, acc = carry
        s = jnp.where(ok, _dot_nt(k_bf, qs_bf) + bias, NEG)
        c = slope * (-offset).astype(F32)
        m_new = jnp.maximum(m, jnp.max(s, axis=0, keepdims=True) + c)
        p = jnp.exp(s + (c - m_new)).astype(BF16)
        acc = jnp.exp(m - m_new) * acc
        for i, v in enumerate(v_tiles):
            acc = acc + jnp.dot(v, p[i * Q_BLOCK:(i + 1) * Q_BLOCK, :], preferred_element_type=F32)
        return m_new, acc

    init = (jnp.full((1, cols), NEG, F32), jnp.zeros((V_AUG_ROWS, cols), F32))
    finish = lambda acc: acc[0:HEAD_DIM, :] / acc[HEAD_DIM:HEAD_DIM + 1, :]
    y_rows = []
    for g in range(N_KV):
        qg = q_ref[:, g * R * HEAD_DIM:(g + 1) * R * HEAD_DIM] * (HEAD_DIM ** -0.5)
        qs = jnp.concatenate([qg[:, r * HEAD_DIM:(r + 1) * HEAD_DIM] for r in range(R)], axis=0)
        qs_bf = qs.astype(BF16)
        slope = slope_ref[g]

        kc = kc_ref[0, g]
        c_idx = lax.broadcasted_iota(jnp.int32, (n_rows_c, 1), 0)
        cmp_end = c_idx * CMP_STRIDE + (CMP_BLOCK - 1)
        valid_c = (cmp_end <= t_col) & (c_idx < n_cmp)
        q_hi, q_lo = _split2(qs)
        k_hi, k_lo = _split2(kc)
        s = _dot_nt(k_hi, q_hi) + (_dot_nt(k_hi, q_lo) + _dot_nt(k_lo, q_hi))
        s = s - slope * jnp.abs(t_col - cmp_end).astype(F32)
        s = jnp.where(valid_c, s, NEG)
        e = jnp.exp(s - jnp.max(s, axis=0, keepdims=True))
        p_c = e / jnp.sum(e, axis=0, keepdims=True)
        p_c = p_c * (t_col >= CMP_BLOCK - 1).astype(F32)
        o_c = jnp.dot(vct_ref[0, g].astype(BF16), p_c.astype(BF16), preferred_element_type=F32)

        p_sum = p_c[:, 0:Q_BLOCK]
        for r in range(1, R):
            p_sum = p_sum + p_c[:, r * Q_BLOCK:(r + 1) * Q_BLOCK]
        p_hi, p_lo = _split2(p_sum)
        dot = lambda a, b: jnp.dot(a, b, preferred_element_type=F32)
        imp_t = dot(selmat_ref[...], p_hi) + dot(selmat_ref[...], p_lo)
        j_idx = lax.broadcasted_iota(jnp.int32, (n_sel, Q_BLOCK), 0)
        j_f = j_idx.astype(F32)
        cur = t_col[:, 0:Q_BLOCK] // SEL_BLOCK
        forced = (j_idx == 0) | (j_idx == cur) | (j_idx == cur - 1)
        work = jnp.where(j_idx <= cur, imp_t + jnp.where(forced, FORCE_BONUS, 0.0), NEG)
        msel_t = jnp.zeros((n_sel, Q_BLOCK), F32)
        for _ in range(n_top):
            mx = jnp.max(work, axis=0, keepdims=True)
            first = jnp.min(jnp.where(work == mx, j_f, float(LANES)), axis=0, keepdims=True)
            pick = j_f == first
            msel_t = jnp.where(pick, 1.0, msel_t)
            work = jnp.where(pick, REMOVED, work)
        if n_sel < LANES:
            msel_t = jnp.concatenate([msel_t, jnp.zeros((LANES - n_sel, Q_BLOCK), F32)], axis=0)
        msel_bf = msel_t.astype(BF16)

        sel_key = lax.broadcasted_iota(jnp.int32, (SEL_TILE, cols), 0)
        bias_sel = bias_ref[g, 0:SEL_TILE, :]

        def sel_scores(kt, causal):
            k0 = pl.multiple_of(kt * SEL_TILE, SEL_TILE)
            ch = jnp.dot(emat_ref[kt], msel_bf, preferred_element_type=F32)
            ok = jnp.concatenate([ch] * R, axis=1) > 0.5
            if causal:
                ok = ok & (sel_key <= q_loc + (q0 - k0))
            s = jnp.where(ok, _dot_nt(ks_ref[0, g, pl.ds(k0, SEL_TILE), :], qs_bf) + bias_sel, NEG)
            return s, jnp.max(s, axis=0, keepdims=True)

        def sel_consume(m, acc, s_max, kt):
            c = slope * (kt * SEL_TILE - q0).astype(F32)
            m_new = jnp.maximum(m, s_max + c)
            p = jnp.exp(s_scr[...] + (c - m_new)).astype(BF16)
            acc = jnp.exp(m - m_new) * acc
            for i in range(SEL_TILE // Q_BLOCK):
                acc = acc + jnp.dot(vst_ref[0, g, kt * (SEL_TILE // Q_BLOCK) + i],
                                    p[i * Q_BLOCK:(i + 1) * Q_BLOCK, :], preferred_element_type=F32)
            return m_new, acc

        kt_diag = q0 // SEL_TILE
        s_first, max_first = sel_scores(kt_diag, True)
        s_scr[...] = s_first

        def sel_body(kt, lc):
            m, acc, s_max, kt_parked = lc
            s_next, max_next = sel_scores(kt, False)
            m, acc = sel_consume(m, acc, s_max, kt_parked)
            s_scr[...] = s_next
            return m, acc, max_next, kt

        m_s, acc_s, max_last, kt_last = lax.fori_loop(0, kt_diag, sel_body, init + (max_first, kt_diag))
        _, acc_s = sel_consume(m_s, acc_s, max_last, kt_last)
        o_s = finish(acc_s)

        w0 = pl.multiple_of(jnp.maximum(q0 + Q_BLOCK - win_span, 0), Q_BLOCK)
        win_key = lax.broadcasted_iota(jnp.int32, (win_span, cols), 0)
        reach = q_loc + (q0 - w0)
        ok_w = (win_key <= reach) & (win_key > reach - WINDOW)
        v_tiles = [vwt_ref[0, g, w0 // Q_BLOCK + i] for i in range(win_span // Q_BLOCK)]
        carry = step(init, qs_bf, slope, bias_ref[g, 0:win_span, :], kw_ref[0, g, pl.ds(w0, win_span), :],
                     v_tiles, q0 - w0, ok_w)
        o_w = finish(carry[1])

        for r in range(R):
            col = (g * R + r) * N_BRANCHES
            sl = slice(r * Q_BLOCK, (r + 1) * Q_BLOCK)
            y_rows.append(gates_t[col:col + 1, :] * o_c[:, sl] + gates_t[col + 1:col + 2, :] * o_s[:, sl]
                          + gates_t[col + 2:col + 3, :] * o_w[:, sl])
    y_t = jnp.concatenate(y_rows, axis=0)
    y_t = y_t * lax.rsqrt(jnp.mean(y_t * y_t, axis=0, keepdims=True) + LN_EPS) * gn_ref[...]
    o_ref[...] = y_t.T


def _nsa(h_all, B, S, kc, vc, ks_bf, vs_tiles, kw_bf, vw_tiles, gn):
    nq = S // Q_BLOCK
    n_cmp = (S - CMP_BLOCK) // CMP_STRIDE + 1
    n_rows_c = kc.shape[2]
    n_sel = S // SEL_BLOCK
    n_top = min(SEL_TOPN, n_sel)
    n_span = S // SEL_TILE
    win_span = min(WIN_SPAN, S)
    n_bias = max(win_span, SEL_TILE)
    cols = Q_PER_KV * Q_BLOCK
    slopes = np.array([2.0 ** (-8.0 * (h + 1) / N_HEADS) for h in range(N_HEADS)], np.float32)
    slope_cols = np.repeat(slopes.reshape(N_KV, Q_PER_KV), Q_BLOCK, axis=1)[:, None, :]
    rel = (np.arange(cols)[None, :] % Q_BLOCK - np.arange(n_bias)[:, None]).astype(np.float32)
    bias = -slope_cols * rel[None]
    selmat_t = jnp.asarray(_selection_matrix(n_rows_c, n_cmp, n_sel).T, dtype=BF16)
    key_blk = (np.arange(S) // SEL_BLOCK).reshape(n_span, SEL_TILE, 1)
    emat_t = jnp.asarray((key_blk == np.arange(LANES)[None, None, :]).astype(np.float32), dtype=BF16)
    qw = Q_PER_KV * HEAD_DIM * N_KV
    per_batch = lambda *dims: pl.BlockSpec((1,) + dims, lambda b, i: (b,) + (0,) * len(dims))
    const = lambda shape: pl.BlockSpec(shape, lambda b, i: (0,) * len(shape))
    return pl.pallas_call(
        functools.partial(_nsa_kernel, n_cmp=n_cmp, n_top=n_top, win_span=win_span),
        grid=(B, nq),
        in_specs=[pl.BlockSpec((Q_BLOCK, qw), lambda b, i: (b * nq + i, COL_Q // qw)),
                  pl.BlockSpec((Q_BLOCK, LANES), lambda b, i: (b * nq + i, COL_GATES // LANES)),
                  per_batch(N_KV, n_rows_c, HEAD_DIM), per_batch(N_KV, HEAD_DIM, n_rows_c),
                  per_batch(N_KV, S, HEAD_DIM), per_batch(N_KV, nq, V_AUG_ROWS, Q_BLOCK),
                  per_batch(N_KV, S, HEAD_DIM), per_batch(N_KV, nq, V_AUG_ROWS, Q_BLOCK),
                  const((N_KV, 1, cols)), const((N_KV, n_bias, cols)), const((n_sel, n_rows_c)),
                  const((n_span, SEL_TILE, LANES)), const((NSA_WIDTH, 1))],
        out_specs=pl.BlockSpec((Q_BLOCK, NSA_WIDTH), lambda b, i: (b * nq + i, 0)),
        out_shape=jax.ShapeDtypeStruct((B * S, NSA_WIDTH), F32),
        scratch_shapes=[pltpu.VMEM((SEL_TILE, cols), F32)],
        compiler_params=_cparams(("arbitrary", "arbitrary")),
        name="nsa",
    )(h_all, h_all, kc, vc.transpose(0, 1, 3, 2), ks_bf, vs_tiles, kw_bf, vw_tiles,
      jnp.asarray(slope_cols), jnp.asarray(bias), selmat_t, emat_t, gn.reshape(NSA_WIDTH, 1))


def _out_ln_kernel(ylru_ref, ynsa_ref, x_ref, w_ref, g_ref, b_ref, o_ref, *, alpha):
    mix = jnp.dot(ylru_ref[...].astype(BF16), w_ref[0:LRU_WIDTH, :], preferred_element_type=F32)
    mix = mix + jnp.dot(ynsa_ref[...].astype(BF16), w_ref[LRU_WIDTH:, :], preferred_element_type=F32)
    o_ref[...] = _layer_norm_rows(alpha * x_ref[...] + mix, g_ref[...], b_ref[...])


def _out_ln(ylru, ynsa, xf, w_bf, g, b, alpha, tm=512):
    T = xf.shape[0]
    const = lambda shape: pl.BlockSpec(shape, lambda i: (0,) * len(shape))
    return pl.pallas_call(
        functools.partial(_out_ln_kernel, alpha=alpha),
        grid=(T // tm,),
        in_specs=[pl.BlockSpec((tm, LRU_WIDTH), lambda i: (i, 0)),
                  pl.BlockSpec((tm, NSA_WIDTH), lambda i: (i, 0)),
                  pl.BlockSpec((tm, D_MODEL), lambda i: (i, 0)),
                  const((D_MODEL, D_MODEL)), const((1, D_MODEL)), const((1, D_MODEL))],
        out_specs=pl.BlockSpec((tm, D_MODEL), lambda i: (i, 0)),
        out_shape=jax.ShapeDtypeStruct((T, D_MODEL), F32),
        compiler_params=_cparams(("arbitrary",)),
        name="out_ln",
    )(ylru, ynsa, xf, w_bf, g.reshape(1, D_MODEL), b.reshape(1, D_MODEL))


def _topk_rows(vals, order, payload, n_keep):
    n = vals.shape[1]
    kidx = lax.broadcasted_iota(jnp.int32, (n_keep, n), 0)
    never = float(1 << 20)
    kept_v = jnp.zeros((n_keep, n), F32)
    kept_p = jnp.zeros((n_keep, n), F32)
    work = vals
    for r in range(n_keep):
        mx = jnp.max(work, axis=0, keepdims=True)
        first = jnp.min(jnp.where(work == mx, order, never), axis=0, keepdims=True)
        pick = order == first
        kept_v = jnp.where(kidx == r, mx, kept_v)
        if payload is None:
            kept_p = jnp.where(kidx == r, first, kept_p)
        else:
            kept_p = jnp.where(kidx == r, jnp.max(jnp.where(pick, payload, -1.0), axis=0, keepdims=True), kept_p)
        work = jnp.where(pick, REMOVED, work)
    return kept_v, kept_p


def _candidate_rows():
    ij = [(i, 0) for i in range(PEER_TOPK)]
    for j in range(1, SUBLANES):
        ij += [(i, j) for i in range(SUBLANES)]
    ij += [(0, j) for j in range(SUBLANES, PEER_TOPK)]
    flat = np.array([i * PEER_TOPK + j for i, j in ij], np.float32)
    valid = np.array([(i + 1) * (j + 1) <= PEER_TOPK for i, j in ij])
    return flat[:, None], np.where(valid, 0.0, REMOVED).astype(np.float32)[:, None]


def _candidates(a0, a1):
    return ([(a0, a1[0:1, :])]
            + [(a0[0:SUBLANES, :], a1[j:j + 1, :]) for j in range(1, SUBLANES)]
            + [(a0[0:1, :], a1[SUBLANES:PEER_TOPK, :])])


def _peer_route_kernel(x_ref, wq_ref, sk_ref, flat_ref, pad_ref, e_ref, g_ref):
    tm = x_ref.shape[0]
    q = jnp.dot(x_ref[...].astype(BF16), wq_ref[...], preferred_element_type=F32)
    key_idx = lax.broadcasted_iota(jnp.int32, (PEER_KEYS, tm), 0).astype(F32)
    flat = jnp.broadcast_to(flat_ref[...], (flat_ref.shape[0], tm))
    experts, gates = [], []
    for h in range(PEER_HEADS):
        sv, si = [], []
        for c in range(2):
            col = (h * 2 + c) * PEER_KEY_DIM
            s_t = _dot_nt(sk_ref[c], q[:, col:col + PEER_KEY_DIM], HI)
            v, i = _topk_rows(s_t, key_idx, None, PEER_TOPK)
            sv.append(v)
            si.append(i)
        cand = jnp.concatenate([a + b for a, b in _candidates(sv[0], sv[1])], axis=0) + pad_ref[...]
        ecand = jnp.concatenate([a * float(PEER_KEYS) + b for a, b in _candidates(si[0], si[1])], axis=0)
        cv, ce = _topk_rows(cand, flat, ecand, PEER_TOPK)
        ex = jnp.exp(cv - cv[0:1, :])
        gates.append(ex / jnp.sum(ex, axis=0, keepdims=True))
        experts.append(ce)
    e_ref[...] = (jnp.concatenate(experts, axis=0) * float(ROWS_PER_EXPERT)).astype(jnp.int32)
    g_ref[...] = jnp.concatenate(gates, axis=0).T


def _peer_route(x1, wq_bf, subkeys, tm=256):
    T = x1.shape[0]
    tm = min(tm, T)
    nq = wq_bf.shape[1]
    flat, pad = _candidate_rows()
    n_cand = flat.shape[0]
    out = lambda: pl.BlockSpec((tm, N_PAIRS), lambda i: (i, 0))
    return pl.pallas_call(
        _peer_route_kernel,
        grid=(T // tm,),
        in_specs=[pl.BlockSpec((tm, D_MODEL), lambda i: (i, 0)),
                  pl.BlockSpec((D_MODEL, nq), lambda i: (0, 0)),
                  pl.BlockSpec((2, PEER_KEYS, PEER_KEY_DIM), lambda i: (0, 0, 0)),
                  pl.BlockSpec((n_cand, 1), lambda i: (0, 0)),
                  pl.BlockSpec((n_cand, 1), lambda i: (0, 0))],
        out_specs=[pl.BlockSpec((N_PAIRS, tm), lambda i: (0, i)), out()],
        out_shape=[jax.ShapeDtypeStruct((N_PAIRS, T), jnp.int32),
                   jax.ShapeDtypeStruct((T, N_PAIRS), F32)],
        compiler_params=_cparams(("arbitrary",)),
        name="peer_route",
    )(x1, wq_bf, subkeys, jnp.asarray(flat), jnp.asarray(pad))


ROWS_PER_EXPERT = D_ROWS // 2
TOKENS_PER_MATMUL = 32
PEER_TOKENS = 2 * LANES


class _ExpertIds:
    def __init__(self, e_hbm, e_smem_halves, sems, tb):
        self.e_hbm, self.e_smem, self.sems, self.tb, self.half = e_hbm, e_smem_halves, sems, tb, tb // 2
        self.step = pl.program_id(0)
        self.n_steps = pl.num_programs(0)

    def _copy(self, block, h):
        return pltpu.make_async_copy(self.e_hbm.at[:, pl.ds(block * self.tb + h * self.half, self.half)],
                                     self.e_smem[h], self.sems.at[h])

    def start_first_block(self):
        @pl.when(self.step == 0)
        def _():
            self._copy(0, 0).start()
            self._copy(0, 1).start()

    def wait_half(self, h):
        self._copy(self.step, h).wait()

    def prefetch_half(self, h):
        @pl.when(self.step + 1 < self.n_steps)
        def _():
            self._copy(self.step + 1, h).start()

    def __call__(self, h, k, t):
        return self.e_smem[h].at[k][t - h * self.half]


def _pack_table_kernel(t_ref, o_ref):
    rows = t_ref.shape[1]
    x = t_ref[0]
    bits = lambda v: lax.bitcast_convert_type(v.astype(BF16).astype(F32), jnp.uint32)
    for i in range(ROWS_PER_EXPERT):
        even = bits(x[:, (2 * i) * LANES:(2 * i + 1) * LANES])
        odd = bits(x[:, (2 * i + 1) * LANES:(2 * i + 2) * LANES])
        word = (even >> 16) | (odd & jnp.uint32(0xFFFF0000))
        o_ref[pl.ds(i, rows, stride=ROWS_PER_EXPERT), :] = lax.bitcast_convert_type(word, jnp.int32)


def _pack_table(tabs, layer, rows=512):
    return pl.pallas_call(
        _pack_table_kernel,
        grid=(N_EXPERTS // rows,),
        in_specs=[pl.BlockSpec((1, rows, D_MODEL), lambda i: (layer, i, 0))],
        out_specs=pl.BlockSpec((rows * ROWS_PER_EXPERT, LANES), lambda i: (i, 0)),
        out_shape=jax.ShapeDtypeStruct((N_EXPERTS * ROWS_PER_EXPERT, LANES), jnp.int32),
        compiler_params=_cparams(("arbitrary",)),
        name="pack_table",
    )(tabs)


def _expert_row(tab_ref, offset):
    words = tab_ref[pl.ds(pl.multiple_of(offset, ROWS_PER_EXPERT), ROWS_PER_EXPERT), :]
    return pltpu.bitcast(words, BF16)


def _sublane_sums(ps, roll, where, sub):
    lvl = ps
    for r in (1, 2, 4):
        m = (sub % (2 * r)) < r
        half = len(lvl) // 2
        lvl = [where(m, lvl[i], lvl[i + half]) + roll(where(m, lvl[i + half], lvl[i]), r) for i in range(half)]
    return lvl[0]


def _sum_src():
    sub = np.arange(SUBLANES)[:, None] * np.ones((1, LANES), np.int64)
    ps = [np.full((SUBLANES, LANES), 10.0 ** i) for i in range(SUBLANES)]
    out = _sublane_sums(ps, lambda v, r: np.roll(v, r, axis=0), np.where, sub)
    return [int(round(np.log10(out[s, 0] / SUBLANES))) for s in range(SUBLANES)]


_SUM_SRC = _sum_src()


def _peer_u_kernel(e_hbm, x_ref, tab_ref, gate_ref, w_ref, part_ref, zt_ref, e_smem_a, e_smem_b, sems, *, tb):
    sub = lax.broadcasted_iota(jnp.int32, (SUBLANES, LANES), 0)
    lane = lax.broadcasted_iota(jnp.int32, (N_PAIRS, tb), 1)
    ids = _ExpertIds(e_hbm, (e_smem_a, e_smem_b), sems, tb)
    ids.start_first_block()

    def gather_dots(h, t, slot):
        xt = x_ref[t]
        for grp in range(N_PAIRS // SUBLANES):
            prods = [None] * SUBLANES
            for s in range(SUBLANES):
                prods[_SUM_SRC[s]] = _expert_row(tab_ref, ids(h, grp * SUBLANES + s, t)).astype(F32) * xt
            part_ref[slot, grp * SUBLANES:(grp + 1) * SUBLANES, :] = _sublane_sums(
                prods, lambda v, r: pltpu.roll(v, r, axis=0), jnp.where, sub)

    n_slots = part_ref.shape[0]

    def lane_sums(t0):
        zt = zt_ref[...]
        for j in range(n_slots):
            zt = jnp.where(lane == t0 + j, jnp.sum(part_ref[j], axis=1, keepdims=True), zt)
        zt_ref[...] = zt

    part_ref[...] = jnp.zeros_like(part_ref)
    zt_ref[...] = jnp.zeros_like(zt_ref)

    def some_tokens(h, i, carry):
        lane_sums(n_slots * (i - 1))
        for j in range(n_slots):
            gather_dots(h, n_slots * i + j, j)
        return carry

    for h in range(2):
        ids.wait_half(h)
        lax.fori_loop(h * tb // (2 * n_slots), (h + 1) * tb // (2 * n_slots), functools.partial(some_tokens, h), 0)
        ids.prefetch_half(h)
    lane_sums(tb - n_slots)
    w_ref[...] = jax.nn.gelu(zt_ref[...].T) * gate_ref[...]


def _peer_u(experts_t, x3, tab, gates, tb=PEER_TOKENS):
    T = x3.shape[0]
    vm = lambda: pl.BlockSpec((tb, N_PAIRS), lambda i: (i, 0))
    return pl.pallas_call(
        functools.partial(_peer_u_kernel, tb=tb),
        grid=(T // tb,),
        in_specs=[pl.BlockSpec(memory_space=pl.ANY),
                  pl.BlockSpec((tb, D_ROWS, LANES), lambda i: (i, 0, 0)),
                  pl.BlockSpec((N_EXPERTS * ROWS_PER_EXPERT, LANES), lambda i: (0, 0), pipeline_mode=pl.Buffered(1)),
                  vm()],
        out_specs=vm(),
        out_shape=jax.ShapeDtypeStruct((T, N_PAIRS), F32),
        scratch_shapes=[pltpu.VMEM((4, N_PAIRS, LANES), F32), pltpu.VMEM((N_PAIRS, tb), F32),
                        pltpu.SMEM((N_PAIRS, tb // 2), jnp.int32), pltpu.SMEM((N_PAIRS, tb // 2), jnp.int32),
                        pltpu.SemaphoreType.DMA((2,))],
        compiler_params=_cparams(("arbitrary",)),
        name="peer_u",
    )(experts_t, x3, tab, gates)


def _expand_matrix():
    k = np.arange(N_PAIRS)[:, None]
    c = np.arange(N_PAIRS * D_ROWS)[None, :]
    return (c // D_ROWS == k).astype(np.float32)


def _peer_v_kernel(e_hbm, w_ref, x_ref, tab_ref, expand_ref, g_ref, b_ref, o_ref, y_ref, wide_ref,
                   e_smem_a, e_smem_b, sems, *, tb, alpha):
    n_col = N_PAIRS * D_ROWS
    ids = _ExpertIds(e_hbm, (e_smem_a, e_smem_b), sems, tb)
    ids.start_first_block()
    diag = (lax.broadcasted_iota(jnp.int32, (D_ROWS, n_col), 1) % D_ROWS
            == lax.broadcasted_iota(jnp.int32, (D_ROWS, n_col), 0))
    part_row = lax.broadcasted_iota(jnp.int32, (SUBLANES, LANES), 0)

    def expand_tokens(i, carry):
        t0 = pl.multiple_of(i * TOKENS_PER_MATMUL, TOKENS_PER_MATMUL)
        parts = []
        for j in range(TOKENS_PER_MATMUL):
            wrow = w_ref[pl.ds(t0 + j, 1), :]
            hi = wrow.astype(BF16).astype(F32)
            mid = (wrow - hi).astype(BF16).astype(F32)
            lo = ((wrow - hi) - mid).astype(BF16).astype(F32)
            bc = lambda v: jnp.broadcast_to(v, (SUBLANES, LANES))
            parts.append(jnp.where(part_row == 0, bc(hi), jnp.where(part_row == 1, bc(mid),
                                                                    jnp.where(part_row == 2, bc(lo), 0.0))))
        lhs = jnp.concatenate(parts, axis=0).astype(BF16)
        wide = jnp.dot(lhs, expand_ref[...], preferred_element_type=F32)
        wide_ref[pl.ds(t0, TOKENS_PER_MATMUL)] = wide.reshape(TOKENS_PER_MATMUL, SUBLANES, n_col)
        return carry

    lax.fori_loop(0, tb // TOKENS_PER_MATMUL, expand_tokens, 0)

    def left_operand(t):
        rows = [jnp.where(diag, jnp.broadcast_to(wide_ref[t, p:p + 1, :], (D_ROWS, n_col)), 0.0) for p in range(3)]
        return jnp.concatenate(rows, axis=0).astype(BF16)

    def weighted_sum(h, t, a):
        rhs = jnp.concatenate([_expert_row(tab_ref, ids(h, k, t)) for k in range(N_PAIRS)], axis=0)
        y3 = jnp.dot(a, rhs, preferred_element_type=F32)
        return (y3[0:D_ROWS] + y3[D_ROWS:2 * D_ROWS]) + y3[2 * D_ROWS:]

    tokens_per_step = 32

    def some_tokens(h, i, carry):
        ts = [tokens_per_step * i + j for j in range(tokens_per_step)]
        ys = [weighted_sum(h, t, left_operand(t)) for t in ts]
        for t, y in zip(ts, ys):
            y_ref[t] = y
        return carry

    for h in range(2):
        ids.wait_half(h)
        lax.fori_loop(h * tb // (2 * tokens_per_step), (h + 1) * tb // (2 * tokens_per_step),
                      functools.partial(some_tokens, h), 0)
        ids.prefetch_half(h)
    z = alpha * x_ref[...] + y_ref[...]
    inv_d = 1.0 / D_MODEL
    mu = jnp.sum(jnp.sum(z, axis=2, keepdims=True), axis=1, keepdims=True) * inv_d
    zc = z - mu
    var = jnp.sum(jnp.sum(zc * zc, axis=2, keepdims=True), axis=1, keepdims=True) * inv_d
    o_ref[...] = zc * lax.rsqrt(var + LN_EPS) * g_ref[...] + b_ref[...]


def _peer_v(experts_t, wgt, x3, tab, g, b, alpha, tb=PEER_TOKENS):
    T = x3.shape[0]
    return pl.pallas_call(
        functools.partial(_peer_v_kernel, tb=tb, alpha=alpha),
        grid=(T // tb,),
        in_specs=[pl.BlockSpec(memory_space=pl.ANY),
                  pl.BlockSpec((tb, N_PAIRS), lambda i: (i, 0)),
                  pl.BlockSpec((tb, D_ROWS, LANES), lambda i: (i, 0, 0)),
                  pl.BlockSpec((N_EXPERTS * ROWS_PER_EXPERT, LANES), lambda i: (0, 0), pipeline_mode=pl.Buffered(1)),
                  pl.BlockSpec((N_PAIRS, N_PAIRS * D_ROWS), lambda i: (0, 0)),
                  pl.BlockSpec((1, D_ROWS, LANES), lambda i: (0, 0, 0)),
                  pl.BlockSpec((1, D_ROWS, LANES), lambda i: (0, 0, 0))],
        out_specs=pl.BlockSpec((tb, D_ROWS, LANES), lambda i: (i, 0, 0)),
        out_shape=jax.ShapeDtypeStruct((T, D_ROWS, LANES), F32),
        scratch_shapes=[pltpu.VMEM((tb, D_ROWS, LANES), F32), pltpu.VMEM((tb, SUBLANES, N_PAIRS * D_ROWS), F32),
                        pltpu.SMEM((N_PAIRS, tb // 2), jnp.int32), pltpu.SMEM((N_PAIRS, tb // 2), jnp.int32),
                        pltpu.SemaphoreType.DMA((2,))],
        compiler_params=_cparams(("arbitrary",)),
        name="peer_v",
    )(experts_t, wgt, x3, tab, jnp.asarray(_expand_matrix(), dtype=BF16),
      g.reshape(1, D_ROWS, LANES), b.reshape(1, D_ROWS, LANES))


def kernel(x, w_in, b_in, conv_w, conv_b, lru_wa, lru_ba, lru_wx, lru_bx, lru_lambda, cmp_pos_k, cmpk_w1, cmpk_b1, cmpk_w2, cmpk_b2, cmp_pos_v, cmpv_w1, cmpv_b1, cmpv_w2, cmpv_b2, gn_lru_g, gn_nsa_g, w_out, ln1_g, ln1_b, peer_wq, peer_subkeys, peer_u, peer_v, ln2_g, ln2_b):
    B, S, D = x.shape
    T = B * S
    depth = w_in.shape[0]
    alpha = (2 * depth) ** 0.25
    xf = x.reshape(T, D)
    for l in range(depth):
        w_in_bf = jnp.pad(w_in[l], ((0, 0), (0, IN_COLS_PAD - IN_COLS))).astype(BF16)
        b_in_row = jnp.pad(b_in[l], (0, IN_COLS_PAD - IN_COLS)).reshape(1, IN_COLS_PAD)
        h_all, kraw, kbf, vtiles = _in_proj(xf, w_in_bf, b_in_row, B, S)

        y_lru = _rglru(h_all, B, S, conv_w[l], conv_b[l], _block_diag(lru_wa[l]), lru_ba[l],
                       _block_diag(lru_wx[l]), lru_bx[l], lru_lambda[l], gn_lru_g[l])

        w1 = jnp.stack([cmpk_w1[l], cmpv_w1[l]])
        w1_pos = w1.reshape(2, 2, CMP_STRIDE, HEAD_DIM, HEAD_DIM)
        w1l = jnp.concatenate([w1_pos[:, 0], w1_pos[:, 1]], axis=-1)
        pos = jnp.stack([cmp_pos_k[l], cmp_pos_v[l]]).reshape(2, 1, CMP_BLOCK * HEAD_DIM)
        posflat = jnp.broadcast_to(pos, (2, SUBLANES, CMP_BLOCK * HEAD_DIM))
        kcvc = _compress(kraw, w1l, w1, posflat,
                         jnp.stack([cmpk_b1[l], cmpv_b1[l]]).reshape(2, 1, HEAD_DIM),
                         jnp.stack([cmpk_w2[l], cmpv_w2[l]]),
                         jnp.stack([cmpk_b2[l], cmpv_b2[l]]).reshape(2, 1, HEAD_DIM))
        kcvc = kcvc.reshape(2, B, N_KV, S // CMP_STRIDE, HEAD_DIM)
        y_nsa = _nsa(h_all, B, S, kcvc[0], kcvc[1], kbf[0], vtiles[0], kbf[1], vtiles[1], gn_nsa_g[l])

        x1 = _out_ln(y_lru, y_nsa, xf, w_out[l].astype(BF16), ln1_g[l], ln1_b[l], alpha)

        experts_t, gates = _peer_route(x1, peer_wq[l].astype(BF16), peer_subkeys[l])
        x3 = x1.reshape(T, D_ROWS, LANES)
        wgt = _peer_u(experts_t, x3, _pack_table(peer_u, l), gates)
        xf = _peer_v(experts_t, wgt, x3, _pack_table(peer_v, l), ln2_g[l], ln2_b[l], alpha).reshape(T, D)
    return xf.reshape(B, S, D)
```
